```python
import math
import jax, jax.numpy as jnp
from jax import lax
import numpy as np

D_MODEL = 1024
BATCH = 8
SEQ = 2048
DEPTH = 2

HEAD_DIM = 64
BLK = 128
H_SB = 4
DIL_PATTERNS = ((128, 1), (512, 4), (2048, 16))
H_PER_DIL = 2
H_DIL = H_PER_DIL * len(DIL_PATTERNS)
H_SWA_Q = 6
H_SWA_KV = 2
SWA_WINDOW = 128
N_BUCKETS = 32
MAX_REL_DIST = 2048
N_SOFT_HEADS = H_DIL + H_SWA_Q
D_FF = 2816
RMS_EPS = 1e-6
ADA_CHUNKS = 9
IN_SPLITS = (H_SB * HEAD_DIM, H_SB * HEAD_DIM, H_SB * HEAD_DIM,
             H_DIL * HEAD_DIM, H_DIL * HEAD_DIM, H_DIL * HEAD_DIM,
             H_SWA_Q * HEAD_DIM, H_SWA_KV * HEAD_DIM, H_SWA_KV * HEAD_DIM,
             D_MODEL, D_MODEL, D_MODEL)
D_IN = sum(IN_SPLITS)

kernel_name = 'hybrid_sb_dilated_swa_macaron'


def rmsnorm(x, g):
    xf = x.astype(jnp.float32)
    y = xf * lax.rsqrt(jnp.mean(xf * xf, axis=-1, keepdims=True) + RMS_EPS)
    return (y * g.astype(jnp.float32)).astype(x.dtype)


def swiglu(h, wg, wu, wd):
    return (jax.nn.silu(h @ wg) * (h @ wu)) @ wd


def t5_bucket(n):
    max_exact = N_BUCKETS // 2
    nf = jnp.maximum(n, 1).astype(jnp.float32)
    large = max_exact + (jnp.log(nf / max_exact) / math.log(MAX_REL_DIST / max_exact)
                         * (N_BUCKETS - max_exact)).astype(jnp.int32)
    large = jnp.minimum(large, N_BUCKETS - 1)
    return jnp.where(n < max_exact, n, large)


def band_bias(table_cols, dilation):
    rel = jnp.arange(BLK)[:, None] + BLK - jnp.arange(2 * BLK)[None, :]
    b = t5_bucket(jnp.maximum(rel, 0) * dilation)
    return jnp.transpose(table_cols[b], (2, 0, 1)).astype(jnp.float32)


def banded_attention(q, k, v, bias, max_dist, sinks=None):
    N, L, Hk, G, Dh = q.shape
    nb = L // BLK
    qb = q.reshape(N, nb, BLK, Hk, G, Dh).astype(jnp.float32)

    def band(t):
        tb = t.reshape(N, nb, BLK, Hk, Dh).astype(jnp.float32)
        prev = jnp.pad(tb, ((0, 0), (1, 0), (0, 0), (0, 0), (0, 0)))[:, :-1]
        return jnp.concatenate([prev, tb], axis=2)

    kk, vv = band(k), band(v)
    s = jnp.einsum('nbqhgd,nbkhd->nbhgqk', qb, kk) * (Dh ** -0.5) + bias
    rel = jnp.arange(BLK)[:, None] + BLK - jnp.arange(2 * BLK)[None, :]
    in_band = (rel >= 0) & (rel <= max_dist)
    key_pos = jnp.arange(nb)[:, None] * BLK - BLK + jnp.arange(2 * BLK)[None, :]
    mask = in_band[None] & (key_pos >= 0)[:, None, :]
    s = jnp.where(mask[None, :, None, None], s, -jnp.inf)
    m = jnp.max(s, axis=-1)
    if sinks is not None:
        sk = sinks.astype(jnp.float32)[None, None, :, :, None]
        m = jnp.maximum(m, sk)
    p = jnp.exp(s - m[..., None])
    denom = jnp.sum(p, axis=-1)
    if sinks is not None:
        denom = denom + jnp.exp(sk - m)
    o = jnp.einsum('nbhgqk,nbkhd->nbqhgd', p, vv) / jnp.transpose(denom, (0, 1, 4, 2, 3))[..., None]
    lse = jnp.transpose(m + jnp.log(denom), (0, 1, 4, 2, 3))
    return o.reshape(N, L, Hk, G, Dh), lse.reshape(N, L, Hk, G)


def stick_breaking_mixer(q, k, v):
    Bn, S, H, Dh = q.shape
    nb = S // BLK
    kf, vf = k.astype(jnp.float32), v.astype(jnp.float32)
    qb = q.astype(jnp.float32).reshape(Bn, nb, BLK, H, Dh).transpose(1, 0, 2, 3, 4)
    key_pos = jnp.arange(S)

    def block(args):
        qblk, i = args
        z = jnp.einsum('bqhd,bkhd->bhqk', qblk, kf) * (Dh ** -0.5)
        q_pos = i * BLK + jnp.arange(BLK)
        before = key_pos[None, :] < q_pos[:, None]
        log_fail = jnp.where(before, jax.nn.log_sigmoid(-z), 0.0)
        between = lax.cumsum(log_fail, axis=3, reverse=True) - log_fail
        w = jnp.where(before, jnp.exp(jax.nn.log_sigmoid(z) + between), 0.0)
        return jnp.einsum('bhqk,bkhd->bqhd', w, vf)

    o = lax.map(block, (qb, jnp.arange(nb)))
    return o.transpose(1, 0, 2, 3, 4).reshape(Bn, S, H, Dh)


def dilated_mixer(q, k, v, rel_table):
    Bn, S = q.shape[:2]
    outs, lses = [], []
    for g, (w, d) in enumerate(DIL_PATTERNS):
        hs = slice(g * H_PER_DIL, (g + 1) * H_PER_DIL)
        Ls = S // d
        Lp = -(-Ls // BLK) * BLK

        def sub(t):
            t = t[:, :, hs].reshape(Bn, Ls, d, H_PER_DIL, HEAD_DIM).transpose(0, 2, 1, 3, 4)
            t = t.reshape(Bn * d, Ls, H_PER_DIL, HEAD_DIM)
            return jnp.pad(t, ((0, 0), (0, Lp - Ls), (0, 0), (0, 0)))

        bias = band_bias(rel_table[:, hs], d)[:, None]
        o, lse = banded_attention(sub(q)[:, :, :, None], sub(k), sub(v), bias, w // d)
        o = o[:, :Ls, :, 0].reshape(Bn, d, Ls, H_PER_DIL, HEAD_DIM).transpose(0, 2, 1, 3, 4)
        lse = lse[:, :Ls, :, 0].reshape(Bn, d, Ls, H_PER_DIL).transpose(0, 2, 1, 3)
        outs.append(o.reshape(Bn, S, H_PER_DIL, HEAD_DIM))
        lses.append(lse.reshape(Bn, S, H_PER_DIL))
    alpha = jax.nn.softmax(jnp.stack(lses), axis=0)
    return jnp.sum(alpha[..., None] * jnp.stack(outs), axis=0)


def swa_mixer(q, k, v, rel_table, sinks):
    Bn, S = q.shape[:2]
    G = H_SWA_Q // H_SWA_KV
    bias = band_bias(rel_table[:, H_DIL:], 1).reshape(H_SWA_KV, G, BLK, 2 * BLK)
    o, _ = banded_attention(q.reshape(Bn, S, H_SWA_KV, G, HEAD_DIM), k, v, bias,
                            SWA_WINDOW - 1, sinks.reshape(H_SWA_KV, G))
    return o.reshape(Bn, S, H_SWA_Q, HEAD_DIM)


def setup_inputs(seed: int = 0) -> dict:
    key = jax.random.key(seed)
    ks = jax.random.split(key, 18)
    f32 = jnp.float32
    nrm = lambda k, shape, scale: jax.random.normal(k, shape, f32) * scale
    return {
        'x': nrm(ks[0], (BATCH, SEQ, D_MODEL), 1.0),
        'c': nrm(ks[1], (BATCH, D_MODEL), 1.0),
        'w_ada': nrm(ks[2], (DEPTH, D_MODEL, ADA_CHUNKS * D_MODEL), 0.5 * D_MODEL ** -0.5),
        'b_ada': nrm(ks[3], (DEPTH, ADA_CHUNKS * D_MODEL), 0.02),
        'norm_gain': 1.0 + nrm(ks[4], (DEPTH, 3, D_MODEL), 0.05),
        'w_ffn_gate': nrm(ks[5], (DEPTH, 2, D_MODEL, D_FF), D_MODEL ** -0.5),
        'w_ffn_up': nrm(ks[6], (DEPTH, 2, D_MODEL, D_FF), D_MODEL ** -0.5),
        'w_ffn_down': nrm(ks[7], (DEPTH, 2, D_FF, D_MODEL), D_FF ** -0.5),
        'w_in': nrm(ks[8], (DEPTH, D_MODEL, D_IN), D_MODEL ** -0.5),
        'w_br_sb': nrm(ks[9], (DEPTH, H_SB * HEAD_DIM, D_MODEL), (H_SB * HEAD_DIM) ** -0.5),
        'w_br_dil': nrm(ks[10], (DEPTH, H_PER_DIL * HEAD_DIM, D_MODEL), (H_PER_DIL * HEAD_DIM) ** -0.5),
        'w_br_swa': nrm(ks[11], (DEPTH, H_SWA_Q * HEAD_DIM, D_MODEL), (H_SWA_Q * HEAD_DIM) ** -0.5),
        'w_out': nrm(ks[12], (DEPTH, D_MODEL, D_MODEL), D_MODEL ** -0.5),
        'sinks': nrm(ks[13], (DEPTH, H_SWA_Q), 0.5),
        'rel_bias': nrm(ks[14], (N_BUCKETS, N_SOFT_HEADS), 0.5),
        'final_gain': 1.0 + nrm(ks[15], (D_MODEL,), 0.05),
    }


def reference(x, c, w_ada, b_ada, norm_gain, w_ffn_gate, w_ffn_up, w_ffn_down, w_in,
              w_br_sb, w_br_dil, w_br_swa, w_out, sinks, rel_bias, final_gain):
    Bn, S, _ = x.shape
    split_idx = np.cumsum(IN_SPLITS)[:-1].tolist()
    heads = lambda t: t.reshape(Bn, S, -1, HEAD_DIM)
    for l in range(DEPTH):
        mod = (jax.nn.silu(c) @ w_ada[l] + b_ada[l]).reshape(Bn, 3, 3, D_MODEL)[:, :, :, None, :]

        def pre(xx, j):
            return rmsnorm(xx, norm_gain[l, j]) * (1 + mod[:, j, 1]) + mod[:, j, 0]

        h = pre(x, 0)
        x = x + 0.5 * mod[:, 0, 2] * swiglu(h, w_ffn_gate[l, 0], w_ffn_up[l, 0], w_ffn_down[l, 0])

        h = pre(x, 1)
        (q_sb, k_sb, v_sb, q_dil, k_dil, v_dil, q_swa, k_swa, v_swa,
         g_sb, g_dil, g_swa) = jnp.split(h @ w_in[l], split_idx, axis=-1)
        o_sb = stick_breaking_mixer(heads(q_sb), heads(k_sb), heads(v_sb)).reshape(Bn, S, -1).astype(x.dtype)
        o_dil = dilated_mixer(heads(q_dil), heads(k_dil), heads(v_dil), rel_bias).reshape(Bn, S, -1).astype(x.dtype)
        o_swa = swa_mixer(heads(q_swa), heads(k_swa), heads(v_swa), rel_bias, sinks[l]).reshape(Bn, S, -1).astype(x.dtype)
        merged = (jax.nn.sigmoid(g_sb) * (o_sb @ w_br_sb[l])
                  + jax.nn.sigmoid(g_dil) * (o_dil @ w_br_dil[l])
                  + jax.nn.sigmoid(g_swa) * (o_swa @ w_br_swa[l]))
        x = x + mod[:, 1, 2] * (merged @ w_out[l])

        h = pre(x, 2)
        x = x + 0.5 * mod[:, 2, 2] * swiglu(h, w_ffn_gate[l, 1], w_ffn_up[l, 1], w_ffn_down[l, 1])
    return rmsnorm(x, final_gain)
```

```python
import functools
import math

import jax
import jax.numpy as jnp
from jax import lax
from jax.experimental import pallas as pl
from jax.experimental.pallas import tpu as pltpu

F32 = jnp.float32
BF16 = jnp.bfloat16

HEAD_DIM = 64
LANES = 128
BLK = 128
H_SB = 4
DIL_PATTERNS = ((128, 1), (512, 4), (2048, 16))
H_PER_DIL = 2
H_DIL = H_PER_DIL * len(DIL_PATTERNS)
H_SWA_Q = 6
H_SWA_KV = 2
SWA_WINDOW = 128
N_BUCKETS = 32
MAX_REL_DIST = 2048
RMS_EPS = 1e-6
ADA_CHUNKS = 9
NEG_BIG = -1e30

VMEM_LIMIT = 56 * 1024 * 1024
ROW_TILE = 512
FF_CHUNK = 256
ADA_TILE = 1024


def _cparams(n_axes):
    return pltpu.CompilerParams(dimension_semantics=("parallel",) * n_axes,
                                vmem_limit_bytes=VMEM_LIMIT)


def _const_spec(shape):
    nd = len(shape)
    return pl.BlockSpec(shape, lambda *_: (0,) * nd, pipeline_mode=pl.Buffered(1))


def _ada_kernel(c_ref, w_ref, b_ref, o_ref):
    c = c_ref[...]
    sc = (c * jax.nn.sigmoid(c)).astype(BF16)
    o_ref[0] = jnp.dot(sc, w_ref[0].astype(BF16), preferred_element_type=F32) + b_ref[0]


def _ada(c, w_ada, b_ada):
    depth, d, n = w_ada.shape
    bn = c.shape[0]
    return pl.pallas_call(
        _ada_kernel,
        grid=(depth, n // ADA_TILE),
        in_specs=[pl.BlockSpec((bn, d), lambda l, j: (0, 0)),
                  pl.BlockSpec((1, d, ADA_TILE), lambda l, j: (l, 0, j)),
                  pl.BlockSpec((1, 1, ADA_TILE), lambda l, j: (l, 0, j))],
        out_specs=pl.BlockSpec((1, bn, ADA_TILE), lambda l, j: (l, 0, j)),
        out_shape=jax.ShapeDtypeStruct((depth, bn, n), F32),
        compiler_params=_cparams(2),
        name="ada_mod",
    )(c, w_ada, b_ada.reshape(depth, 1, n))


def _rms(x, gain):
    ms = jnp.mean(x * x, axis=-1, keepdims=True)
    return x * lax.rsqrt(ms + RMS_EPS) * gain


def _prenorm(x, gain, mod, j):
    shift = mod[3 * j:3 * j + 1]
    scale = mod[3 * j + 1:3 * j + 2]
    return _rms(x, gain) * (1.0 + scale) + shift


def _ffn_kernel(*refs, j, final):
    if final:
        x_ref, mod_ref, gain_ref, wg_ref, wu_ref, wd_ref, fg_ref, o_ref = refs
    else:
        x_ref, mod_ref, gain_ref, wg_ref, wu_ref, wd_ref, o_ref = refs
    x = x_ref[0]
    mod = mod_ref[0]
    h = _prenorm(x, gain_ref[...], mod, j).astype(BF16)
    d_ff = wg_ref.shape[1]
    acc = jnp.zeros(x.shape, F32)
    for ci in range(d_ff // FF_CHUNK):
        cols = slice(ci * FF_CHUNK, (ci + 1) * FF_CHUNK)
        g = jnp.dot(h, wg_ref[:, cols], preferred_element_type=F32)
        u = jnp.dot(h, wu_ref[:, cols], preferred_element_type=F32)
        a = (g * jax.nn.sigmoid(g) * u).astype(BF16)
        acc = acc + jnp.dot(a, wd_ref[cols, :], preferred_element_type=F32)
    y = x + (0.5 * mod[3 * j + 2:3 * j + 3]) * acc
    if final:
        y = _rms(y, fg_ref[...])
    o_ref[0] = y


def _ffn(x, mod, gain, wg, wu, wd, j, final_gain=None):
    bn, s, d = x.shape
    final = final_gain is not None
    in_specs = [pl.BlockSpec((1, ROW_TILE, d), lambda b, i: (b, i, 0)),
                pl.BlockSpec((1, ADA_CHUNKS, d), lambda b, i: (b, 0, 0)),
                _const_spec((1, d)),
                _const_spec(wg.shape), _const_spec(wu.shape), _const_spec(wd.shape)]
    args = [x, mod, gain.reshape(1, d), wg, wu, wd]
    if final:
        in_specs.append(_const_spec((1, d)))
        args.append(final_gain.reshape(1, d))
    return pl.pallas_call(
        functools.partial(_ffn_kernel, j=j, final=final),
        grid=(bn, s // ROW_TILE),
        in_specs=in_specs,
        out_specs=pl.BlockSpec((1, ROW_TILE, d), lambda b, i: (b, i, 0)),
        out_shape=jax.ShapeDtypeStruct(x.shape, F32),
        compiler_params=_cparams(2),
        name="ffn_final" if final else f"ffn{j}",
    )(*args)


N_SB = 3 * H_SB * HEAD_DIM
N_DIL = 3 * H_DIL * HEAD_DIM
N_SWA = (H_SWA_Q + 2 * H_SWA_KV) * HEAD_DIM
N_QKV = N_SB + N_DIL + N_SWA
N_DIL_SLABS = N_DIL // LANES


def _inproj_kernel(x_ref, mod_ref, gain_ref, w_ref, sb_ref, dil_ref, swa_ref):
    h = _prenorm(x_ref[0], gain_ref[...], mod_ref[0], 1).astype(BF16)
    r = jnp.dot(h, w_ref[...], preferred_element_type=F32)
    qs = HEAD_DIM ** -0.5
    n_q_sb = H_SB * HEAD_DIM
    sb_ref[0, :, :n_q_sb] = (r[:, :n_q_sb] * qs).astype(BF16)
    sb_ref[0, :, n_q_sb:] = r[:, n_q_sb:N_SB].astype(BF16)
    for t in range(N_DIL_SLABS):
        slab = r[:, N_SB + t * LANES:N_SB + (t + 1) * LANES]
        dil_ref[0, t] = slab * qs if t % 3 == 0 else slab
    n_q_swa = H_SWA_Q * HEAD_DIM
    o = N_SB + N_DIL
    swa_ref[0, :, :n_q_swa] = (r[:, o:o + n_q_swa] * qs).astype(BF16)
    swa_ref[0, :, n_q_swa:] = r[:, o + n_q_swa:].astype(BF16)


def _inproj(x, mod, gain, w_qkv):
    bn, s, d = x.shape
    return pl.pallas_call(
        _inproj_kernel,
        grid=(bn, s // ROW_TILE),
        in_specs=[pl.BlockSpec((1, ROW_TILE, d), lambda b, i: (b, i, 0)),
                  pl.BlockSpec((1, ADA_CHUNKS, d), lambda b, i: (b, 0, 0)),
                  _const_spec((1, d)),
                  _const_spec(w_qkv.shape)],
        out_specs=[pl.BlockSpec((1, ROW_TILE, N_SB), lambda b, i: (b, i, 0)),
                   pl.BlockSpec((1, N_DIL_SLABS, ROW_TILE, LANES), lambda b, i: (b, 0, i, 0)),
                   pl.BlockSpec((1, ROW_TILE, N_SWA), lambda b, i: (b, i, 0))],
        out_shape=[jax.ShapeDtypeStruct((bn, s, N_SB), BF16),
                   jax.ShapeDtypeStruct((bn, N_DIL_SLABS, s, LANES), F32),
                   jax.ShapeDtypeStruct((bn, s, N_SWA), BF16)],
        compiler_params=_cparams(2),
        name="in_proj",
    )(x, mod, gain.reshape(1, d), w_qkv)


def _dot_nt(a, b):
    return lax.dot_general(a, b, (((1,), (1,)), ((), ())), preferred_element_type=F32)


def _sb_kernel(q_ref, k_ref, v_ref, o_ref, acc_ref, c_ref):
    s_len = q_ref.shape[1]
    nq = s_len // BLK
    row = lax.broadcasted_iota(jnp.int32, (BLK, BLK), 0)
    col = lax.broadcasted_iota(jnp.int32, (BLK, BLK), 1)
    head0 = col < HEAD_DIM
    strict = col < row
    r2 = lax.broadcasted_iota(jnp.int32, (BLK, 2 * BLK), 0)
    c2 = lax.broadcasted_iota(jnp.int32, (BLK, 2 * BLK), 1)
    suffix_ones = jnp.where((r2 >= c2) | (c2 >= BLK), 1.0, 0.0).astype(BF16)

    def tile(qh, kb, vb, c, diag):
        z = _dot_nt(qh, kb)
        log_fail = -(jnp.maximum(z, 0.0) + jnp.log(1.0 + jnp.exp(-jnp.abs(z))))
        if diag:
            log_fail = jnp.where(strict, log_fail, 0.0)
        hi = log_fail.astype(BF16)
        lo = (log_fail - hi.astype(F32)).astype(BF16)
        ct = (jnp.dot(hi, suffix_ones, preferred_element_type=F32)
              + jnp.dot(lo, suffix_ones, preferred_element_type=F32))
        w = jnp.exp(z + ct[:, :BLK] + c)
        if diag:
            w = jnp.where(strict, w, 0.0)
        pv = jnp.dot(w.astype(BF16), vb, preferred_element_type=F32)
        return pv, c + ct[:, BLK:]

    def qblock(i, carry):
        q0 = pl.multiple_of(i * BLK, BLK)
        q = q_ref[0, pl.ds(q0, BLK), :]
        zero = jnp.zeros_like(q)
        qh = (jnp.where(head0, q, zero), jnp.where(head0, zero, q))

        def sweep(k0, diag):
            kb = k_ref[0, pl.ds(k0, BLK), :]
            vb = v_ref[0, pl.ds(k0, BLK), :]
            pv0, c0 = tile(qh[0], kb, vb, c_ref[0], diag)
            pv1, c1 = tile(qh[1], kb, vb, c_ref[1], diag)
            c_ref[0] = c0
            c_ref[1] = c1
            acc_ref[...] += jnp.where(head0, pv0, pv1)

        acc_ref[...] = jnp.zeros_like(acc_ref)
        c_ref[...] = jnp.zeros_like(c_ref)
        sweep(q0, True)

        def kstep(jj, carry2):
            sweep(pl.multiple_of((i - 1 - jj) * BLK, BLK), False)
            return carry2

        lax.fori_loop(0, i, kstep, 0)
        o_ref[0, pl.ds(q0, BLK), :] = acc_ref[...].astype(o_ref.dtype)
        return carry

    lax.fori_loop(0, nq, qblock, 0)


def _sb_attention(qkv_sb):
    bn, s, _ = qkv_sb.shape
    n_pairs = H_SB // 2
    spec = lambda off: pl.BlockSpec((1, s, LANES), lambda b, p: (b, 0, off + p))
    return pl.pallas_call(
        _sb_kernel,
        grid=(bn, n_pairs),
        in_specs=[spec(0), spec(n_pairs), spec(2 * n_pairs)],
        out_specs=pl.BlockSpec((1, s, LANES), lambda b, p: (b, 0, p)),
        out_shape=jax.ShapeDtypeStruct((bn, s, n_pairs * LANES), BF16),
        scratch_shapes=[pltpu.VMEM((BLK, LANES), F32), pltpu.VMEM((2, BLK, LANES), F32)],
        compiler_params=_cparams(2),
        name="sb_attn",
    )(qkv_sb, qkv_sb, qkv_sb)


def _band_mask(max_dist, block_idx):
    row = lax.broadcasted_iota(jnp.int32, (BLK, 2 * BLK), 0)
    col = lax.broadcasted_iota(jnp.int32, (BLK, 2 * BLK), 1)
    rel = row + BLK - col
    first_valid_col = jnp.where(block_idx > 0, 0, BLK)
    return (rel >= 0) & (rel <= max_dist) & (col >= first_valid_col)


def _band_pair(q, kcat, vcat, bias_ref, mask, sinks):
    lane = lax.broadcasted_iota(jnp.int32, (BLK, LANES), 1)
    head0 = lane < HEAD_DIM
    zero = jnp.zeros_like(q)
    outs, lses = [], []
    for h in range(2):
        qh = jnp.where(head0, q, zero) if h == 0 else jnp.where(head0, zero, q)
        s = jnp.where(mask, _dot_nt(qh, kcat) + bias_ref[h], NEG_BIG)
        m = jnp.max(s, axis=-1, keepdims=True)
        if sinks is not None:
            m = jnp.maximum(m, sinks[h])
        p = jnp.exp(s - m)
        den = jnp.sum(p, axis=-1, keepdims=True)
        if sinks is not None:
            den = den + jnp.exp(sinks[h] - m)
        o = jnp.dot(p.astype(BF16), vcat, preferred_element_type=F32)
        outs.append(o * (1.0 / den))
        lses.append(m + jnp.log(den))
    out = jnp.where(head0, outs[0], outs[1])
    lse = jnp.where(head0, lses[0], lses[1])
    return out, lse


def _dil_kernel(qkv_ref, bias_ref, o_ref, og_ref, lg_ref):
    s_len = o_ref.shape[1]
    n_blocks = s_len // BLK
    for g, (window, d) in enumerate(DIL_PATTERNS):
        nb = s_len // d // BLK

        def rows(start):
            return pl.ds(start, BLK) if d == 1 else pl.ds(start, BLK, stride=d)

        def body(idx, carry):
            r, i = idx // nb, idx % nb
            start = r + d * BLK * i
            q = qkv_ref[0, 3 * g, rows(start), :].astype(BF16)
            k_own = qkv_ref[0, 3 * g + 1, rows(start), :]
            v_own = qkv_ref[0, 3 * g + 2, rows(start), :]
            prev = r + d * BLK * jnp.maximum(i - 1, 0)
            k_prev = qkv_ref[0, 3 * g + 1, rows(prev), :]
            v_prev = qkv_ref[0, 3 * g + 2, rows(prev), :]
            kcat = jnp.concatenate([k_prev, k_own], axis=0).astype(BF16)
            vcat = jnp.concatenate([v_prev, v_own], axis=0).astype(BF16)
            mask = _band_mask(window // d, i)
            out, lse = _band_pair(q, kcat, vcat, bias_ref.at[g], mask, None)
            og_ref[g, rows(start), :] = out
            lg_ref[g, rows(start), :] = lse
            return carry

        lax.fori_loop(0, n_blocks, body, 0)

    def merge(i, carry):
        sl = pl.ds(pl.multiple_of(i * BLK, BLK), BLK)
        l0, l1, l2 = lg_ref[0, sl, :], lg_ref[1, sl, :], lg_ref[2, sl, :]
        m = jnp.maximum(jnp.maximum(l0, l1), l2)
        e0, e1, e2 = jnp.exp(l0 - m), jnp.exp(l1 - m), jnp.exp(l2 - m)
        num = e0 * og_ref[0, sl, :] + e1 * og_ref[1, sl, :] + e2 * og_ref[2, sl, :]
        o_ref[0, sl, :] = (num / (e0 + e1 + e2)).astype(o_ref.dtype)
        return carry

    lax.fori_loop(0, n_blocks, merge, 0)


def _dil_attention(qkv_dil, bias_dil):
    bn, n_slabs, s, _ = qkv_dil.shape
    n_groups = len(DIL_PATTERNS)
    return pl.pallas_call(
        _dil_kernel,
        grid=(bn,),
        in_specs=[pl.BlockSpec((1, n_slabs, s, LANES), lambda b: (b, 0, 0, 0)),
                  _const_spec(bias_dil.shape)],
        out_specs=pl.BlockSpec((1, s, LANES), lambda b: (b, 0, 0)),
        out_shape=jax.ShapeDtypeStruct((bn, s, LANES), BF16),
        scratch_shapes=[pltpu.VMEM((n_groups, s, LANES), F32), pltpu.VMEM((n_groups, s, LANES), F32)],
        compiler_params=_cparams(1),
        name="dil_attn",
    )(qkv_dil, bias_dil)


N_SWA_PAIRS = H_SWA_Q // 2


def _swa_kernel(sink_ref, qkv_ref, bias_ref, o_ref):
    s_len = o_ref.shape[1]
    k_col = N_SWA_PAIRS * LANES
    v_col = k_col + LANES

    def body(i, carry):
        start = pl.multiple_of(i * BLK, BLK)
        prev = pl.multiple_of(jnp.maximum(i - 1, 0) * BLK, BLK)
        kcat = jnp.concatenate([qkv_ref[0, pl.ds(prev, BLK), k_col:k_col + LANES],
                                qkv_ref[0, pl.ds(start, BLK), k_col:k_col + LANES]], axis=0)
        vcat = jnp.concatenate([qkv_ref[0, pl.ds(prev, BLK), v_col:v_col + LANES],
                                qkv_ref[0, pl.ds(start, BLK), v_col:v_col + LANES]], axis=0)
        mask = _band_mask(SWA_WINDOW - 1, i)
        for p in range(N_SWA_PAIRS):
            q = qkv_ref[0, pl.ds(start, BLK), p * LANES:(p + 1) * LANES]
            sinks = (sink_ref[2 * p], sink_ref[2 * p + 1])
            out, _ = _band_pair(q, kcat, vcat, bias_ref.at[p], mask, sinks)
            o_ref[0, pl.ds(start, BLK), p * LANES:(p + 1) * LANES] = out.astype(o_ref.dtype)
        return carry

    lax.fori_loop(0, s_len // BLK, body, 0)


def _swa_attention(qkv_swa, bias_swa, sinks_pairs):
    bn, s, n = qkv_swa.shape
    return pl.pallas_call(
        _swa_kernel,
        grid_spec=pltpu.PrefetchScalarGridSpec(
            num_scalar_prefetch=1,
            grid=(bn,),
            in_specs=[pl.BlockSpec((1, s, n), lambda b, sk: (b, 0, 0)),
                      pl.BlockSpec(bias_swa.shape, lambda b, sk: (0, 0, 0, 0))],
            out_specs=pl.BlockSpec((1, s, N_SWA_PAIRS * LANES), lambda b, sk: (b, 0, 0)),
        ),
        out_shape=jax.ShapeDtypeStruct((bn, s, N_SWA_PAIRS * LANES), BF16),
        compiler_params=_cparams(1),
        name="swa_attn",
    )(sinks_pairs, qkv_swa, bias_swa)


def _merge_kernel(x_ref, mod_ref, gain_ref, osb_ref, odil_ref, oswa_ref,
                  wgate_ref, wsb_ref, wdil_ref, wswa_ref, wout_ref, o_ref):
    x = x_ref[0]
    mod = mod_ref[0]
    d = x.shape[1]
    h = _prenorm(x, gain_ref[...], mod, 1).astype(BF16)
    merged = None
    for t, (o_br, w_br) in enumerate(((osb_ref, wsb_ref), (odil_ref, wdil_ref), (oswa_ref, wswa_ref))):
        gate = jax.nn.sigmoid(jnp.dot(h, wgate_ref[:, t * d:(t + 1) * d], preferred_element_type=F32))
        term = gate * jnp.dot(o_br[0], w_br[...], preferred_element_type=F32)
        merged = term if merged is None else merged + term
    y = jnp.dot(merged.astype(BF16), wout_ref[...], preferred_element_type=F32)
    o_ref[0] = x + mod[5:6] * y


def _merge(x, mod, gain, o_sb, o_dil, o_swa, w_gate, w_sb, w_dil, w_swa, w_out):
    bn, s, d = x.shape
    row = lambda n: pl.BlockSpec((1, ROW_TILE, n), lambda b, i: (b, i, 0))
    return pl.pallas_call(
        _merge_kernel,
        grid=(bn, s // ROW_TILE),
        in_specs=[row(d),
                  pl.BlockSpec((1, ADA_CHUNKS, d), lambda b, i: (b, 0, 0)),
                  _const_spec((1, d)),
                  row(o_sb.shape[2]), row(o_dil.shape[2]), row(o_swa.shape[2]),
                  _const_spec(w_gate.shape), _const_spec(w_sb.shape), _const_spec(w_dil.shape),
                  _const_spec(w_swa.shape), _const_spec(w_out.shape)],
        out_specs=row(d),
        out_shape=jax.ShapeDtypeStruct(x.shape, F32),
        compiler_params=_cparams(2),
        name="merge",
    )(x, mod, gain.reshape(1, d), o_sb, o_dil, o_swa, w_gate, w_sb, w_dil, w_swa, w_out)


def _t5_bucket(n):
    max_exact = N_BUCKETS // 2
    nf = jnp.maximum(n, 1).astype(jnp.float32)
    large = max_exact + (jnp.log(nf / max_exact) / math.log(MAX_REL_DIST / max_exact)
                         * (N_BUCKETS - max_exact)).astype(jnp.int32)
    large = jnp.minimum(large, N_BUCKETS - 1)
    return jnp.where(n < max_exact, n, large)


def _band_bias(table_cols, dilation):
    rel = jnp.arange(BLK)[:, None] + BLK - jnp.arange(2 * BLK)[None, :]
    b = _t5_bucket(jnp.maximum(rel, 0) * dilation)
    return jnp.transpose(table_cols[b], (2, 0, 1)).astype(F32)


_SWA_HEAD_ORDER = tuple(p + (H_SWA_Q // H_SWA_KV) * h for p in range(N_SWA_PAIRS) for h in range(2))


def _head_cols(order):
    return jnp.asarray([hd * HEAD_DIM + e for hd in order for e in range(HEAD_DIM)], dtype=jnp.int32)


def _split_w_in(w_in_l):
    d_model = w_in_l.shape[0]
    sizes = (H_SB * HEAD_DIM,) * 3 + (H_DIL * HEAD_DIM,) * 3 + (H_SWA_Q * HEAD_DIM,) \
        + (H_SWA_KV * HEAD_DIM,) * 2 + (d_model,) * 3
    offs = [0]
    for n in sizes:
        offs.append(offs[-1] + n)
    part = lambda t: w_in_l[:, offs[t]:offs[t + 1]]
    q_d, k_d, v_d = part(3), part(4), part(5)
    dil = []
    for g in range(len(DIL_PATTERNS)):
        cs = slice(g * LANES, (g + 1) * LANES)
        dil += [q_d[:, cs], k_d[:, cs], v_d[:, cs]]
    q_swa = jnp.take(part(6), _head_cols(_SWA_HEAD_ORDER), axis=1)
    w_qkv = jnp.concatenate([part(0), part(1), part(2)] + dil + [q_swa, part(7), part(8)], axis=1)
    w_gate = w_in_l[:, offs[9]:]
    return w_qkv.astype(BF16), w_gate.astype(BF16)


def kernel(x, c, w_ada, b_ada, norm_gain, w_ffn_gate, w_ffn_up, w_ffn_down, w_in,
           w_br_sb, w_br_dil, w_br_swa, w_out, sinks, rel_bias, final_gain):
    depth = w_ada.shape[0]
    bn, s, d = x.shape
    mods = _ada(c, w_ada, b_ada).reshape(depth, bn, ADA_CHUNKS, d)

    bias_dil = jnp.stack([_band_bias(rel_bias[:, g * H_PER_DIL:(g + 1) * H_PER_DIL], dil)
                          for g, (_, dil) in enumerate(DIL_PATTERNS)])
    swa_cols = jnp.asarray([H_DIL + hd for hd in _SWA_HEAD_ORDER], dtype=jnp.int32)
    bias_swa = _band_bias(jnp.take(rel_bias, swa_cols, axis=1), 1).reshape(N_SWA_PAIRS, 2, BLK, 2 * BLK)
    swa_rows = _head_cols(_SWA_HEAD_ORDER)

    for l in range(depth):
        mod = mods[l]
        bf = lambda w: w.astype(BF16)
        x = _ffn(x, mod, norm_gain[l, 0], bf(w_ffn_gate[l, 0]), bf(w_ffn_up[l, 0]), bf(w_ffn_down[l, 0]), 0)

        w_qkv, w_gate = _split_w_in(w_in[l])
        qkv_sb, qkv_dil, qkv_swa = _inproj(x, mod, norm_gain[l, 1], w_qkv)
        o_sb = _sb_attention(qkv_sb)
        o_dil = _dil_attention(qkv_dil, bias_dil)
        o_swa = _swa_attention(qkv_swa, bias_swa, jnp.take(sinks[l], jnp.asarray(_SWA_HEAD_ORDER)))
        x = _merge(x, mod, norm_gain[l, 1], o_sb, o_dil, o_swa, w_gate,
                   bf(w_br_sb[l]), bf(w_br_dil[l]), bf(jnp.take(w_br_swa[l], swa_rows, axis=0)), bf(w_out[l]))

        last = l == depth - 1
        x = _ffn(x, mod, norm_gain[l, 2], bf(w_ffn_gate[l, 1]), bf(w_ffn_up[l, 1]), bf(w_ffn_down[l, 1]), 2,
                 final_gain=final_gain if last else None)
    return x
```

```python
import functools
import math

import jax
import jax.numpy as jnp
from jax import lax
from jax.experimental import pallas as pl
from jax.experimental.pallas import tpu as pltpu

F32 = jnp.float32
BF16 = jnp.bfloat16

HEAD_DIM = 64
LANES = 128
BLK = 128
H_SB = 4
DIL_PATTERNS = ((128, 1), (512, 4), (2048, 16))
H_PER_DIL = 2
H_DIL = H_PER_DIL * len(DIL_PATTERNS)
H_SWA_Q = 6
H_SWA_KV = 2
SWA_WINDOW = 128
N_BUCKETS = 32
MAX_REL_DIST = 2048
RMS_EPS = 1e-6
ADA_CHUNKS = 9
NEG_BIG = -1e30

VMEM_LIMIT = 56 * 1024 * 1024
ROW_TILE = 512
FF_CHUNK = 256
ADA_TILE = 1024
SB_ROWS = 512


def _cparams(n_axes):
    return pltpu.CompilerParams(dimension_semantics=("parallel",) * n_axes,
                                vmem_limit_bytes=VMEM_LIMIT)


def _const_spec(shape):
    nd = len(shape)
    return pl.BlockSpec(shape, lambda *_: (0,) * nd, pipeline_mode=pl.Buffered(1))


def _ada_kernel(c_ref, w_ref, b_ref, o_ref):
    c = c_ref[...]
    sc = (c * jax.nn.sigmoid(c)).astype(BF16)
    o_ref[0] = jnp.dot(sc, w_ref[0].astype(BF16), preferred_element_type=F32) + b_ref[0]


def _ada(c, w_ada, b_ada):
    depth, d, n = w_ada.shape
    bn = c.shape[0]
    return pl.pallas_call(
        _ada_kernel,
        grid=(depth, n // ADA_TILE),
        in_specs=[pl.BlockSpec((bn, d), lambda l, j: (0, 0)),
                  pl.BlockSpec((1, d, ADA_TILE), lambda l, j: (l, 0, j)),
                  pl.BlockSpec((1, 1, ADA_TILE), lambda l, j: (l, 0, j))],
        out_specs=pl.BlockSpec((1, bn, ADA_TILE), lambda l, j: (l, 0, j)),
        out_shape=jax.ShapeDtypeStruct((depth, bn, n), F32),
        compiler_params=_cparams(2),
        name="ada_mod",
    )(c, w_ada, b_ada.reshape(depth, 1, n))


def _rms(x, gain):
    ms = jnp.mean(x * x, axis=-1, keepdims=True)
    return x * lax.rsqrt(ms + RMS_EPS) * gain


def _prenorm(x, gain, mod, j):
    shift = mod[3 * j:3 * j + 1]
    scale = mod[3 * j + 1:3 * j + 2]
    return _rms(x, gain) * (1.0 + scale) + shift


def _ffn_kernel(*refs, j, final):
    if final:
        x_ref, mod_ref, gain_ref, wg_ref, wu_ref, wd_ref, fg_ref, o_ref = refs
    else:
        x_ref, mod_ref, gain_ref, wg_ref, wu_ref, wd_ref, o_ref = refs
    x = x_ref[0]
    mod = mod_ref[0]
    h = _prenorm(x, gain_ref[...], mod, j).astype(BF16)
    d_ff = wg_ref.shape[1]
    acc = jnp.zeros(x.shape, F32)
    for ci in range(d_ff // FF_CHUNK):
        cols = slice(ci * FF_CHUNK, (ci + 1) * FF_CHUNK)
        g = jnp.dot(h, wg_ref[:, cols], preferred_element_type=F32)
        u = jnp.dot(h, wu_ref[:, cols], preferred_element_type=F32)
        a = (g * jax.nn.sigmoid(g) * u).astype(BF16)
        acc = acc + jnp.dot(a, wd_ref[cols, :], preferred_element_type=F32)
    y = x + (0.5 * mod[3 * j + 2:3 * j + 3]) * acc
    if final:
        y = _rms(y, fg_ref[...])
    o_ref[0] = y


def _ffn(x, mod, gain, wg, wu, wd, j, final_gain=None):
    bn, s, d = x.shape
    final = final_gain is not None
    in_specs = [pl.BlockSpec((1, ROW_TILE, d), lambda b, i: (b, i, 0)),
                pl.BlockSpec((1, ADA_CHUNKS, d), lambda b, i: (b, 0, 0)),
                _const_spec((1, d)),
                _const_spec(wg.shape), _const_spec(wu.shape), _const_spec(wd.shape)]
    args = [x, mod, gain.reshape(1, d), wg, wu, wd]
    if final:
        in_specs.append(_const_spec((1, d)))
        args.append(final_gain.reshape(1, d))
    return pl.pallas_call(
        functools.partial(_ffn_kernel, j=j, final=final),
        grid=(bn, s // ROW_TILE),
        in_specs=in_specs,
        out_specs=pl.BlockSpec((1, ROW_TILE, d), lambda b, i: (b, i, 0)),
        out_shape=jax.ShapeDtypeStruct(x.shape, F32),
        compiler_params=_cparams(2),
        name="ffn_final" if final else f"ffn{j}",
    )(*args)


N_SB = 3 * H_SB * HEAD_DIM
N_DIL = 3 * H_DIL * HEAD_DIM
N_SWA = (H_SWA_Q + 2 * H_SWA_KV) * HEAD_DIM
N_QKV = N_SB + N_DIL + N_SWA
N_DIL_SLABS = N_DIL // LANES


def _inproj_kernel(x_ref, mod_ref, gain_ref, w_ref, sb_ref, dil_ref, swa_ref):
    h = _prenorm(x_ref[0], gain_ref[...], mod_ref[0], 1).astype(BF16)
    r = jnp.dot(h, w_ref[...], preferred_element_type=F32)
    qs = HEAD_DIM ** -0.5
    n_q_sb = H_SB * HEAD_DIM
    sb_ref[0, :, :n_q_sb] = (r[:, :n_q_sb] * qs).astype(BF16)
    sb_ref[0, :, n_q_sb:] = r[:, n_q_sb:N_SB].astype(BF16)
    for t in range(N_DIL_SLABS):
        slab = r[:, N_SB + t * LANES:N_SB + (t + 1) * LANES]
        dil_ref[0, t] = slab * qs if t % 3 == 0 else slab
    n_q_swa = H_SWA_Q * HEAD_DIM
    o = N_SB + N_DIL
    swa_ref[0, :, :n_q_swa] = (r[:, o:o + n_q_swa] * qs).astype(BF16)
    swa_ref[0, :, n_q_swa:] = r[:, o + n_q_swa:].astype(BF16)


def _inproj(x, mod, gain, w_qkv):
    bn, s, d = x.shape
    return pl.pallas_call(
        _inproj_kernel,
        grid=(bn, s // ROW_TILE),
        in_specs=[pl.BlockSpec((1, ROW_TILE, d), lambda b, i: (b, i, 0)),
                  pl.BlockSpec((1, ADA_CHUNKS, d), lambda b, i: (b, 0, 0)),
                  _const_spec((1, d)),
                  _const_spec(w_qkv.shape)],
        out_specs=[pl.BlockSpec((1, ROW_TILE, N_SB), lambda b, i: (b, i, 0)),
                   pl.BlockSpec((1, N_DIL_SLABS, ROW_TILE, LANES), lambda b, i: (b, 0, i, 0)),
                   pl.BlockSpec((1, ROW_TILE, N_SWA), lambda b, i: (b, i, 0))],
        out_shape=[jax.ShapeDtypeStruct((bn, s, N_SB), BF16),
                   jax.ShapeDtypeStruct((bn, N_DIL_SLABS, s, LANES), F32),
                   jax.ShapeDtypeStruct((bn, s, N_SWA), BF16)],
        compiler_params=_cparams(2),
        name="in_proj",
    )(x, mod, gain.reshape(1, d), w_qkv)


def _dot_nt(a, b):
    return lax.dot_general(a, b, (((1,), (1,)), ((), ())), preferred_element_type=F32)


def _sb_kernel(q_ref, k_ref, v_ref, o_ref, acc_ref, c_ref):
    s_len = q_ref.shape[1]
    per = SB_ROWS // BLK
    lane = lax.broadcasted_iota(jnp.int32, (BLK, LANES), 1)
    head0 = lane < HEAD_DIM
    row2 = lax.broadcasted_iota(jnp.int32, (BLK, 2 * BLK), 0)
    col2 = lax.broadcasted_iota(jnp.int32, (BLK, 2 * BLK), 1)
    strict2 = (col2 & (BLK - 1)) < row2
    rk = lax.broadcasted_iota(jnp.int32, (2 * BLK, 2 * BLK), 0)
    cn = lax.broadcasted_iota(jnp.int32, (2 * BLK, 2 * BLK), 1)
    suffix_ones = jnp.where(((rk & (BLK - 1)) >= cn) | (cn >= BLK), 1.0, 0.0).astype(BF16)

    def by_head(t):
        zero = jnp.zeros_like(t)
        return jnp.concatenate([jnp.where(head0, t, zero), jnp.where(head0, zero, t)], axis=0)

    def mask_first_block(t):
        first = jnp.where(strict2, t[:BLK], 0.0)
        return first if t.shape[0] == BLK else jnp.concatenate([first, t[BLK:]], axis=0)

    def step(q, row0, k0, diag):
        rows = slice(row0, row0 + q.shape[0])
        kk = by_head(k_ref[0, pl.ds(k0, BLK), :])
        vv = by_head(v_ref[0, pl.ds(k0, BLK), :])
        z = _dot_nt(q, kk)
        log_fail = -(jnp.maximum(z, 0.0) + jnp.log(1.0 + jnp.exp(-jnp.abs(z))))
        if diag:
            log_fail = mask_first_block(log_fail)
        hi = log_fail.astype(BF16)
        lo = (log_fail - hi.astype(F32)).astype(BF16)
        ws = []
        for h in range(2):
            cols = slice(h * BLK, (h + 1) * BLK)
            ct = jnp.dot(jnp.concatenate([hi[:, cols], lo[:, cols]], axis=1), suffix_ones,
                         preferred_element_type=F32)
            c = c_ref[h, rows, :]
            ws.append(jnp.exp(z[:, cols] + ct[:, :BLK] + c))
            c_ref[h, rows, :] = c + ct[:, BLK:]
        w = jnp.concatenate(ws, axis=1)
        if diag:
            w = mask_first_block(w)
        acc_ref[rows, :] += jnp.dot(w.astype(BF16), vv, preferred_element_type=F32)

    def superblock(sb, carry):
        base = pl.multiple_of(sb * SB_ROWS, SB_ROWS)
        acc_ref[...] = jnp.zeros_like(acc_ref)
        c_ref[...] = jnp.zeros_like(c_ref)
        for kq in reversed(range(per)):
            row0 = kq * BLK
            step(q_ref[0, pl.ds(base + row0, SB_ROWS - row0), :], row0, base + row0, True)
        q_all = q_ref[0, pl.ds(base, SB_ROWS), :]

        def kstep(jj, carry2):
            step(q_all, 0, pl.multiple_of(base - (jj + 1) * BLK, BLK), False)
            return carry2

        lax.fori_loop(0, sb * per, kstep, 0)
        o_ref[0, pl.ds(base, SB_ROWS), :] = acc_ref[...].astype(o_ref.dtype)
        return carry

    lax.fori_loop(0, s_len // SB_ROWS, superblock, 0)


def _sb_attention(qkv_sb):
    bn, s, _ = qkv_sb.shape
    n_pairs = H_SB // 2
    spec = lambda off: pl.BlockSpec((1, s, LANES), lambda b, p: (b, 0, off + p))
    return pl.pallas_call(
        _sb_kernel,
        grid=(bn, n_pairs),
        in_specs=[spec(0), spec(n_pairs), spec(2 * n_pairs)],
        out_specs=pl.BlockSpec((1, s, LANES), lambda b, p: (b, 0, p)),
        out_shape=jax.ShapeDtypeStruct((bn, s, n_pairs * LANES), BF16),
        scratch_shapes=[pltpu.VMEM((SB_ROWS, LANES), F32), pltpu.VMEM((2, SB_ROWS, LANES), F32)],
        compiler_params=_cparams(2),
        name="sb_attn",
    )(qkv_sb, qkv_sb, qkv_sb)


def _band_mask(max_dist, block_idx):
    row = lax.broadcasted_iota(jnp.int32, (BLK, 2 * BLK), 0)
    col = lax.broadcasted_iota(jnp.int32, (BLK, 2 * BLK), 1)
    rel = row + BLK - col
    first_valid_col = jnp.where(block_idx > 0, 0, BLK)
    return (rel >= 0) & (rel <= max_dist) & (col >= first_valid_col)


def _band_pair(q, kcat, vcat, bias_ref, mask, sinks):
    lane = lax.broadcasted_iota(jnp.int32, (BLK, LANES), 1)
    head0 = lane < HEAD_DIM
    zero = jnp.zeros_like(q)
    outs, lses = [], []
    for h in range(2):
        qh = jnp.where(head0, q, zero) if h == 0 else jnp.where(head0, zero, q)
        s = jnp.where(mask, _dot_nt(qh, kcat) + bias_ref[h], NEG_BIG)
        m = jnp.max(s, axis=-1, keepdims=True)
        if sinks is not None:
            m = jnp.maximum(m, sinks[h])
        p = jnp.exp(s - m)
        den = jnp.sum(p, axis=-1, keepdims=True)
        if sinks is not None:
            den = den + jnp.exp(sinks[h] - m)
        o = jnp.dot(p.astype(BF16), vcat, preferred_element_type=F32)
        outs.append(o * (1.0 / den))
        lses.append(m + jnp.log(den))
    out = jnp.where(head0, outs[0], outs[1])
    lse = jnp.where(head0, lses[0], lses[1])
    return out, lse


def _dil_kernel(qkv_ref, bias_ref, o_ref, og_ref, lg_ref):
    s_len = o_ref.shape[1]
    n_blocks = s_len // BLK
    for g, (window, d) in enumerate(DIL_PATTERNS):
        nb = s_len // d // BLK

        def rows(start):
            return pl.ds(start, BLK) if d == 1 else pl.ds(start, BLK, stride=d)

        def body(idx, carry):
            r, i = idx // nb, idx % nb
            start = r + d * BLK * i
            q = qkv_ref[0, 3 * g, rows(start), :].astype(BF16)
            k_own = qkv_ref[0, 3 * g + 1, rows(start), :]
            v_own = qkv_ref[0, 3 * g + 2, rows(start), :]
            prev = r + d * BLK * jnp.maximum(i - 1, 0)
            k_prev = qkv_ref[0, 3 * g + 1, rows(prev), :]
            v_prev = qkv_ref[0, 3 * g + 2, rows(prev), :]
            kcat = jnp.concatenate([k_prev, k_own], axis=0).astype(BF16)
            vcat = jnp.concatenate([v_prev, v_own], axis=0).astype(BF16)
            mask = _band_mask(window // d, i)
            out, lse = _band_pair(q, kcat, vcat, bias_ref.at[g], mask, None)
            og_ref[g, rows(start), :] = out
            lg_ref[g, rows(start), :] = lse
            return carry

        lax.fori_loop(0, n_blocks, body, 0)

    def merge(i, carry):
        sl = pl.ds(pl.multiple_of(i * BLK, BLK), BLK)
        l0, l1, l2 = lg_ref[0, sl, :], lg_ref[1, sl, :], lg_ref[2, sl, :]
        m = jnp.maximum(jnp.maximum(l0, l1), l2)
        e0, e1, e2 = jnp.exp(l0 - m), jnp.exp(l1 - m), jnp.exp(l2 - m)
        num = e0 * og_ref[0, sl, :] + e1 * og_ref[1, sl, :] + e2 * og_ref[2, sl, :]
        o_ref[0, sl, :] = (num / (e0 + e1 + e2)).astype(o_ref.dtype)
        return carry

    lax.fori_loop(0, n_blocks, merge, 0)


def _dil_attention(qkv_dil, bias_dil):
    bn, n_slabs, s, _ = qkv_dil.shape
    n_groups = len(DIL_PATTERNS)
    return pl.pallas_call(
        _dil_kernel,
        grid=(bn,),
        in_specs=[pl.BlockSpec((1, n_slabs, s, LANES), lambda b: (b, 0, 0, 0)),
                  _const_spec(bias_dil.shape)],
        out_specs=pl.BlockSpec((1, s, LANES), lambda b: (b, 0, 0)),
        out_shape=jax.ShapeDtypeStruct((bn, s, LANES), BF16),
        scratch_shapes=[pltpu.VMEM((n_groups, s, LANES), F32), pltpu.VMEM((n_groups, s, LANES), F32)],
        compiler_params=_cparams(1),
        name="dil_attn",
    )(qkv_dil, bias_dil)


N_SWA_PAIRS = H_SWA_Q // 2


def _swa_kernel(sink_ref, qkv_ref, bias_ref, o_ref):
    s_len = o_ref.shape[1]
    k_col = N_SWA_PAIRS * LANES
    v_col = k_col + LANES

    def body(i, carry):
        start = pl.multiple_of(i * BLK, BLK)
        prev = pl.multiple_of(jnp.maximum(i - 1, 0) * BLK, BLK)
        kcat = jnp.concatenate([qkv_ref[0, pl.ds(prev, BLK), k_col:k_col + LANES],
                                qkv_ref[0, pl.ds(start, BLK), k_col:k_col + LANES]], axis=0)
        vcat = jnp.concatenate([qkv_ref[0, pl.ds(prev, BLK), v_col:v_col + LANES],
                                qkv_ref[0, pl.ds(start, BLK), v_col:v_col + LANES]], axis=0)
        mask = _band_mask(SWA_WINDOW - 1, i)
        for p in range(N_SWA_PAIRS):
            q = qkv_ref[0, pl.ds(start, BLK), p * LANES:(p + 1) * LANES]
            sinks = (sink_ref[2 * p], sink_ref[2 * p + 1])
            out, _ = _band_pair(q, kcat, vcat, bias_ref.at[p], mask, sinks)
            o_ref[0, pl.ds(start, BLK), p * LANES:(p + 1) * LANES] = out.astype(o_ref.dtype)
        return carry

    lax.fori_loop(0, s_len // BLK, body, 0)


def _swa_attention(qkv_swa, bias_swa, sinks_pairs):
    bn, s, n = qkv_swa.shape
    return pl.pallas_call(
        _swa_kernel,
        grid_spec=pltpu.PrefetchScalarGridSpec(
            num_scalar_prefetch=1,
            grid=(bn,),
            in_specs=[pl.BlockSpec((1, s, n), lambda b, sk: (b, 0, 0)),
                      pl.BlockSpec(bias_swa.shape, lambda b, sk: (0, 0, 0, 0))],
            out_specs=pl.BlockSpec((1, s, N_SWA_PAIRS * LANES), lambda b, sk: (b, 0, 0)),
        ),
        out_shape=jax.ShapeDtypeStruct((bn, s, N_SWA_PAIRS * LANES), BF16),
        compiler_params=_cparams(1),
        name="swa_attn",
    )(sinks_pairs, qkv_swa, bias_swa)


def _merge_kernel(x_ref, mod_ref, gain_ref, osb_ref, odil_ref, oswa_ref,
                  wgate_ref, wsb_ref, wdil_ref, wswa_ref, wout_ref, o_ref):
    x = x_ref[0]
    mod = mod_ref[0]
    d = x.shape[1]
    h = _prenorm(x, gain_ref[...], mod, 1).astype(BF16)
    merged = None
    for t, (o_br, w_br) in enumerate(((osb_ref, wsb_ref), (odil_ref, wdil_ref), (oswa_ref, wswa_ref))):
        gate = jax.nn.sigmoid(jnp.dot(h, wgate_ref[:, t * d:(t + 1) * d], preferred_element_type=F32))
        term = gate * jnp.dot(o_br[0], w_br[...], preferred_element_type=F32)
        merged = term if merged is None else merged + term
    y = jnp.dot(merged.astype(BF16), wout_ref[...], preferred_element_type=F32)
    o_ref[0] = x + mod[5:6] * y


def _merge(x, mod, gain, o_sb, o_dil, o_swa, w_gate, w_sb, w_dil, w_swa, w_out):
    bn, s, d = x.shape
    row = lambda n: pl.BlockSpec((1, ROW_TILE, n), lambda b, i: (b, i, 0))
    return pl.pallas_call(
        _merge_kernel,
        grid=(bn, s // ROW_TILE),
        in_specs=[row(d),
                  pl.BlockSpec((1, ADA_CHUNKS, d), lambda b, i: (b, 0, 0)),
                  _const_spec((1, d)),
                  row(o_sb.shape[2]), row(o_dil.shape[2]), row(o_swa.shape[2]),
                  _const_spec(w_gate.shape), _const_spec(w_sb.shape), _const_spec(w_dil.shape),
                  _const_spec(w_swa.shape), _const_spec(w_out.shape)],
        out_specs=row(d),
        out_shape=jax.ShapeDtypeStruct(x.shape, F32),
        compiler_params=_cparams(2),
        name="merge",
    )(x, mod, gain.reshape(1, d), o_sb, o_dil, o_swa, w_gate, w_sb, w_dil, w_swa, w_out)


def _t5_bucket(n):
    max_exact = N_BUCKETS // 2
    nf = jnp.maximum(n, 1).astype(jnp.float32)
    large = max_exact + (jnp.log(nf / max_exact) / math.log(MAX_REL_DIST / max_exact)
                         * (N_BUCKETS - max_exact)).astype(jnp.int32)
    large = jnp.minimum(large, N_BUCKETS - 1)
    return jnp.where(n < max_exact, n, large)


def _band_bias(table_cols, dilation):
    rel = jnp.arange(BLK)[:, None] + BLK - jnp.arange(2 * BLK)[None, :]
    b = _t5_bucket(jnp.maximum(rel, 0) * dilation)
    return jnp.transpose(table_cols[b], (2, 0, 1)).astype(F32)


_SWA_HEAD_ORDER = tuple(p + (H_SWA_Q // H_SWA_KV) * h for p in range(N_SWA_PAIRS) for h in range(2))


def _reorder_heads(w, order, axis):
    parts = [lax.slice_in_dim(w, hd * HEAD_DIM, (hd + 1) * HEAD_DIM, axis=axis) for hd in order]
    return jnp.concatenate(parts, axis=axis)


def _split_w_in(w_in_l):
    d_model = w_in_l.shape[0]
    sizes = (H_SB * HEAD_DIM,) * 3 + (H_DIL * HEAD_DIM,) * 3 + (H_SWA_Q * HEAD_DIM,) \
        + (H_SWA_KV * HEAD_DIM,) * 2 + (d_model,) * 3
    offs = [0]
    for n in sizes:
        offs.append(offs[-1] + n)
    part = lambda t: w_in_l[:, offs[t]:offs[t + 1]]
    q_d, k_d, v_d = part(3), part(4), part(5)
    dil = []
    for g in range(len(DIL_PATTERNS)):
        cs = slice(g * LANES, (g + 1) * LANES)
        dil += [q_d[:, cs], k_d[:, cs], v_d[:, cs]]
    q_swa = _reorder_heads(part(6), _SWA_HEAD_ORDER, 1)
    w_qkv = jnp.concatenate([part(0), part(1), part(2)] + dil + [q_swa, part(7), part(8)], axis=1)
    w_gate = w_in_l[:, offs[9]:]
    return w_qkv.astype(BF16), w_gate.astype(BF16)


def kernel(x, c, w_ada, b_ada, norm_gain, w_ffn_gate, w_ffn_up, w_ffn_down, w_in,
           w_br_sb, w_br_dil, w_br_swa, w_out, sinks, rel_bias, final_gain):
    depth = w_ada.shape[0]
    bn, s, d = x.shape
    mods = _ada(c, w_ada, b_ada).reshape(depth, bn, ADA_CHUNKS, d)

    bias_dil = jnp.stack([_band_bias(rel_bias[:, g * H_PER_DIL:(g + 1) * H_PER_DIL], dil)
                          for g, (_, dil) in enumerate(DIL_PATTERNS)])
    swa_table = jnp.stack([rel_bias[:, H_DIL + hd] for hd in _SWA_HEAD_ORDER], axis=1)
    bias_swa = _band_bias(swa_table, 1).reshape(N_SWA_PAIRS, 2, BLK, 2 * BLK)

    for l in range(depth):
        mod = mods[l]
        bf = lambda w: w.astype(BF16)
        x = _ffn(x, mod, norm_gain[l, 0], bf(w_ffn_gate[l, 0]), bf(w_ffn_up[l, 0]), bf(w_ffn_down[l, 0]), 0)

        w_qkv, w_gate = _split_w_in(w_in[l])
        qkv_sb, qkv_dil, qkv_swa = _inproj(x, mod, norm_gain[l, 1], w_qkv)
        o_sb = _sb_attention(qkv_sb)
        o_dil = _dil_attention(qkv_dil, bias_dil)
        sinks_pairs = jnp.stack([sinks[l, hd] for hd in _SWA_HEAD_ORDER])
        o_swa = _swa_attention(qkv_swa, bias_swa, sinks_pairs)
        x = _merge(x, mod, norm_gain[l, 1], o_sb, o_dil, o_swa, w_gate,
                   bf(w_br_sb[l]), bf(w_br_dil[l]), bf(_reorder_heads(w_br_swa[l], _SWA_HEAD_ORDER, 0)),
                   bf(w_out[l]))

        last = l == depth - 1
        x = _ffn(x, mod, norm_gain[l, 2], bf(w_ffn_gate[l, 1]), bf(w_ffn_up[l, 1]), bf(w_ffn_down[l, 1]), 2,
                 final_gain=final_gain if last else None)
    return x
```

```python
import functools
import math

import jax
import jax.numpy as jnp
from jax import lax
from jax.experimental import pallas as pl
from jax.experimental.pallas import tpu as pltpu

F32 = jnp.float32
BF16 = jnp.bfloat16

HEAD_DIM = 64
LANES = 128
BLK = 128
H_SB = 4
DIL_PATTERNS = ((128, 1), (512, 4), (2048, 16))
H_PER_DIL = 2
H_DIL = H_PER_DIL * len(DIL_PATTERNS)
H_SWA_Q = 6
H_SWA_KV = 2
SWA_WINDOW = 128
N_SOFT_HEADS = H_DIL + H_SWA_Q
N_SWA_PAIRS = H_SWA_Q // 2
N_BUCKETS = 32
MAX_REL_DIST = 2048
RMS_EPS = 1e-6
ADA_CHUNKS = 9
NEG_BIG = -1e30
LOG2E = 1.4426950408889634

VMEM_LIMIT = 56 * 1024 * 1024
ROW_TILE = 512
FF_CHUNK = 256
ADA_TILE = 1024
SB_ROWS = 512
SB_KEY_BLOCKS = 4
SB_DEAD_LOG2 = -150.0
BAND_UNROLL = 4
MAX_STRIDE = 4
SWA_UNROLL = 2

_SWA_HEAD_ORDER = tuple(p + (H_SWA_Q // H_SWA_KV) * h for p in range(N_SWA_PAIRS) for h in range(2))


def _cparams(n_axes):
    return pltpu.CompilerParams(dimension_semantics=("parallel",) * n_axes,
                                vmem_limit_bytes=VMEM_LIMIT)


def _const_spec(shape):
    nd = len(shape)
    return pl.BlockSpec(shape, lambda *_: (0,) * nd, pipeline_mode=pl.Buffered(1))


def _ada_kernel(c_ref, w_ref, b_ref, o_ref):
    c = c_ref[...]
    sc = (c * jax.nn.sigmoid(c)).astype(BF16)
    o_ref[0] = jnp.dot(sc, w_ref[0].astype(BF16), preferred_element_type=F32) + b_ref[0]


def _ada(c, w_ada, b_ada):
    depth, d, n = w_ada.shape
    bn = c.shape[0]
    return pl.pallas_call(
        _ada_kernel,
        grid=(depth, n // ADA_TILE),
        in_specs=[pl.BlockSpec((bn, d), lambda l, j: (0, 0)),
                  pl.BlockSpec((1, d, ADA_TILE), lambda l, j: (l, 0, j)),
                  pl.BlockSpec((1, 1, ADA_TILE), lambda l, j: (l, 0, j))],
        out_specs=pl.BlockSpec((1, bn, ADA_TILE), lambda l, j: (l, 0, j)),
        out_shape=jax.ShapeDtypeStruct((depth, bn, n), F32),
        compiler_params=_cparams(2),
        name="ada_mod",
    )(c, w_ada, b_ada.reshape(depth, 1, n))


def _rms(x, gain):
    ms = jnp.mean(x * x, axis=-1, keepdims=True)
    return x * lax.rsqrt(ms + RMS_EPS) * gain


def _prenorm(x, gain, mod, j):
    shift = mod[3 * j:3 * j + 1]
    scale = mod[3 * j + 1:3 * j + 2]
    return _rms(x, gain) * (1.0 + scale) + shift


def _dot_nt(a, b):
    return lax.dot_general(a, b, (((1,), (1,)), ((), ())), preferred_element_type=F32)


def _ffn_kernel(*refs, j, final):
    if final:
        x_ref, mod_ref, gain_ref, wg_ref, wu_ref, wd_ref, fg_ref, o_ref = refs
    else:
        x_ref, mod_ref, gain_ref, wg_ref, wu_ref, wd_ref, o_ref = refs
    x = x_ref[0]
    mod = mod_ref[0]
    h = _prenorm(x, gain_ref[...], mod, j).astype(BF16)
    d_ff = wg_ref.shape[1]
    acc = jnp.zeros(x.shape, F32)
    for ci in range(d_ff // FF_CHUNK):
        cols = slice(ci * FF_CHUNK, (ci + 1) * FF_CHUNK)
        g = jnp.dot(h, wg_ref[:, cols], preferred_element_type=F32)
        u = jnp.dot(h, wu_ref[:, cols], preferred_element_type=F32)
        a = (g * jax.nn.sigmoid(g) * u).astype(BF16)
        acc = acc + jnp.dot(a, wd_ref[cols, :], preferred_element_type=F32)
    y = x + (0.5 * mod[3 * j + 2:3 * j + 3]) * acc
    if final:
        y = _rms(y, fg_ref[...])
    o_ref[0] = y


def _ffn(x, mod, gain, wg, wu, wd, j, final_gain=None):
    bn, s, d = x.shape
    final = final_gain is not None
    in_specs = [pl.BlockSpec((1, ROW_TILE, d), lambda b, i: (b, i, 0)),
                pl.BlockSpec((1, ADA_CHUNKS, d), lambda b, i: (b, 0, 0)),
                _const_spec((1, d)),
                _const_spec(wg.shape), _const_spec(wu.shape), _const_spec(wd.shape)]
    args = [x, mod, gain.reshape(1, d), wg, wu, wd]
    if final:
        in_specs.append(_const_spec((1, d)))
        args.append(final_gain.reshape(1, d))
    return pl.pallas_call(
        functools.partial(_ffn_kernel, j=j, final=final),
        grid=(bn, s // ROW_TILE),
        in_specs=in_specs,
        out_specs=pl.BlockSpec((1, ROW_TILE, d), lambda b, i: (b, i, 0)),
        out_shape=jax.ShapeDtypeStruct(x.shape, F32),
        compiler_params=_cparams(2),
        name="ffn_final" if final else f"ffn{j}",
    )(*args)


N_SB = 3 * H_SB * HEAD_DIM
N_DIL = 3 * H_DIL * HEAD_DIM
N_SWA = (H_SWA_Q + 2 * H_SWA_KV) * HEAD_DIM
N_QKV = N_SB + N_DIL + N_SWA
N_DIL_SLABS = N_DIL // LANES


def _inproj_kernel(x_ref, mod_ref, gain_ref, w_ref, sb_ref, dil_ref, swa_ref):
    h = _prenorm(x_ref[0], gain_ref[...], mod_ref[0], 1).astype(BF16)
    r = jnp.dot(h, w_ref[...], preferred_element_type=F32)
    qs = HEAD_DIM ** -0.5
    n_q_sb = H_SB * HEAD_DIM
    sb_ref[0, :, :n_q_sb] = (r[:, :n_q_sb] * (qs * LOG2E)).astype(BF16)
    sb_ref[0, :, n_q_sb:] = r[:, n_q_sb:N_SB].astype(BF16)
    for t in range(N_DIL_SLABS):
        slab = r[:, N_SB + t * LANES:N_SB + (t + 1) * LANES]
        dil_ref[0, t] = slab * qs if t % 3 == 0 else slab
    n_q_swa = H_SWA_Q * HEAD_DIM
    o = N_SB + N_DIL
    swa_ref[0, :, :n_q_swa] = (r[:, o:o + n_q_swa] * qs).astype(BF16)
    swa_ref[0, :, n_q_swa:] = r[:, o + n_q_swa:].astype(BF16)


def _inproj(x, mod, gain, w_qkv):
    bn, s, d = x.shape
    return pl.pallas_call(
        _inproj_kernel,
        grid=(bn, s // ROW_TILE),
        in_specs=[pl.BlockSpec((1, ROW_TILE, d), lambda b, i: (b, i, 0)),
                  pl.BlockSpec((1, ADA_CHUNKS, d), lambda b, i: (b, 0, 0)),
                  _const_spec((1, d)),
                  _const_spec(w_qkv.shape)],
        out_specs=[pl.BlockSpec((1, ROW_TILE, N_SB), lambda b, i: (b, i, 0)),
                   pl.BlockSpec((1, N_DIL_SLABS, ROW_TILE, LANES), lambda b, i: (b, 0, i, 0)),
                   pl.BlockSpec((1, ROW_TILE, N_SWA), lambda b, i: (b, i, 0))],
        out_shape=[jax.ShapeDtypeStruct((bn, s, N_SB), BF16),
                   jax.ShapeDtypeStruct((bn, N_DIL_SLABS, s, LANES), F32),
                   jax.ShapeDtypeStruct((bn, s, N_SWA), BF16)],
        compiler_params=_cparams(2),
        name="in_proj",
    )(x, mod, gain.reshape(1, d), w_qkv)


def _sb_kernel(q_ref, k_ref, v_ref, o_ref, acc_ref, c_ref):
    s_len = q_ref.shape[1]
    per = SB_ROWS // BLK
    lane = lax.broadcasted_iota(jnp.int32, (BLK, LANES), 1)
    head0 = lane < HEAD_DIM
    row2 = lax.broadcasted_iota(jnp.int32, (BLK, 2 * BLK), 0)
    col2 = lax.broadcasted_iota(jnp.int32, (BLK, 2 * BLK), 1)
    strict2 = (col2 & (BLK - 1)) < row2
    rk = lax.broadcasted_iota(jnp.int32, (2 * BLK, 2 * BLK), 0)
    cn = lax.broadcasted_iota(jnp.int32, (2 * BLK, 2 * BLK), 1)
    neg_suffix_ones = jnp.where(((rk & (BLK - 1)) >= cn) | (cn >= BLK), -1.0, 0.0).astype(BF16)

    def by_head(t):
        zero = jnp.zeros_like(t)
        return jnp.concatenate([jnp.where(head0, t, zero), jnp.where(head0, zero, t)], axis=0)

    def mask_first_block(t):
        first = jnp.where(strict2, t[:BLK], 0.0)
        return first if t.shape[0] == BLK else jnp.concatenate([first, t[BLK:]], axis=0)

    def step(q, c, k0, diag):
        kk = by_head(k_ref[0, pl.ds(k0, BLK), :])
        vv = by_head(v_ref[0, pl.ds(k0, BLK), :])
        z = _dot_nt(q, kk)
        neg_abs = lax.bitcast_convert_type(lax.bitcast_convert_type(z, jnp.int32) | jnp.int32(-2 ** 31), F32)
        softplus2 = jnp.maximum(z, 0.0) + jnp.log2(1.0 + jnp.exp2(neg_abs))
        if diag:
            softplus2 = mask_first_block(softplus2)
        hi = softplus2.astype(BF16)
        lo = (softplus2 - hi.astype(F32)).astype(BF16)
        ws, tots = [], []
        for h in range(2):
            cols = slice(h * BLK, (h + 1) * BLK)
            ct = jnp.dot(jnp.concatenate([hi[:, cols], lo[:, cols]], axis=1), neg_suffix_ones,
                         preferred_element_type=F32)
            ws.append(jnp.exp2(z[:, cols] + ct[:, :BLK] + c[h]))
            tots.append(ct[:, BLK:])
        w = jnp.concatenate(ws, axis=1)
        if diag:
            w = mask_first_block(w)
        return jnp.dot(w.astype(BF16), vv, preferred_element_type=F32), tots

    def superblock(sb, carry):
        base = pl.multiple_of(sb * SB_ROWS, SB_ROWS)
        acc_ref[...] = jnp.zeros_like(acc_ref)
        c_ref[...] = jnp.zeros_like(c_ref)
        for kq in reversed(range(per)):
            rows = slice(kq * BLK, SB_ROWS)
            q = q_ref[0, pl.ds(base + kq * BLK, SB_ROWS - kq * BLK), :]
            c = (c_ref[0, rows, :], c_ref[1, rows, :])
            pv, tots = step(q, c, base + kq * BLK, True)
            acc_ref[rows, :] += pv
            c_ref[0, rows, :] = c[0] + tots[0]
            c_ref[1, rows, :] = c[1] + tots[1]
        q_all = q_ref[0, pl.ds(base, SB_ROWS), :]

        def kstep(state):
            jj, _ = state
            c = (c_ref[0], c_ref[1])
            pv_sum = None
            for u in range(SB_KEY_BLOCKS):
                k0 = pl.multiple_of(base - (SB_KEY_BLOCKS * jj + u + 1) * BLK, BLK)
                pv, tots = step(q_all, c, k0, False)
                c = (c[0] + tots[0], c[1] + tots[1])
                pv_sum = pv if pv_sum is None else pv_sum + pv
            c_ref[0] = c[0]
            c_ref[1] = c[1]
            acc_ref[...] += pv_sum
            return jj + 1, jnp.max(jnp.maximum(c[0], c[1])) > SB_DEAD_LOG2

        n_iter = sb * (per // SB_KEY_BLOCKS)
        lax.while_loop(lambda st: jnp.logical_and(st[0] < n_iter, st[1]), kstep,
                       (jnp.int32(0), jnp.bool_(True)))
        o_ref[0, pl.ds(base, SB_ROWS), :] = acc_ref[...].astype(o_ref.dtype)
        return carry

    lax.fori_loop(0, s_len // SB_ROWS, superblock, 0)


def _sb_attention(qkv_sb):
    bn, s, _ = qkv_sb.shape
    n_pairs = H_SB // 2
    spec = lambda off: pl.BlockSpec((1, s, LANES), lambda b, p: (b, 0, off + p))
    return pl.pallas_call(
        _sb_kernel,
        grid=(bn, n_pairs),
        in_specs=[spec(0), spec(n_pairs), spec(2 * n_pairs)],
        out_specs=pl.BlockSpec((1, s, LANES), lambda b, p: (b, 0, p)),
        out_shape=jax.ShapeDtypeStruct((bn, s, n_pairs * LANES), BF16),
        scratch_shapes=[pltpu.VMEM((SB_ROWS, LANES), F32), pltpu.VMEM((2, SB_ROWS, LANES), F32)],
        compiler_params=_cparams(2),
        name="sb_attn",
    )(qkv_sb, qkv_sb, qkv_sb)


def _band_bias_tiles(bucket, table_ref, col, max_dist):
    row = lax.broadcasted_iota(jnp.int32, (BLK, 2 * BLK), 0)
    kcol = lax.broadcasted_iota(jnp.int32, (BLK, 2 * BLK), 1)
    rel = row + BLK - kcol
    in_band = (rel >= 0) & (rel <= max_dist)
    bias = jnp.zeros((BLK, 2 * BLK), F32)
    for n in range(N_BUCKETS):
        bias = jnp.where(bucket == n, table_ref[n * N_SOFT_HEADS + col], bias)
    return jnp.where(in_band, bias, NEG_BIG), jnp.where(in_band & (kcol >= BLK), bias, NEG_BIG)


def _band_pair(q, kcat, vcat, biases, sinks):
    lane = lax.broadcasted_iota(jnp.int32, (BLK, LANES), 1)
    head0 = lane < HEAD_DIM
    zero = jnp.zeros_like(q)
    outs, lses = [], []
    for h in range(2):
        qh = jnp.where(head0, q, zero) if h == 0 else jnp.where(head0, zero, q)
        s = _dot_nt(qh, kcat) + biases[h]
        m = jnp.max(s, axis=-1, keepdims=True)
        if sinks is not None:
            m = jnp.maximum(m, sinks[h])
        p = jnp.exp(s - m)
        den = jnp.sum(p, axis=-1, keepdims=True)
        if sinks is not None:
            den = den + jnp.exp(sinks[h] - m)
        o = jnp.dot(p.astype(BF16), vcat, preferred_element_type=F32)
        outs.append(o * (1.0 / den))
        lses.append(m + jnp.log(den))
    out = jnp.where(head0, outs[0], outs[1])
    lse = jnp.where(head0, lses[0], lses[1])
    return out, lse


def _dil_kernel(table_ref, qkv_ref, bucket_ref, o_ref, bias_ref, og_ref, lg_ref, stage_ref, ostage_ref):
    s_len = o_ref.shape[1]
    n_blocks = s_len // BLK

    @pl.when(pl.program_id(0) == 0)
    def _():
        for g, (window, d) in enumerate(DIL_PATTERNS):
            for h in range(H_PER_DIL):
                head = g * H_PER_DIL + h
                normal, first = _band_bias_tiles(bucket_ref[g], table_ref, head, window // d)
                bias_ref[head, 0] = normal
                bias_ref[head, 1] = first

    for g, (window, d) in enumerate(DIL_PATTERNS):
        nb = s_len // d // BLK
        two_pass = d > MAX_STRIDE
        d_in = d // MAX_STRIDE if two_pass else d
        seg = s_len // MAX_STRIDE

        if two_pass:
            def stage_in(cc, carry):
                c1, j = cc // (seg // BLK), cc % (seg // BLK)
                for a in range(3):
                    stage_ref[a, pl.ds(pl.multiple_of(c1 * seg + j * BLK, BLK), BLK), :] = \
                        qkv_ref[0, 3 * g + a, pl.ds(c1 + MAX_STRIDE * BLK * j, BLK, stride=MAX_STRIDE), :]
                return carry

            lax.fori_loop(0, n_blocks, stage_in, 0)

        def src(a, start):
            if d == 1:
                return qkv_ref[0, 3 * g + a, pl.ds(start, BLK), :]
            if not two_pass:
                return qkv_ref[0, 3 * g + a, pl.ds(start, BLK, stride=d), :]
            return stage_ref[a, pl.ds(staged(start), BLK, stride=d_in), :]

        def staged(start):
            return (start % MAX_STRIDE) * seg + start // MAX_STRIDE

        def put(dst_ref, ostage_slot, start, val):
            if d == 1:
                dst_ref[g, pl.ds(start, BLK), :] = val
            elif not two_pass:
                dst_ref[g, pl.ds(start, BLK, stride=d), :] = val
            else:
                ostage_ref[ostage_slot, pl.ds(staged(start), BLK, stride=d_in), :] = val

        def block(idx):
            r, i = idx // nb, idx % nb
            start = r + d * BLK * i
            q = src(0, start).astype(BF16)
            k_own, v_own = src(1, start), src(2, start)
            heads = (g * H_PER_DIL, g * H_PER_DIL + 1)
            prev = r + d * BLK * jnp.maximum(i - 1, 0)
            kcat = jnp.concatenate([src(1, prev), k_own], axis=0).astype(BF16)
            vcat = jnp.concatenate([src(2, prev), v_own], axis=0).astype(BF16)
            variant = jnp.where(i > 0, 0, 1)
            biases = [bias_ref[hd, variant] for hd in heads]
            out, lse = _band_pair(q, kcat, vcat, biases, None)
            put(og_ref, 0, start, out)
            put(lg_ref, 1, start, lse)

        def body(it, carry):
            for u in range(BAND_UNROLL):
                block(it * BAND_UNROLL + u)
            return carry

        lax.fori_loop(0, n_blocks // BAND_UNROLL, body, 0)

        if two_pass:
            def stage_out(cc, carry):
                c1, j = cc // (seg // BLK), cc % (seg // BLK)
                rows_in = pl.ds(pl.multiple_of(c1 * seg + j * BLK, BLK), BLK)
                rows_out = pl.ds(c1 + MAX_STRIDE * BLK * j, BLK, stride=MAX_STRIDE)
                og_ref[g, rows_out, :] = ostage_ref[0, rows_in, :]
                lg_ref[g, rows_out, :] = ostage_ref[1, rows_in, :]
                return carry

            lax.fori_loop(0, n_blocks, stage_out, 0)

    def merge(i, carry):
        sl = pl.ds(pl.multiple_of(i * BLK, BLK), BLK)
        l0, l1, l2 = lg_ref[0, sl, :], lg_ref[1, sl, :], lg_ref[2, sl, :]
        m = jnp.maximum(jnp.maximum(l0, l1), l2)
        e0, e1, e2 = jnp.exp(l0 - m), jnp.exp(l1 - m), jnp.exp(l2 - m)
        num = e0 * og_ref[0, sl, :] + e1 * og_ref[1, sl, :] + e2 * og_ref[2, sl, :]
        o_ref[0, sl, :] = (num / (e0 + e1 + e2)).astype(o_ref.dtype)
        return carry

    lax.fori_loop(0, n_blocks, merge, 0)


def _dil_attention(qkv_dil, buckets, table):
    bn, n_slabs, s, _ = qkv_dil.shape
    n_groups = len(DIL_PATTERNS)
    return pl.pallas_call(
        _dil_kernel,
        grid_spec=pltpu.PrefetchScalarGridSpec(
            num_scalar_prefetch=1,
            grid=(bn,),
            in_specs=[pl.BlockSpec((1, n_slabs, s, LANES), lambda b, tb: (b, 0, 0, 0)),
                      pl.BlockSpec(buckets.shape, lambda b, tb: (0, 0, 0))],
            out_specs=pl.BlockSpec((1, s, LANES), lambda b, tb: (b, 0, 0)),
            scratch_shapes=[pltpu.VMEM((H_DIL, 2, BLK, 2 * BLK), F32),
                            pltpu.VMEM((n_groups, s, LANES), F32),
                            pltpu.VMEM((n_groups, s, LANES), F32),
                            pltpu.VMEM((3, s, LANES), F32),
                            pltpu.VMEM((2, s, LANES), F32)],
        ),
        out_shape=jax.ShapeDtypeStruct((bn, s, LANES), BF16),
        compiler_params=pltpu.CompilerParams(dimension_semantics=("arbitrary",),
                                             vmem_limit_bytes=VMEM_LIMIT),
        name="dil_attn",
    )(table, qkv_dil, buckets)


def _swa_kernel(table_ref, sink_ref, qkv_ref, bucket_ref, o_ref, bias_ref):
    s_len = o_ref.shape[1]
    k_col = N_SWA_PAIRS * LANES
    v_col = k_col + LANES

    @pl.when(pl.program_id(0) == 0)
    def _():
        for slot, hd in enumerate(_SWA_HEAD_ORDER):
            normal, first = _band_bias_tiles(bucket_ref[0], table_ref, H_DIL + hd, SWA_WINDOW - 1)
            bias_ref[slot, 0] = normal
            bias_ref[slot, 1] = first

    def block(i):
        start = pl.multiple_of(i * BLK, BLK)
        prev = pl.multiple_of(jnp.maximum(i - 1, 0) * BLK, BLK)
        kcat = jnp.concatenate([qkv_ref[0, pl.ds(prev, BLK), k_col:k_col + LANES],
                                qkv_ref[0, pl.ds(start, BLK), k_col:k_col + LANES]], axis=0)
        vcat = jnp.concatenate([qkv_ref[0, pl.ds(prev, BLK), v_col:v_col + LANES],
                                qkv_ref[0, pl.ds(start, BLK), v_col:v_col + LANES]], axis=0)
        variant = jnp.where(i > 0, 0, 1)
        for p in range(N_SWA_PAIRS):
            q = qkv_ref[0, pl.ds(start, BLK), p * LANES:(p + 1) * LANES]
            sinks = [sink_ref[_SWA_HEAD_ORDER[2 * p + h]] for h in range(2)]
            biases = [bias_ref[2 * p + h, variant] for h in range(2)]
            out, _ = _band_pair(q, kcat, vcat, biases, sinks)
            o_ref[0, pl.ds(start, BLK), p * LANES:(p + 1) * LANES] = out.astype(o_ref.dtype)

    def body(it, carry):
        for u in range(SWA_UNROLL):
            block(it * SWA_UNROLL + u)
        return carry

    lax.fori_loop(0, s_len // BLK // SWA_UNROLL, body, 0)


def _swa_attention(qkv_swa, bucket, table, sinks_l):
    bn, s, n = qkv_swa.shape
    return pl.pallas_call(
        _swa_kernel,
        grid_spec=pltpu.PrefetchScalarGridSpec(
            num_scalar_prefetch=2,
            grid=(bn,),
            in_specs=[pl.BlockSpec((1, s, n), lambda b, tb, sk: (b, 0, 0)),
                      pl.BlockSpec(bucket.shape, lambda b, tb, sk: (0, 0, 0))],
            out_specs=pl.BlockSpec((1, s, N_SWA_PAIRS * LANES), lambda b, tb, sk: (b, 0, 0)),
            scratch_shapes=[pltpu.VMEM((H_SWA_Q, 2, BLK, 2 * BLK), F32)],
        ),
        out_shape=jax.ShapeDtypeStruct((bn, s, N_SWA_PAIRS * LANES), BF16),
        compiler_params=pltpu.CompilerParams(dimension_semantics=("arbitrary",),
                                             vmem_limit_bytes=VMEM_LIMIT),
        name="swa_attn",
    )(table, sinks_l, qkv_swa, bucket)


def _merge_kernel(x_ref, mod_ref, gain_ref, osb_ref, odil_ref, oswa_ref,
                  wgate_ref, wsb_ref, wdil_ref, wswa_ref, wout_ref, o_ref):
    x = x_ref[0]
    mod = mod_ref[0]
    d = x.shape[1]
    h = _prenorm(x, gain_ref[...], mod, 1).astype(BF16)
    merged = None
    for t, (o_br, w_br) in enumerate(((osb_ref, wsb_ref), (odil_ref, wdil_ref), (oswa_ref, wswa_ref))):
        gate = jax.nn.sigmoid(jnp.dot(h, wgate_ref[:, t * d:(t + 1) * d], preferred_element_type=F32))
        term = gate * jnp.dot(o_br[0], w_br[...], preferred_element_type=F32)
        merged = term if merged is None else merged + term
    y = jnp.dot(merged.astype(BF16), wout_ref[...], preferred_element_type=F32)
    o_ref[0] = x + mod[5:6] * y


def _merge(x, mod, gain, o_sb, o_dil, o_swa, w_gate, w_sb, w_dil, w_swa, w_out):
    bn, s, d = x.shape
    row = lambda n: pl.BlockSpec((1, ROW_TILE, n), lambda b, i: (b, i, 0))
    return pl.pallas_call(
        _merge_kernel,
        grid=(bn, s // ROW_TILE),
        in_specs=[row(d),
                  pl.BlockSpec((1, ADA_CHUNKS, d), lambda b, i: (b, 0, 0)),
                  _const_spec((1, d)),
                  row(o_sb.shape[2]), row(o_dil.shape[2]), row(o_swa.shape[2]),
                  _const_spec(w_gate.shape), _const_spec(w_sb.shape), _const_spec(w_dil.shape),
                  _const_spec(w_swa.shape), _const_spec(w_out.shape)],
        out_specs=row(d),
        out_shape=jax.ShapeDtypeStruct(x.shape, F32),
        compiler_params=_cparams(2),
        name="merge",
    )(x, mod, gain.reshape(1, d), o_sb, o_dil, o_swa, w_gate, w_sb, w_dil, w_swa, w_out)


def _t5_bucket(n):
    max_exact = N_BUCKETS // 2
    nf = jnp.maximum(n, 1).astype(jnp.float32)
    large = max_exact + (jnp.log(nf / max_exact) / math.log(MAX_REL_DIST / max_exact)
                         * (N_BUCKETS - max_exact)).astype(jnp.int32)
    large = jnp.minimum(large, N_BUCKETS - 1)
    return jnp.where(n < max_exact, n, large)


def _band_buckets(dilation):
    rel = jnp.arange(BLK)[:, None] + BLK - jnp.arange(2 * BLK)[None, :]
    return _t5_bucket(jnp.maximum(rel, 0) * dilation).astype(jnp.int32)


def _reorder_heads(w, order, axis):
    parts = [lax.slice_in_dim(w, hd * HEAD_DIM, (hd + 1) * HEAD_DIM, axis=axis) for hd in order]
    return jnp.concatenate(parts, axis=axis)


def _split_w_in(w_in_l):
    d_model = w_in_l.shape[0]
    sizes = (H_SB * HEAD_DIM,) * 3 + (H_DIL * HEAD_DIM,) * 3 + (H_SWA_Q * HEAD_DIM,) \
        + (H_SWA_KV * HEAD_DIM,) * 2 + (d_model,) * 3
    offs = [0]
    for n in sizes:
        offs.append(offs[-1] + n)
    part = lambda t: w_in_l[:, offs[t]:offs[t + 1]]
    q_d, k_d, v_d = part(3), part(4), part(5)
    dil = []
    for g in range(len(DIL_PATTERNS)):
        cs = slice(g * LANES, (g + 1) * LANES)
        dil += [q_d[:, cs], k_d[:, cs], v_d[:, cs]]
    q_swa = _reorder_heads(part(6), _SWA_HEAD_ORDER, 1)
    w_qkv = jnp.concatenate([part(0), part(1), part(2)] + dil + [q_swa, part(7), part(8)], axis=1)
    w_gate = w_in_l[:, offs[9]:]
    return w_qkv.astype(BF16), w_gate.astype(BF16)


def kernel(x, c, w_ada, b_ada, norm_gain, w_ffn_gate, w_ffn_up, w_ffn_down, w_in,
           w_br_sb, w_br_dil, w_br_swa, w_out, sinks, rel_bias, final_gain):
    depth = w_ada.shape[0]
    bn, s, d = x.shape
    mods = _ada(c, w_ada, b_ada).reshape(depth, bn, ADA_CHUNKS, d)

    buckets = jnp.stack([_band_buckets(dil) for _, dil in DIL_PATTERNS])
    table = rel_bias.reshape(-1)

    for l in range(depth):
        mod = mods[l]
        bf = lambda w: w.astype(BF16)
        x = _ffn(x, mod, norm_gain[l, 0], bf(w_ffn_gate[l, 0]), bf(w_ffn_up[l, 0]), bf(w_ffn_down[l, 0]), 0)

        w_qkv, w_gate = _split_w_in(w_in[l])
        qkv_sb, qkv_dil, qkv_swa = _inproj(x, mod, norm_gain[l, 1], w_qkv)
        o_sb = _sb_attention(qkv_sb)
        o_dil = _dil_attention(qkv_dil, buckets, table)
        o_swa = _swa_attention(qkv_swa, buckets[:1], table, sinks[l])
        x = _merge(x, mod, norm_gain[l, 1], o_sb, o_dil, o_swa, w_gate,
                   bf(w_br_sb[l]), bf(w_br_dil[l]), bf(_reorder_heads(w_br_swa[l], _SWA_HEAD_ORDER, 0)),
                   bf(w_out[l]))

        last = l == depth - 1
        x = _ffn(x, mod, norm_gain[l, 2], bf(w_ffn_gate[l, 1]), bf(w_ffn_up[l, 1]), bf(w_ffn_down[l, 1]), 2,
                 final_gain=final_gain if last else None)
    return x
```

```python
import functools
import math

import jax
import jax.numpy as jnp
from jax import lax
from jax.experimental import pallas as pl
from jax.experimental.pallas import tpu as pltpu

F32 = jnp.float32
BF16 = jnp.bfloat16

HEAD_DIM = 64
LANES = 128
BLK = 128
H_SB = 4
DIL_PATTERNS = ((128, 1), (512, 4), (2048, 16))
H_PER_DIL = 2
H_DIL = H_PER_DIL * len(DIL_PATTERNS)
H_SWA_Q = 6
H_SWA_KV = 2
SWA_WINDOW = 128
N_SOFT_HEADS = H_DIL + H_SWA_Q
N_SWA_PAIRS = H_SWA_Q // 2
N_BUCKETS = 32
MAX_REL_DIST = 2048
RMS_EPS = 1e-6
ADA_CHUNKS = 9
NEG_BIG = -1e30
LOG2E = 1.4426950408889634

VMEM_LIMIT = 56 * 1024 * 1024
ROW_TILE = 512
FF_CHUNK = 256
ADA_TILE = 1024
SB_ROWS = 512
SB_KEY_BLOCKS = 2
SB_DEAD_LOG2 = -150.0
BAND_UNROLL = 8
MAX_STRIDE = 4
SWA_UNROLL = 4

_SWA_HEAD_ORDER = tuple(p + (H_SWA_Q // H_SWA_KV) * h for p in range(N_SWA_PAIRS) for h in range(2))


def _cparams(n_axes):
    return pltpu.CompilerParams(dimension_semantics=("parallel",) * n_axes,
                                vmem_limit_bytes=VMEM_LIMIT)


def _const_spec(shape):
    nd = len(shape)
    return pl.BlockSpec(shape, lambda *_: (0,) * nd, pipeline_mode=pl.Buffered(1))


def _slab_spec(arr, *lead):
    block = (None,) * len(lead) + tuple(arr.shape[len(lead):])
    return pl.BlockSpec(block, lambda *_: tuple(lead) + (0, 0), pipeline_mode=pl.Buffered(1))


def _ada_kernel(c_ref, w_ref, b_ref, o_ref):
    c = c_ref[...]
    sc = (c * jax.nn.sigmoid(c)).astype(BF16)
    o_ref[0] = jnp.dot(sc, w_ref[0].astype(BF16), preferred_element_type=F32) + b_ref[0]


def _ada(c, w_ada, b_ada):
    depth, d, n = w_ada.shape
    bn = c.shape[0]
    return pl.pallas_call(
        _ada_kernel,
        grid=(depth, n // ADA_TILE),
        in_specs=[pl.BlockSpec((bn, d), lambda l, j: (0, 0)),
                  pl.BlockSpec((1, d, ADA_TILE), lambda l, j: (l, 0, j)),
                  pl.BlockSpec((1, 1, ADA_TILE), lambda l, j: (l, 0, j))],
        out_specs=pl.BlockSpec((1, bn, ADA_TILE), lambda l, j: (l, 0, j)),
        out_shape=jax.ShapeDtypeStruct((depth, bn, n), F32),
        compiler_params=_cparams(2),
        name="ada_mod",
    )(c, w_ada, b_ada.reshape(depth, 1, n))


def _rms(x, gain):
    ms = jnp.mean(x * x, axis=-1, keepdims=True)
    return x * lax.rsqrt(ms + RMS_EPS) * gain


def _prenorm(x, gain, mod, j):
    shift = mod[3 * j:3 * j + 1]
    scale = mod[3 * j + 1:3 * j + 2]
    return _rms(x, gain) * (1.0 + scale) + shift


def _dot_nt(a, b):
    return lax.dot_general(a, b, (((1,), (1,)), ((), ())), preferred_element_type=F32)


def _ffn_kernel(*refs, j, final):
    if final:
        x_ref, mod_ref, gain_ref, wg_ref, wu_ref, wd_ref, fg_ref, o_ref = refs
    else:
        x_ref, mod_ref, gain_ref, wg_ref, wu_ref, wd_ref, o_ref = refs
    x = x_ref[0]
    mod = mod_ref[0]
    h = _prenorm(x, gain_ref[...], mod, j).astype(BF16)
    d_ff = wg_ref.shape[1]
    acc = jnp.zeros(x.shape, F32)
    for ci in range(d_ff // FF_CHUNK):
        cols = slice(ci * FF_CHUNK, (ci + 1) * FF_CHUNK)
        g = jnp.dot(h, wg_ref[:, cols], preferred_element_type=F32)
        u = jnp.dot(h, wu_ref[:, cols], preferred_element_type=F32)
        a = (g * jax.nn.sigmoid(g) * u).astype(BF16)
        acc = acc + jnp.dot(a, wd_ref[cols, :], preferred_element_type=F32)
    y = x + (0.5 * mod[3 * j + 2:3 * j + 3]) * acc
    if final:
        y = _rms(y, fg_ref[...])
    o_ref[0] = y


def _ffn(x, mod, gain, wg, wu, wd, layer, which, j, final_gain=None):
    bn, s, d = x.shape
    final = final_gain is not None
    in_specs = [pl.BlockSpec((1, ROW_TILE, d), lambda b, i: (b, i, 0)),
                pl.BlockSpec((1, ADA_CHUNKS, d), lambda b, i: (b, 0, 0)),
                _const_spec((1, d)),
                _slab_spec(wg, layer, which), _slab_spec(wu, layer, which), _slab_spec(wd, layer, which)]
    args = [x, mod, gain.reshape(1, d), wg, wu, wd]
    if final:
        in_specs.append(_const_spec((1, d)))
        args.append(final_gain.reshape(1, d))
    return pl.pallas_call(
        functools.partial(_ffn_kernel, j=j, final=final),
        grid=(bn, s // ROW_TILE),
        in_specs=in_specs,
        out_specs=pl.BlockSpec((1, ROW_TILE, d), lambda b, i: (b, i, 0)),
        out_shape=jax.ShapeDtypeStruct(x.shape, F32),
        compiler_params=_cparams(2),
        name="ffn_final" if final else f"ffn{j}",
    )(*args)


N_SB = 3 * H_SB * HEAD_DIM
N_DIL = 3 * H_DIL * HEAD_DIM
N_SWA = (H_SWA_Q + 2 * H_SWA_KV) * HEAD_DIM
N_QKV = N_SB + N_DIL + N_SWA
N_DIL_SLABS = N_DIL // LANES


def _inproj_kernel(x_ref, mod_ref, gain_ref, w_ref, wq_swa_ref, sb_ref, dil_ref, swa_ref):
    h = _prenorm(x_ref[0], gain_ref[...], mod_ref[0], 1).astype(BF16)
    qs = HEAD_DIM ** -0.5
    n_q_sb = H_SB * HEAD_DIM
    n_groups = len(DIL_PATTERNS)
    r = jnp.dot(h, w_ref[:, :N_SB + N_DIL], preferred_element_type=F32)
    sb_ref[0, :, :n_q_sb] = (r[:, :n_q_sb] * (qs * LOG2E)).astype(BF16)
    sb_ref[0, :, n_q_sb:] = r[:, n_q_sb:N_SB].astype(BF16)
    for a in range(3):
        for g in range(n_groups):
            c0 = N_SB + (a * n_groups + g) * LANES
            slab = r[:, c0:c0 + LANES]
            dil_ref[0, 3 * g + a] = slab * qs if a == 0 else slab
    n_q_swa = H_SWA_Q * HEAD_DIM
    kv0 = N_SB + N_DIL + n_q_swa
    rq = jnp.dot(h, wq_swa_ref[...], preferred_element_type=F32)
    swa_ref[0, :, :n_q_swa] = (rq * qs).astype(BF16)
    rkv = jnp.dot(h, w_ref[:, kv0:kv0 + 2 * H_SWA_KV * HEAD_DIM], preferred_element_type=F32)
    swa_ref[0, :, n_q_swa:] = rkv.astype(BF16)


def _inproj(x, mod, gain, w_in, layer, wq_swa):
    bn, s, d = x.shape
    return pl.pallas_call(
        _inproj_kernel,
        grid=(bn, s // ROW_TILE),
        in_specs=[pl.BlockSpec((1, ROW_TILE, d), lambda b, i: (b, i, 0)),
                  pl.BlockSpec((1, ADA_CHUNKS, d), lambda b, i: (b, 0, 0)),
                  _const_spec((1, d)),
                  _slab_spec(w_in, layer), _const_spec(wq_swa.shape)],
        out_specs=[pl.BlockSpec((1, ROW_TILE, N_SB), lambda b, i: (b, i, 0)),
                   pl.BlockSpec((1, N_DIL_SLABS, ROW_TILE, LANES), lambda b, i: (b, 0, i, 0)),
                   pl.BlockSpec((1, ROW_TILE, N_SWA), lambda b, i: (b, i, 0))],
        out_shape=[jax.ShapeDtypeStruct((bn, s, N_SB), BF16),
                   jax.ShapeDtypeStruct((bn, N_DIL_SLABS, s, LANES), F32),
                   jax.ShapeDtypeStruct((bn, s, N_SWA), BF16)],
        compiler_params=_cparams(2),
        name="in_proj",
    )(x, mod, gain.reshape(1, d), w_in, wq_swa)


def _sb_kernel(q_ref, k_ref, v_ref, o_ref, acc_ref, c_ref):
    s_len = q_ref.shape[1]
    per = SB_ROWS // BLK
    lane = lax.broadcasted_iota(jnp.int32, (BLK, LANES), 1)
    head0 = lane < HEAD_DIM
    row2 = lax.broadcasted_iota(jnp.int32, (BLK, 2 * BLK), 0)
    col2 = lax.broadcasted_iota(jnp.int32, (BLK, 2 * BLK), 1)
    strict2 = (col2 & (BLK - 1)) < row2
    rk = lax.broadcasted_iota(jnp.int32, (2 * BLK, 2 * BLK), 0)
    cn = lax.broadcasted_iota(jnp.int32, (2 * BLK, 2 * BLK), 1)
    neg_suffix_ones = jnp.where(((rk & (BLK - 1)) >= cn) | (cn >= BLK), -1.0, 0.0).astype(BF16)

    def by_head(t):
        zero = jnp.zeros_like(t)
        return jnp.concatenate([jnp.where(head0, t, zero), jnp.where(head0, zero, t)], axis=0)

    def mask_first_block(t):
        first = jnp.where(strict2, t[:BLK], 0.0)
        return first if t.shape[0] == BLK else jnp.concatenate([first, t[BLK:]], axis=0)

    def step(q, c, k0, diag):
        kk = by_head(k_ref[0, pl.ds(k0, BLK), :])
        vv = by_head(v_ref[0, pl.ds(k0, BLK), :])
        z = _dot_nt(q, kk)
        neg_abs = lax.bitcast_convert_type(lax.bitcast_convert_type(z, jnp.int32) | jnp.int32(-2 ** 31), F32)
        softplus2 = jnp.maximum(z, 0.0) + jnp.log2(1.0 + jnp.exp2(neg_abs))
        if diag:
            softplus2 = mask_first_block(softplus2)
        hi = softplus2.astype(BF16)
        lo = (softplus2 - hi.astype(F32)).astype(BF16)
        ws, tots = [], []
        for h in range(2):
            cols = slice(h * BLK, (h + 1) * BLK)
            ct = jnp.dot(jnp.concatenate([hi[:, cols], lo[:, cols]], axis=1), neg_suffix_ones,
                         preferred_element_type=F32)
            ws.append(jnp.exp2(z[:, cols] + ct[:, :BLK] + c[h]))
            tots.append(ct[:, BLK:])
        w = jnp.concatenate(ws, axis=1)
        if diag:
            w = mask_first_block(w)
        return jnp.dot(w.astype(BF16), vv, preferred_element_type=F32), tots

    def superblock(sb, carry):
        base = pl.multiple_of(sb * SB_ROWS, SB_ROWS)
        acc_ref[...] = jnp.zeros_like(acc_ref)
        c_ref[...] = jnp.zeros_like(c_ref)
        for kq in reversed(range(per)):
            rows = slice(kq * BLK, SB_ROWS)
            q = q_ref[0, pl.ds(base + kq * BLK, SB_ROWS - kq * BLK), :]
            c = (c_ref[0, rows, :], c_ref[1, rows, :])
            pv, tots = step(q, c, base + kq * BLK, True)
            acc_ref[rows, :] += pv
            c_ref[0, rows, :] = c[0] + tots[0]
            c_ref[1, rows, :] = c[1] + tots[1]
        q_all = q_ref[0, pl.ds(base, SB_ROWS), :]

        def kstep(state):
            jj, _ = state
            c = (c_ref[0], c_ref[1])
            pv_sum = None
            for u in range(SB_KEY_BLOCKS):
                k0 = pl.multiple_of(base - (SB_KEY_BLOCKS * jj + u + 1) * BLK, BLK)
                pv, tots = step(q_all, c, k0, False)
                c = (c[0] + tots[0], c[1] + tots[1])
                pv_sum = pv if pv_sum is None else pv_sum + pv
            c_ref[0] = c[0]
            c_ref[1] = c[1]
            acc_ref[...] += pv_sum
            return jj + 1, jnp.max(jnp.maximum(c[0], c[1])) > SB_DEAD_LOG2

        n_iter = sb * (per // SB_KEY_BLOCKS)
        lax.while_loop(lambda st: jnp.logical_and(st[0] < n_iter, st[1]), kstep,
                       (jnp.int32(0), jnp.bool_(True)))
        o_ref[0, pl.ds(base, SB_ROWS), :] = acc_ref[...].astype(o_ref.dtype)
        return carry

    lax.fori_loop(0, s_len // SB_ROWS, superblock, 0)


def _sb_attention(qkv_sb):
    bn, s, _ = qkv_sb.shape
    n_pairs = H_SB // 2
    spec = lambda off: pl.BlockSpec((1, s, LANES), lambda b, p: (b, 0, off + p))
    return pl.pallas_call(
        _sb_kernel,
        grid=(bn, n_pairs),
        in_specs=[spec(0), spec(n_pairs), spec(2 * n_pairs)],
        out_specs=pl.BlockSpec((1, s, LANES), lambda b, p: (b, 0, p)),
        out_shape=jax.ShapeDtypeStruct((bn, s, n_pairs * LANES), BF16),
        scratch_shapes=[pltpu.VMEM((SB_ROWS, LANES), F32), pltpu.VMEM((2, SB_ROWS, LANES), F32)],
        compiler_params=_cparams(2),
        name="sb_attn",
    )(qkv_sb, qkv_sb, qkv_sb)


def _band_bias_tiles(bucket, table_ref, col, max_dist):
    row = lax.broadcasted_iota(jnp.int32, (BLK, 2 * BLK), 0)
    kcol = lax.broadcasted_iota(jnp.int32, (BLK, 2 * BLK), 1)
    rel = row + BLK - kcol
    in_band = (rel >= 0) & (rel <= max_dist)
    bias = jnp.zeros((BLK, 2 * BLK), F32)
    for n in range(N_BUCKETS):
        bias = jnp.where(bucket == n, table_ref[n * N_SOFT_HEADS + col], bias)
    return jnp.where(in_band, bias, NEG_BIG), jnp.where(in_band & (kcol >= BLK), bias, NEG_BIG)


def _band_pair(q, kcat, vcat, biases, sinks):
    lane = lax.broadcasted_iota(jnp.int32, (BLK, LANES), 1)
    head0 = lane < HEAD_DIM
    zero = jnp.zeros_like(q)
    outs, lses = [], []
    for h in range(2):
        qh = jnp.where(head0, q, zero) if h == 0 else jnp.where(head0, zero, q)
        s = _dot_nt(qh, kcat) + biases[h]
        m = jnp.max(s, axis=-1, keepdims=True)
        if sinks is not None:
            m = jnp.maximum(m, sinks[h])
        p = jnp.exp(s - m)
        den = jnp.sum(p, axis=-1, keepdims=True)
        if sinks is not None:
            den = den + jnp.exp(sinks[h] - m)
        o = jnp.dot(p.astype(BF16), vcat, preferred_element_type=F32)
        outs.append(o * (1.0 / den))
        lses.append(m + jnp.log(den))
    out = jnp.where(head0, outs[0], outs[1])
    lse = jnp.where(head0, lses[0], lses[1])
    return out, lse


def _dil_kernel(table_ref, qkv_ref, bucket_ref, o_ref, bias_ref, og_ref, lg_ref, stage_ref, ostage_ref):
    s_len = o_ref.shape[1]
    n_blocks = s_len // BLK

    @pl.when(pl.program_id(0) == 0)
    def _():
        for g, (window, d) in enumerate(DIL_PATTERNS):
            for h in range(H_PER_DIL):
                head = g * H_PER_DIL + h
                normal, first = _band_bias_tiles(bucket_ref[g], table_ref, head, window // d)
                bias_ref[head, 0] = normal
                bias_ref[head, 1] = first

    for g, (window, d) in enumerate(DIL_PATTERNS):
        nb = s_len // d // BLK
        two_pass = d > MAX_STRIDE
        d_in = d // MAX_STRIDE if two_pass else d
        seg = s_len // MAX_STRIDE

        if two_pass:
            def stage_in(cc, carry):
                c1, j = cc // (seg // BLK), cc % (seg // BLK)
                for a in range(3):
                    stage_ref[a, pl.ds(pl.multiple_of(c1 * seg + j * BLK, BLK), BLK), :] = \
                        qkv_ref[0, 3 * g + a, pl.ds(c1 + MAX_STRIDE * BLK * j, BLK, stride=MAX_STRIDE), :]
                return carry

            lax.fori_loop(0, n_blocks, stage_in, 0)

        def src(a, start):
            if d == 1:
                return qkv_ref[0, 3 * g + a, pl.ds(start, BLK), :]
            if not two_pass:
                return qkv_ref[0, 3 * g + a, pl.ds(start, BLK, stride=d), :]
            return stage_ref[a, pl.ds(staged(start), BLK, stride=d_in), :]

        def staged(start):
            return (start % MAX_STRIDE) * seg + start // MAX_STRIDE

        def put(dst_ref, ostage_slot, start, val):
            if d == 1:
                dst_ref[g, pl.ds(start, BLK), :] = val
            elif not two_pass:
                dst_ref[g, pl.ds(start, BLK, stride=d), :] = val
            else:
                ostage_ref[ostage_slot, pl.ds(staged(start), BLK, stride=d_in), :] = val

        def block(idx):
            r, i = idx // nb, idx % nb
            start = r + d * BLK * i
            q = src(0, start).astype(BF16)
            k_own, v_own = src(1, start), src(2, start)
            heads = (g * H_PER_DIL, g * H_PER_DIL + 1)
            prev = r + d * BLK * jnp.maximum(i - 1, 0)
            kcat = jnp.concatenate([src(1, prev), k_own], axis=0).astype(BF16)
            vcat = jnp.concatenate([src(2, prev), v_own], axis=0).astype(BF16)
            variant = jnp.where(i > 0, 0, 1)
            biases = [bias_ref[hd, variant] for hd in heads]
            out, lse = _band_pair(q, kcat, vcat, biases, None)
            put(og_ref, 0, start, out)
            put(lg_ref, 1, start, lse)

        def body(it, carry):
            for u in range(BAND_UNROLL):
                block(it * BAND_UNROLL + u)
            return carry

        lax.fori_loop(0, n_blocks // BAND_UNROLL, body, 0)

        if two_pass:
            def stage_out(cc, carry):
                c1, j = cc // (seg // BLK), cc % (seg // BLK)
                rows_in = pl.ds(pl.multiple_of(c1 * seg + j * BLK, BLK), BLK)
                rows_out = pl.ds(c1 + MAX_STRIDE * BLK * j, BLK, stride=MAX_STRIDE)
                og_ref[g, rows_out, :] = ostage_ref[0, rows_in, :]
                lg_ref[g, rows_out, :] = ostage_ref[1, rows_in, :]
                return carry

            lax.fori_loop(0, n_blocks, stage_out, 0)

    def merge(i, carry):
        sl = pl.ds(pl.multiple_of(i * BLK, BLK), BLK)
        l0, l1, l2 = lg_ref[0, sl, :], lg_ref[1, sl, :], lg_ref[2, sl, :]
        m = jnp.maximum(jnp.maximum(l0, l1), l2)
        e0, e1, e2 = jnp.exp(l0 - m), jnp.exp(l1 - m), jnp.exp(l2 - m)
        num = e0 * og_ref[0, sl, :] + e1 * og_ref[1, sl, :] + e2 * og_ref[2, sl, :]
        o_ref[0, sl, :] = (num / (e0 + e1 + e2)).astype(o_ref.dtype)
        return carry

    lax.fori_loop(0, n_blocks, merge, 0)


def _dil_attention(qkv_dil, buckets, table):
    bn, n_slabs, s, _ = qkv_dil.shape
    n_groups = len(DIL_PATTERNS)
    return pl.pallas_call(
        _dil_kernel,
        grid_spec=pltpu.PrefetchScalarGridSpec(
            num_scalar_prefetch=1,
            grid=(bn,),
            in_specs=[pl.BlockSpec((1, n_slabs, s, LANES), lambda b, tb: (b, 0, 0, 0)),
                      pl.BlockSpec(buckets.shape, lambda b, tb: (0, 0, 0))],
            out_specs=pl.BlockSpec((1, s, LANES), lambda b, tb: (b, 0, 0)),
            scratch_shapes=[pltpu.VMEM((H_DIL, 2, BLK, 2 * BLK), F32),
                            pltpu.VMEM((n_groups, s, LANES), F32),
                            pltpu.VMEM((n_groups, s, LANES), F32),
                            pltpu.VMEM((3, s, LANES), F32),
                            pltpu.VMEM((2, s, LANES), F32)],
        ),
        out_shape=jax.ShapeDtypeStruct((bn, s, LANES), BF16),
        compiler_params=pltpu.CompilerParams(dimension_semantics=("arbitrary",),
                                             vmem_limit_bytes=VMEM_LIMIT),
        name="dil_attn",
    )(table, qkv_dil, buckets)


def _swa_kernel(table_ref, sink_ref, qkv_ref, bucket_ref, o_ref, bias_ref):
    s_len = o_ref.shape[1]
    k_col = N_SWA_PAIRS * LANES
    v_col = k_col + LANES

    @pl.when(pl.program_id(0) == 0)
    def _():
        for slot, hd in enumerate(_SWA_HEAD_ORDER):
            normal, first = _band_bias_tiles(bucket_ref[0], table_ref, H_DIL + hd, SWA_WINDOW - 1)
            bias_ref[slot, 0] = normal
            bias_ref[slot, 1] = first

    def block(i):
        start = pl.multiple_of(i * BLK, BLK)
        prev = pl.multiple_of(jnp.maximum(i - 1, 0) * BLK, BLK)
        kcat = jnp.concatenate([qkv_ref[0, pl.ds(prev, BLK), k_col:k_col + LANES],
                                qkv_ref[0, pl.ds(start, BLK), k_col:k_col + LANES]], axis=0)
        vcat = jnp.concatenate([qkv_ref[0, pl.ds(prev, BLK), v_col:v_col + LANES],
                                qkv_ref[0, pl.ds(start, BLK), v_col:v_col + LANES]], axis=0)
        variant = jnp.where(i > 0, 0, 1)
        for p in range(N_SWA_PAIRS):
            q = qkv_ref[0, pl.ds(start, BLK), p * LANES:(p + 1) * LANES]
            sinks = [sink_ref[_SWA_HEAD_ORDER[2 * p + h]] for h in range(2)]
            biases = [bias_ref[2 * p + h, variant] for h in range(2)]
            out, _ = _band_pair(q, kcat, vcat, biases, sinks)
            o_ref[0, pl.ds(start, BLK), p * LANES:(p + 1) * LANES] = out.astype(o_ref.dtype)

    def body(it, carry):
        for u in range(SWA_UNROLL):
            block(it * SWA_UNROLL + u)
        return carry

    lax.fori_loop(0, s_len // BLK // SWA_UNROLL, body, 0)


def _swa_attention(qkv_swa, bucket, table, sinks_l):
    bn, s, n = qkv_swa.shape
    return pl.pallas_call(
        _swa_kernel,
        grid_spec=pltpu.PrefetchScalarGridSpec(
            num_scalar_prefetch=2,
            grid=(bn,),
            in_specs=[pl.BlockSpec((1, s, n), lambda b, tb, sk: (b, 0, 0)),
                      pl.BlockSpec(bucket.shape, lambda b, tb, sk: (0, 0, 0))],
            out_specs=pl.BlockSpec((1, s, N_SWA_PAIRS * LANES), lambda b, tb, sk: (b, 0, 0)),
            scratch_shapes=[pltpu.VMEM((H_SWA_Q, 2, BLK, 2 * BLK), F32)],
        ),
        out_shape=jax.ShapeDtypeStruct((bn, s, N_SWA_PAIRS * LANES), BF16),
        compiler_params=pltpu.CompilerParams(dimension_semantics=("arbitrary",),
                                             vmem_limit_bytes=VMEM_LIMIT),
        name="swa_attn",
    )(table, sinks_l, qkv_swa, bucket)


def _merge_kernel(x_ref, mod_ref, gain_ref, osb_ref, odil_ref, oswa_ref,
                  win_ref, wsb_ref, wdil_ref, wswa_ref, wout_ref, o_ref):
    x = x_ref[0]
    mod = mod_ref[0]
    d = x.shape[1]
    gate0 = win_ref.shape[1] - 3 * d
    h = _prenorm(x, gain_ref[...], mod, 1).astype(BF16)
    merged = None
    for t, (o_br, w_br) in enumerate(((osb_ref, wsb_ref), (odil_ref, wdil_ref), (oswa_ref, wswa_ref))):
        w_gate = win_ref[:, gate0 + t * d:gate0 + (t + 1) * d]
        gate = jax.nn.sigmoid(jnp.dot(h, w_gate, preferred_element_type=F32))
        term = gate * jnp.dot(o_br[0], w_br[...], preferred_element_type=F32)
        merged = term if merged is None else merged + term
    y = jnp.dot(merged.astype(BF16), wout_ref[...], preferred_element_type=F32)
    o_ref[0] = x + mod[5:6] * y


def _merge(x, mod, gain, o_sb, o_dil, o_swa, w_in, w_sb, w_dil, w_swa, w_out, layer):
    bn, s, d = x.shape
    row = lambda n: pl.BlockSpec((1, ROW_TILE, n), lambda b, i: (b, i, 0))
    return pl.pallas_call(
        _merge_kernel,
        grid=(bn, s // ROW_TILE),
        in_specs=[row(d),
                  pl.BlockSpec((1, ADA_CHUNKS, d), lambda b, i: (b, 0, 0)),
                  _const_spec((1, d)),
                  row(o_sb.shape[2]), row(o_dil.shape[2]), row(o_swa.shape[2]),
                  _slab_spec(w_in, layer), _slab_spec(w_sb, layer), _slab_spec(w_dil, layer),
                  _const_spec(w_swa.shape), _slab_spec(w_out, layer)],
        out_specs=row(d),
        out_shape=jax.ShapeDtypeStruct(x.shape, F32),
        compiler_params=_cparams(2),
        name="merge",
    )(x, mod, gain.reshape(1, d), o_sb, o_dil, o_swa, w_in, w_sb, w_dil, w_swa, w_out)


def _t5_bucket(n):
    max_exact = N_BUCKETS // 2
    nf = jnp.maximum(n, 1).astype(jnp.float32)
    large = max_exact + (jnp.log(nf / max_exact) / math.log(MAX_REL_DIST / max_exact)
                         * (N_BUCKETS - max_exact)).astype(jnp.int32)
    large = jnp.minimum(large, N_BUCKETS - 1)
    return jnp.where(n < max_exact, n, large)


def _band_buckets(dilation):
    rel = jnp.arange(BLK)[:, None] + BLK - jnp.arange(2 * BLK)[None, :]
    return _t5_bucket(jnp.maximum(rel, 0) * dilation).astype(jnp.int32)


def _reorder_heads(w, order, axis):
    parts = [lax.slice_in_dim(w, hd * HEAD_DIM, (hd + 1) * HEAD_DIM, axis=axis) for hd in order]
    return jnp.concatenate(parts, axis=axis)


def kernel(x, c, w_ada, b_ada, norm_gain, w_ffn_gate, w_ffn_up, w_ffn_down, w_in,
           w_br_sb, w_br_dil, w_br_swa, w_out, sinks, rel_bias, final_gain):
    depth = w_ada.shape[0]
    bn, s, d = x.shape
    mods = _ada(c, w_ada, b_ada).reshape(depth, bn, ADA_CHUNKS, d)

    buckets = jnp.stack([_band_buckets(dil) for _, dil in DIL_PATTERNS])
    table = rel_bias.reshape(-1)
    wg, wu, wd = w_ffn_gate.astype(BF16), w_ffn_up.astype(BF16), w_ffn_down.astype(BF16)
    w_in_bf, w_sb_bf, w_dil_bf, w_out_bf = (w_in.astype(BF16), w_br_sb.astype(BF16),
                                            w_br_dil.astype(BF16), w_out.astype(BF16))
    q_swa0 = N_SB + N_DIL

    for l in range(depth):
        mod = mods[l]
        x = _ffn(x, mod, norm_gain[l, 0], wg, wu, wd, l, 0, 0)

        wq_swa = _reorder_heads(w_in[l, :, q_swa0:q_swa0 + H_SWA_Q * HEAD_DIM], _SWA_HEAD_ORDER, 1).astype(BF16)
        qkv_sb, qkv_dil, qkv_swa = _inproj(x, mod, norm_gain[l, 1], w_in_bf, l, wq_swa)
        o_sb = _sb_attention(qkv_sb)
        o_dil = _dil_attention(qkv_dil, buckets, table)
        o_swa = _swa_attention(qkv_swa, buckets[:1], table, sinks[l])
        w_swa = _reorder_heads(w_br_swa[l], _SWA_HEAD_ORDER, 0).astype(BF16)
        x = _merge(x, mod, norm_gain[l, 1], o_sb, o_dil, o_swa, w_in_bf, w_sb_bf, w_dil_bf, w_swa, w_out_bf, l)

        last = l == depth - 1
        x = _ffn(x, mod, norm_gain[l, 2], wg, wu, wd, l, 1, 2, final_gain=final_gain if last else None)
    return x
```

```python
import functools
import math

import jax
import jax.numpy as jnp
from jax import lax
from jax.experimental import pallas as pl
from jax.experimental.pallas import tpu as pltpu

F32 = jnp.float32
BF16 = jnp.bfloat16

HEAD_DIM = 64
LANES = 128
BLK = 128
H_SB = 4
DIL_PATTERNS = ((128, 1), (512, 4), (2048, 16))
H_PER_DIL = 2
H_DIL = H_PER_DIL * len(DIL_PATTERNS)
H_SWA_Q = 6
H_SWA_KV = 2
SWA_WINDOW = 128
N_SOFT_HEADS = H_DIL + H_SWA_Q
N_SWA_PAIRS = H_SWA_Q // 2
N_BUCKETS = 32
MAX_REL_DIST = 2048
RMS_EPS = 1e-6
ADA_CHUNKS = 9
NEG_BIG = -1e30
LOG2E = 1.4426950408889634

VMEM_LIMIT = 56 * 1024 * 1024
ROW_TILE = 512
FF_CHUNK = 256
ADA_TILE = 1024
SB_ROWS = 512
SB_KEY_BLOCKS = 2
SB_DEAD_LOG2 = -150.0
BAND_UNROLL = 16
MAX_STRIDE = 4
SWA_UNROLL = 8

_SWA_HEAD_ORDER = tuple(p + (H_SWA_Q // H_SWA_KV) * h for p in range(N_SWA_PAIRS) for h in range(2))


def _cparams(n_axes):
    return pltpu.CompilerParams(dimension_semantics=("parallel",) * n_axes,
                                vmem_limit_bytes=VMEM_LIMIT)


def _const_spec(shape):
    nd = len(shape)
    return pl.BlockSpec(shape, lambda *_: (0,) * nd, pipeline_mode=pl.Buffered(1))


def _slab_spec(arr, *lead):
    block = (None,) * len(lead) + tuple(arr.shape[len(lead):])
    return pl.BlockSpec(block, lambda *_: tuple(lead) + (0, 0), pipeline_mode=pl.Buffered(1))


def _ada_kernel(c_ref, w_ref, b_ref, o_ref):
    c = c_ref[...]
    sc = (c * jax.nn.sigmoid(c)).astype(BF16)
    o_ref[0] = jnp.dot(sc, w_ref[0].astype(BF16), preferred_element_type=F32) + b_ref[0]


def _ada(c, w_ada, b_ada):
    depth, d, n = w_ada.shape
    bn = c.shape[0]
    return pl.pallas_call(
        _ada_kernel,
        grid=(depth, n // ADA_TILE),
        in_specs=[pl.BlockSpec((bn, d), lambda l, j: (0, 0)),
                  pl.BlockSpec((1, d, ADA_TILE), lambda l, j: (l, 0, j)),
                  pl.BlockSpec((1, 1, ADA_TILE), lambda l, j: (l, 0, j))],
        out_specs=pl.BlockSpec((1, bn, ADA_TILE), lambda l, j: (l, 0, j)),
        out_shape=jax.ShapeDtypeStruct((depth, bn, n), F32),
        compiler_params=_cparams(2),
        name="ada_mod",
    )(c, w_ada, b_ada.reshape(depth, 1, n))


def _rms(x, gain):
    ms = jnp.mean(x * x, axis=-1, keepdims=True)
    return x * lax.rsqrt(ms + RMS_EPS) * gain


def _prenorm(x, gain, mod, j):
    shift = mod[3 * j:3 * j + 1]
    scale = mod[3 * j + 1:3 * j + 2]
    ms = jnp.mean(x * x, axis=-1, keepdims=True)
    return (x * lax.rsqrt(ms + RMS_EPS)) * (gain * (1.0 + scale)) + shift


def _dot_nt(a, b):
    return lax.dot_general(a, b, (((1,), (1,)), ((), ())), preferred_element_type=F32)


def _ffn_kernel(*refs, j, final):
    if final:
        x_ref, mod_ref, gain_ref, wg_ref, wu_ref, wd_ref, fg_ref, o_ref = refs
    else:
        x_ref, mod_ref, gain_ref, wg_ref, wu_ref, wd_ref, o_ref = refs
    x = x_ref[0]
    mod = mod_ref[0]
    h = _prenorm(x, gain_ref[...], mod, j).astype(BF16)
    d_ff = wg_ref.shape[1]
    acc = jnp.zeros(x.shape, F32)
    for ci in range(d_ff // FF_CHUNK):
        cols = slice(ci * FF_CHUNK, (ci + 1) * FF_CHUNK)
        g = jnp.dot(h, wg_ref[:, cols], preferred_element_type=F32)
        u = jnp.dot(h, wu_ref[:, cols], preferred_element_type=F32)
        a = (g * jax.nn.sigmoid(g) * u).astype(BF16)
        acc = acc + jnp.dot(a, wd_ref[cols, :], preferred_element_type=F32)
    y = x + (0.5 * mod[3 * j + 2:3 * j + 3]) * acc
    if final:
        y = _rms(y, fg_ref[...])
    o_ref[0] = y


def _ffn(x, mod, gain, wg, wu, wd, layer, which, j, final_gain=None):
    bn, s, d = x.shape
    final = final_gain is not None
    in_specs = [pl.BlockSpec((1, ROW_TILE, d), lambda b, i: (b, i, 0)),
                pl.BlockSpec((1, ADA_CHUNKS, d), lambda b, i: (b, 0, 0)),
                _const_spec((1, d)),
                _slab_spec(wg, layer, which), _slab_spec(wu, layer, which), _slab_spec(wd, layer, which)]
    args = [x, mod, gain.reshape(1, d), wg, wu, wd]
    if final:
        in_specs.append(_const_spec((1, d)))
        args.append(final_gain.reshape(1, d))
    return pl.pallas_call(
        functools.partial(_ffn_kernel, j=j, final=final),
        grid=(bn, s // ROW_TILE),
        in_specs=in_specs,
        out_specs=pl.BlockSpec((1, ROW_TILE, d), lambda b, i: (b, i, 0)),
        out_shape=jax.ShapeDtypeStruct(x.shape, F32),
        compiler_params=_cparams(2),
        name="ffn_final" if final else f"ffn{j}",
    )(*args)


N_SB = 3 * H_SB * HEAD_DIM
N_DIL = 3 * H_DIL * HEAD_DIM
N_SWA = (H_SWA_Q + 2 * H_SWA_KV) * HEAD_DIM
N_QKV = N_SB + N_DIL + N_SWA
N_DIL_SLABS = N_DIL // LANES


def _inproj_kernel(x_ref, mod_ref, gain_ref, w_ref, wq_swa_ref, sb_ref, dil_ref, swa_ref, hout_ref):
    qs = HEAD_DIM ** -0.5
    n_q_sb = H_SB * HEAD_DIM
    n_groups = len(DIL_PATTERNS)
    n_q_swa = H_SWA_Q * HEAD_DIM
    kv0 = N_SB + N_DIL + n_q_swa
    h = _prenorm(x_ref[0], gain_ref[...], mod_ref[0], 1).astype(BF16)
    hout_ref[0] = h
    r = jnp.dot(h, w_ref[:, :N_SB + N_DIL], preferred_element_type=F32)
    sb_ref[0, :, :n_q_sb] = (r[:, :n_q_sb] * (qs * LOG2E)).astype(BF16)
    sb_ref[0, :, n_q_sb:] = r[:, n_q_sb:N_SB].astype(BF16)
    for a in range(3):
        for g in range(n_groups):
            c0 = N_SB + (a * n_groups + g) * LANES
            slab = r[:, c0:c0 + LANES]
            dil_ref[0, 3 * g + a] = slab * qs if a == 0 else slab
    rq = jnp.dot(h, wq_swa_ref[...], preferred_element_type=F32)
    swa_ref[0, :, :n_q_swa] = (rq * qs).astype(BF16)
    rkv = jnp.dot(h, w_ref[:, kv0:kv0 + 2 * H_SWA_KV * HEAD_DIM], preferred_element_type=F32)
    swa_ref[0, :, n_q_swa:] = rkv.astype(BF16)


def _inproj(x, mod, gain, w_in, layer, wq_swa):
    bn, s, d = x.shape
    nt = s // ROW_TILE
    return pl.pallas_call(
        _inproj_kernel,
        grid=(bn, nt),
        in_specs=[pl.BlockSpec((1, ROW_TILE, d), lambda b, i: (b, i, 0)),
                  pl.BlockSpec((1, ADA_CHUNKS, d), lambda b, i: (b, 0, 0)),
                  _const_spec((1, d)), _slab_spec(w_in, layer), _const_spec(wq_swa.shape)],
        out_specs=[pl.BlockSpec((1, ROW_TILE, N_SB), lambda b, i: (b, i, 0)),
                   pl.BlockSpec((1, N_DIL_SLABS, ROW_TILE, LANES), lambda b, i: (b, 0, i, 0)),
                   pl.BlockSpec((1, ROW_TILE, N_SWA), lambda b, i: (b, i, 0)),
                   pl.BlockSpec((1, ROW_TILE, d), lambda b, i: (b, i, 0))],
        out_shape=[jax.ShapeDtypeStruct((bn, s, N_SB), BF16),
                   jax.ShapeDtypeStruct((bn, N_DIL_SLABS, s, LANES), F32),
                   jax.ShapeDtypeStruct((bn, s, N_SWA), BF16),
                   jax.ShapeDtypeStruct((bn, s, d), BF16)],
        compiler_params=_cparams(2),
        name="in_proj",
    )(x, mod, gain.reshape(1, d), w_in, wq_swa)


def _sb_kernel(q_ref, k_ref, v_ref, o_ref, acc_ref, c_ref):
    s_len = q_ref.shape[1]
    n_pairs = q_ref.shape[2] // LANES
    per = SB_ROWS // BLK
    lane = lax.broadcasted_iota(jnp.int32, (BLK, LANES), 1)
    head0 = lane < HEAD_DIM
    row2 =lax.broadcasted_iota(jnp.int32, (BLK, 2 * BLK), 0)
    col2 = lax.broadcasted_iota(jnp.int32, (BLK, 2 * BLK), 1)
    strict2 = (col2 & (BLK - 1)) < row2
    rk = lax.broadcasted_iota(jnp.int32, (2 * BLK, 2 * BLK), 0)
    cn = lax.broadcasted_iota(jnp.int32, (2 * BLK, 2 * BLK), 1)
    neg_suffix_ones = jnp.where(((rk & (BLK - 1)) >= cn) | (cn >= BLK), -1.0, 0.0).astype(BF16)

    def by_head(t):
        zero = jnp.zeros_like(t)
        return jnp.concatenate([jnp.where(head0, t, zero), jnp.where(head0, zero, t)], axis=0)

    def mask_first_block(t):
        first = jnp.where(strict2, t[:BLK], 0.0)
        return first if t.shape[0] == BLK else jnp.concatenate([first, t[BLK:]], axis=0)

    def step(p, q, c, k0, diag):
        lanes = slice(p * LANES, (p + 1) * LANES)
        kk = by_head(k_ref[0, pl.ds(k0, BLK), lanes])
        vv = by_head(v_ref[0, pl.ds(k0, BLK), lanes])
        z = _dot_nt(q, kk)
        neg_abs = lax.bitcast_convert_type(lax.bitcast_convert_type(z, jnp.int32) | jnp.int32(-2 ** 31), F32)
        softplus2 = jnp.maximum(z, 0.0) + jnp.log2(1.0 + jnp.exp2(neg_abs))
        if diag:
            softplus2 = mask_first_block(softplus2)
        hi = softplus2.astype(BF16)
        lo = (softplus2 - hi.astype(F32)).astype(BF16)
        ws, tots = [], []
        for h in range(2):
            cols = slice(h * BLK, (h + 1) * BLK)
            ct = jnp.dot(jnp.concatenate([hi[:, cols], lo[:, cols]], axis=1), neg_suffix_ones,
                         preferred_element_type=F32)
            ws.append(jnp.exp2(z[:, cols] + ct[:, :BLK] + c[h]))
            tots.append(ct[:, BLK:])
        w = jnp.concatenate(ws, axis=1)
        if diag:
            w = mask_first_block(w)
        return jnp.dot(w.astype(BF16), vv, preferred_element_type=F32), tots

    def superblock(sb, carry):
        base = pl.multiple_of(sb * SB_ROWS, SB_ROWS)
        acc_ref[...] = jnp.zeros_like(acc_ref)
        c_ref[...] = jnp.zeros_like(c_ref)
        for kq in reversed(range(per)):
            rows = slice(kq * BLK, SB_ROWS)
            for p in range(n_pairs):
                q = q_ref[0, pl.ds(base + kq * BLK, SB_ROWS - kq * BLK), p * LANES:(p + 1) * LANES]
                c = (c_ref[p, 0, rows, :], c_ref[p, 1, rows, :])
                pv, tots = step(p, q, c, base + kq * BLK, True)
                acc_ref[p, rows, :] += pv
                c_ref[p, 0, rows, :] = c[0] + tots[0]
                c_ref[p, 1, rows, :] = c[1] + tots[1]

        def kstep(state):
            jj, _ = state
            live = None
            for p in range(n_pairs):
                q_all = q_ref[0, pl.ds(base, SB_ROWS), p * LANES:(p + 1) * LANES]
                c = (c_ref[p, 0], c_ref[p, 1])
                pv_sum = None
                for u in range(SB_KEY_BLOCKS):
                    k0 = pl.multiple_of(base - (SB_KEY_BLOCKS * jj + u + 1) * BLK, BLK)
                    pv, tots = step(p, q_all, c, k0, False)
                    c = (c[0] + tots[0], c[1] + tots[1])
                    pv_sum = pv if pv_sum is None else pv_sum + pv
                c_ref[p, 0] = c[0]
                c_ref[p, 1] = c[1]
                acc_ref[p] += pv_sum
                c_max = jnp.maximum(c[0], c[1])
                live = c_max if live is None else jnp.maximum(live, c_max)
            return jj + 1, jnp.max(live) > SB_DEAD_LOG2

        n_iter = sb * (per // SB_KEY_BLOCKS)
        lax.while_loop(lambda st: jnp.logical_and(st[0] < n_iter, st[1]), kstep,
                       (jnp.int32(0), jnp.bool_(True)))
        for p in range(n_pairs):
            o_ref[0, pl.ds(base, SB_ROWS), p * LANES:(p + 1) * LANES] = acc_ref[p].astype(o_ref.dtype)
        return carry

    lax.fori_loop(0, s_len // SB_ROWS, superblock, 0)


def _sb_attention(qkv_sb):
    bn, s, _ = qkv_sb.shape
    n_pairs = H_SB // 2
    width = n_pairs * LANES
    spec = lambda t: pl.BlockSpec((1, s, width), lambda b: (b, 0, t))
    return pl.pallas_call(
        _sb_kernel,
        grid=(bn,),
        in_specs=[spec(0), spec(1), spec(2)],
        out_specs=pl.BlockSpec((1, s, width), lambda b: (b, 0, 0)),
        out_shape=jax.ShapeDtypeStruct((bn, s, width), BF16),
        scratch_shapes=[pltpu.VMEM((n_pairs, SB_ROWS, LANES), F32),
                        pltpu.VMEM((n_pairs, 2, SB_ROWS, LANES), F32)],
        compiler_params=_cparams(1),
        name="sb_attn",
    )(qkv_sb, qkv_sb, qkv_sb)


def _band_bias_tiles(bucket, table_ref, col, max_dist, sink=None):
    row = lax.broadcasted_iota(jnp.int32, (BLK, 2 * BLK), 0)
    kcol = lax.broadcasted_iota(jnp.int32, (BLK, 2 * BLK), 1)
    rel = row + BLK - kcol
    in_band = (rel >= 0) & (rel <= max_dist)
    bias = jnp.zeros((BLK, 2 * BLK), F32)
    for n in range(N_BUCKETS):
        bias = jnp.where(bucket == n, table_ref[n * N_SOFT_HEADS + col], bias)
    tiles = [jnp.where(in_band, bias, NEG_BIG), jnp.where(in_band & (kcol >= BLK), bias, NEG_BIG)]
    if sink is not None:
        tiles = [jnp.where(kcol == 0, sink, t) for t in tiles]
    return tiles


def _band_pair(q, kcat, vcat, biases):
    return _band_finish(*_band_pv(_band_scores(q, kcat, biases), vcat))


def _band_scores(q, kcat, biases):
    head0 = lax.broadcasted_iota(jnp.int32, q.shape, 1) < HEAD_DIM
    zero = jnp.zeros_like(q)
    return [_dot_nt(jnp.where(head0, q, zero) if h == 0 else jnp.where(head0, zero, q), kcat) + biases[h]
            for h in range(2)]


def _band_pv(scores, vcat):
    one = jnp.ones_like(vcat)
    v_head0 = lax.broadcasted_iota(jnp.int32, vcat.shape, 1) < HEAD_DIM
    outs, ms = [], []
    for h, s in enumerate(scores):
        vh = jnp.where(v_head0, vcat, one) if h == 0 else jnp.where(v_head0, one, vcat)
        m = jnp.max(s, axis=-1, keepdims=True)
        p = jnp.exp(s - m)
        outs.append(jnp.dot(p.astype(BF16), vh, preferred_element_type=F32))
        ms.append(m)
    return outs, ms


def _band_finish(outs, ms):
    head0 = lax.broadcasted_iota(jnp.int32, outs[0].shape, 1) < HEAD_DIM
    num = jnp.where(head0, outs[0], outs[1])
    den = pltpu.roll(jnp.where(head0, outs[1], outs[0]), HEAD_DIM, axis=1)
    return num, den, jnp.where(head0, ms[0], ms[1])


def _dil_kernel(table_ref, qkv_ref, bucket_ref, o_ref, bias_ref, og_ref, dg_ref, mg_ref, stage_ref, ostage_ref):
    s_len = o_ref.shape[1]
    n_blocks = s_len // BLK

    @pl.when(pl.program_id(0) == 0)
    def _():
        for g, (window, d) in enumerate(DIL_PATTERNS):
            for h in range(H_PER_DIL):
                head = g * H_PER_DIL + h
                normal, first = _band_bias_tiles(bucket_ref[g], table_ref, head, window // d)
                bias_ref[head, 0] = normal
                bias_ref[head, 1] = first

    for g, (window, d) in enumerate(DIL_PATTERNS):
        nb = s_len // d // BLK
        two_pass = d > MAX_STRIDE
        d_in = d // MAX_STRIDE if two_pass else d
        seg = s_len // MAX_STRIDE

        if two_pass:
            def stage_in(cc, carry):
                c1, j = cc // (seg // BLK), cc % (seg // BLK)
                for a in range(3):
                    stage_ref[a, pl.ds(pl.multiple_of(c1 * seg + j * BLK, BLK), BLK), :] = \
                        qkv_ref[0, 3 * g + a, pl.ds(c1 + MAX_STRIDE * BLK * j, BLK, stride=MAX_STRIDE), :]
                return carry

            lax.fori_loop(0, n_blocks, stage_in, 0)

        def src(a, start):
            if d == 1:
                return qkv_ref[0, 3 * g + a, pl.ds(start, BLK), :]
            if not two_pass:
                return qkv_ref[0, 3 * g + a, pl.ds(start, BLK, stride=d), :]
            return stage_ref[a, pl.ds(staged(start), BLK, stride=d_in), :]

        def staged(start):
            return (start % MAX_STRIDE) * seg + start // MAX_STRIDE

        def put(dst_ref, ostage_slot, start, val):
            if d == 1:
                dst_ref[g, pl.ds(start, BLK), :] = val
            elif not two_pass:
                dst_ref[g, pl.ds(start, BLK, stride=d), :] = val
            else:
                ostage_ref[ostage_slot, pl.ds(staged(start), BLK, stride=d_in), :] = val

        def scores(idx):
            r, i = idx // nb, idx % nb
            start = r + d * BLK * i
            q = src(0, start).astype(BF16)
            heads = (g * H_PER_DIL, g * H_PER_DIL + 1)
            prev = r + d * BLK * jnp.maximum(i - 1, 0)
            kcat = jnp.concatenate([src(1, prev), src(1, start)], axis=0).astype(BF16)
            vcat = jnp.concatenate([src(2, prev), src(2, start)], axis=0).astype(BF16)
            variant = jnp.where(i > 0, 0, 1)
            biases = [bias_ref[hd, variant] for hd in heads]
            return start, _band_scores(q, kcat, biases), vcat

        def body(it, carry):
            staged1 = [scores(it * BAND_UNROLL + u) for u in range(BAND_UNROLL)]
            staged2 = [(start, _band_pv(s, vcat)) for start, s, vcat in staged1]
            for start, (outs, ms) in staged2:
                num, den, m = _band_finish(outs, ms)
                put(og_ref, 0, start, num)
                put(dg_ref, 1, start, den)
                put(mg_ref, 2, start, m)
            return carry

        lax.fori_loop(0, n_blocks // BAND_UNROLL, body, 0)

        if two_pass:
            def stage_out(cc, carry):
                c1, j = cc // (seg // BLK), cc % (seg // BLK)
                rows_in = pl.ds(pl.multiple_of(c1 * seg + j * BLK, BLK), BLK)
                rows_out = pl.ds(c1 + MAX_STRIDE * BLK * j, BLK, stride=MAX_STRIDE)
                og_ref[g, rows_out, :] = ostage_ref[0, rows_in, :]
                dg_ref[g, rows_out, :] = ostage_ref[1, rows_in, :]
                mg_ref[g, rows_out, :] = ostage_ref[2, rows_in, :]
                return carry

            lax.fori_loop(0, n_blocks, stage_out, 0)

    def merge(i, carry):
        sl = pl.ds(pl.multiple_of(i * BLK, BLK), BLK)
        m0, m1, m2 = mg_ref[0, sl, :], mg_ref[1, sl, :], mg_ref[2, sl, :]
        m = jnp.maximum(jnp.maximum(m0, m1), m2)
        e0, e1, e2 = jnp.exp(m0 - m), jnp.exp(m1 - m), jnp.exp(m2 - m)
        num = e0 * og_ref[0, sl, :] + e1 * og_ref[1, sl, :] + e2 * og_ref[2, sl, :]
        den = e0 * dg_ref[0, sl, :] + e1 * dg_ref[1, sl, :] + e2 * dg_ref[2, sl, :]
        o_ref[0, sl, :] = (num / den).astype(o_ref.dtype)
        return carry

    lax.fori_loop(0, n_blocks, merge, 0)


def _dil_attention(qkv_dil, buckets, table):
    bn, n_slabs, s, _ = qkv_dil.shape
    n_groups = len(DIL_PATTERNS)
    return pl.pallas_call(
        _dil_kernel,
        grid_spec=pltpu.PrefetchScalarGridSpec(
            num_scalar_prefetch=1,
            grid=(bn,),
            in_specs=[pl.BlockSpec((1, n_slabs, s, LANES), lambda b, tb: (b, 0, 0, 0)),
                      pl.BlockSpec(buckets.shape, lambda b, tb: (0, 0, 0))],
            out_specs=pl.BlockSpec((1, s, LANES), lambda b, tb: (b, 0, 0)),
            scratch_shapes=[pltpu.VMEM((H_DIL, 2, BLK, 2 * BLK), F32),
                            pltpu.VMEM((n_groups, s, LANES), F32),
                            pltpu.VMEM((n_groups, s, LANES), F32),
                            pltpu.VMEM((n_groups, s, LANES), F32),
                            pltpu.VMEM((3, s, LANES), F32),
                            pltpu.VMEM((3, s, LANES), F32)],
        ),
        out_shape=jax.ShapeDtypeStruct((bn, s, LANES), BF16),
        compiler_params=pltpu.CompilerParams(dimension_semantics=("arbitrary",),
                                             vmem_limit_bytes=VMEM_LIMIT),
        name="dil_attn",
    )(table, qkv_dil, buckets)


def _swa_kernel(table_ref, sink_ref, qkv_ref, bucket_ref, o_ref, bias_ref):
    s_len = o_ref.shape[1]
    k_col = N_SWA_PAIRS * LANES
    v_col = k_col + LANES
    key_row = lax.broadcasted_iota(jnp.int32, (2 * BLK, LANES), 0)

    @pl.when(pl.program_id(0) == 0)
    def _():
        for slot, hd in enumerate(_SWA_HEAD_ORDER):
            normal, first = _band_bias_tiles(bucket_ref[0], table_ref, H_DIL + hd, SWA_WINDOW - 1, sink_ref[hd])
            bias_ref[slot, 0] = normal
            bias_ref[slot, 1] = first

    def scores(i):
        start = pl.multiple_of(i * BLK, BLK)
        prev = pl.multiple_of(jnp.maximum(i - 1, 0) * BLK, BLK)
        kcat = jnp.concatenate([qkv_ref[0, pl.ds(prev, BLK), k_col:k_col + LANES],
                                qkv_ref[0, pl.ds(start, BLK), k_col:k_col + LANES]], axis=0)
        vcat = jnp.concatenate([qkv_ref[0, pl.ds(prev, BLK), v_col:v_col + LANES],
                                qkv_ref[0, pl.ds(start, BLK), v_col:v_col + LANES]], axis=0)
        kcat = jnp.where(key_row == 0, jnp.zeros_like(kcat), kcat)
        vcat = jnp.where(key_row == 0, jnp.zeros_like(vcat), vcat)
        variant = jnp.where(i > 0, 0, 1)
        out = []
        for p in range(N_SWA_PAIRS):
            q = qkv_ref[0, pl.ds(start, BLK), p * LANES:(p + 1) * LANES]
            biases = [bias_ref[2 * p + h, variant] for h in range(2)]
            out.append((start, p, _band_scores(q, kcat, biases), vcat))
        return out

    def body(it, carry):
        staged1 = [unit for u in range(SWA_UNROLL) for unit in scores(it * SWA_UNROLL + u)]
        staged2 = [(start, p, _band_pv(s, vcat)) for start, p, s, vcat in staged1]
        for start, p, (outs, ms) in staged2:
            num, den, _ = _band_finish(outs, ms)
            o_ref[0, pl.ds(start, BLK), p * LANES:(p + 1) * LANES] = (num * (1.0 / den)).astype(o_ref.dtype)
        return carry

    lax.fori_loop(0, s_len // BLK // SWA_UNROLL, body, 0)


def _swa_attention(qkv_swa, bucket, table, sinks_l):
    bn, s, n = qkv_swa.shape
    return pl.pallas_call(
        _swa_kernel,
        grid_spec=pltpu.PrefetchScalarGridSpec(
            num_scalar_prefetch=2,
            grid=(bn,),
            in_specs=[pl.BlockSpec((1, s, n), lambda b, tb, sk: (b, 0, 0)),
                      pl.BlockSpec(bucket.shape, lambda b, tb, sk: (0, 0, 0))],
            out_specs=pl.BlockSpec((1, s, N_SWA_PAIRS * LANES), lambda b, tb, sk: (b, 0, 0)),
            scratch_shapes=[pltpu.VMEM((H_SWA_Q, 2, BLK, 2 * BLK), F32)],
        ),
        out_shape=jax.ShapeDtypeStruct((bn, s, N_SWA_PAIRS * LANES), BF16),
        compiler_params=pltpu.CompilerParams(dimension_semantics=("arbitrary",),
                                             vmem_limit_bytes=VMEM_LIMIT),
        name="swa_attn",
    )(table, sinks_l, qkv_swa, bucket)


def _merge_kernel(x_ref, h_ref, mod_ref, osb_ref, odil_ref, oswa_ref,
                  win_ref, wsb_ref, wdil_ref, wswa_ref, wout_ref, o_ref):
    x = x_ref[0]
    mod = mod_ref[0]
    d = x.shape[1]
    gate0 = win_ref.shape[1] - 3 * d
    h = h_ref[0]
    merged = None
    for t, (o_br, w_br) in enumerate(((osb_ref, wsb_ref), (odil_ref, wdil_ref), (oswa_ref, wswa_ref))):
        w_gate = win_ref[:, gate0 + t * d:gate0 + (t + 1) * d]
        gate = jax.nn.sigmoid(jnp.dot(h, w_gate, preferred_element_type=F32))
        term = gate * jnp.dot(o_br[0], w_br[...], preferred_element_type=F32)
        merged = term if merged is None else merged + term
    y = jnp.dot(merged.astype(BF16), wout_ref[...], preferred_element_type=F32)
    o_ref[0] = x + mod[5:6] * y


def _merge(x, h, mod, o_sb, o_dil, o_swa, w_in, w_sb, w_dil, w_swa, w_out, layer):
    bn, s, d = x.shape
    row = lambda n: pl.BlockSpec((1, ROW_TILE, n), lambda b, i: (b, i, 0))
    return pl.pallas_call(
        _merge_kernel,
        grid=(bn, s // ROW_TILE),
        in_specs=[row(d), row(d),
                  pl.BlockSpec((1, ADA_CHUNKS, d), lambda b, i: (b, 0, 0)),
                  row(o_sb.shape[2]), row(o_dil.shape[2]), row(o_swa.shape[2]),
                  _slab_spec(w_in, layer), _slab_spec(w_sb, layer), _slab_spec(w_dil, layer),
                  _const_spec(w_swa.shape), _slab_spec(w_out, layer)],
        out_specs=row(d),
        out_shape=jax.ShapeDtypeStruct(x.shape, F32),
        compiler_params=_cparams(2),
        name="merge",
    )(x, h, mod, o_sb, o_dil, o_swa, w_in, w_sb, w_dil, w_swa, w_out)


def _t5_bucket(n):
    max_exact = N_BUCKETS // 2
    nf = jnp.maximum(n, 1).astype(jnp.float32)
    large = max_exact + (jnp.log(nf / max_exact) / math.log(MAX_REL_DIST / max_exact)
                         * (N_BUCKETS - max_exact)).astype(jnp.int32)
    large = jnp.minimum(large, N_BUCKETS - 1)
    return jnp.where(n < max_exact, n, large)


def _band_buckets(dilation):
    rel = jnp.arange(BLK)[:, None] + BLK - jnp.arange(2 * BLK)[None, :]
    return _t5_bucket(jnp.maximum(rel, 0) * dilation).astype(jnp.int32)


def _reorder_heads(w, order, axis):
    parts = [lax.slice_in_dim(w, hd * HEAD_DIM, (hd + 1) * HEAD_DIM, axis=axis) for hd in order]
    return jnp.concatenate(parts, axis=axis)


def kernel(x, c, w_ada, b_ada, norm_gain, w_ffn_gate, w_ffn_up, w_ffn_down, w_in,
           w_br_sb, w_br_dil, w_br_swa, w_out, sinks, rel_bias, final_gain):
    depth = w_ada.shape[0]
    bn, s, d = x.shape
    mods = _ada(c, w_ada, b_ada).reshape(depth, bn, ADA_CHUNKS, d)

    buckets = jnp.stack([_band_buckets(dil) for _, dil in DIL_PATTERNS])
    table = rel_bias.reshape(-1)
    wg, wu, wd = w_ffn_gate.astype(BF16), w_ffn_up.astype(BF16), w_ffn_down.astype(BF16)
    w_in_bf, w_sb_bf, w_dil_bf, w_out_bf = (w_in.astype(BF16), w_br_sb.astype(BF16),
                                            w_br_dil.astype(BF16), w_out.astype(BF16))
    q_swa0 = N_SB + N_DIL

    for l in range(depth):
        mod = mods[l]
        x = _ffn(x, mod, norm_gain[l, 0], wg, wu, wd, l, 0, 0)

        wq_swa = _reorder_heads(w_in[l, :, q_swa0:q_swa0 + H_SWA_Q * HEAD_DIM], _SWA_HEAD_ORDER, 1).astype(BF16)
        qkv_sb, qkv_dil, qkv_swa, h_mix = _inproj(x, mod, norm_gain[l, 1], w_in_bf, l, wq_swa)
        o_sb = _sb_attention(qkv_sb)
        o_dil = _dil_attention(qkv_dil, buckets, table)
        o_swa = _swa_attention(qkv_swa, buckets[:1], table, sinks[l])
        w_swa = _reorder_heads(w_br_swa[l], _SWA_HEAD_ORDER, 0).astype(BF16)
        x = _merge(x, h_mix, mod, o_sb, o_dil, o_swa, w_in_bf, w_sb_bf, w_dil_bf, w_swa, w_out_bf, l)

        last = l == depth - 1
        x = _ffn(x, mod, norm_gain[l, 2], wg, wu, wd, l, 1, 2, final_gain=final_gain if last else None)
    return x
```

```python
import functools
import math

import jax
import jax.numpy as jnp
from jax import lax
from jax.experimental import pallas as pl
from jax.experimental.pallas import tpu as pltpu

F32 = jnp.float32
BF16 = jnp.bfloat16

HEAD_DIM = 64
LANES = 128
BLK = 128
H_SB = 4
DIL_PATTERNS = ((128, 1), (512, 4), (2048, 16))
H_PER_DIL = 2
H_DIL = H_PER_DIL * len(DIL_PATTERNS)
H_SWA_Q = 6
H_SWA_KV = 2
SWA_WINDOW = 128
N_SOFT_HEADS = H_DIL + H_SWA_Q
N_SWA_PAIRS = H_SWA_Q // 2
N_BUCKETS = 32
MAX_REL_DIST = 2048
RMS_EPS = 1e-6
ADA_CHUNKS = 9
NEG_BIG = -1e30
LOG2E = 1.4426950408889634

VMEM_LIMIT = 56 * 1024 * 1024
ROW_TILE = 512
FF_CHUNK = 256
ADA_TILE = 3072
SB_ROWS = 512
SB_KEY_BLOCKS = 2
SB_DEAD_LOG2 = -150.0
BAND_UNROLL = 16
MAX_STRIDE = 4
SWA_UNROLL = 8

_SWA_HEAD_ORDER = tuple(p + (H_SWA_Q // H_SWA_KV) * h for p in range(N_SWA_PAIRS) for h in range(2))


def _cparams(n_axes):
    return pltpu.CompilerParams(dimension_semantics=("parallel",) * n_axes,
                                vmem_limit_bytes=VMEM_LIMIT)


def _const_spec(shape):
    nd = len(shape)
    return pl.BlockSpec(shape, lambda *_: (0,) * nd, pipeline_mode=pl.Buffered(1))


def _slab_spec(arr, *lead):
    block = (None,) * len(lead) + tuple(arr.shape[len(lead):])
    return pl.BlockSpec(block, lambda *_: tuple(lead) + (0, 0), pipeline_mode=pl.Buffered(1))


def _ada_kernel(c_ref, w_ref, b_ref, o_ref):
    c = c_ref[...]
    sc = (c * jax.nn.sigmoid(c)).astype(BF16)
    o_ref[0] = jnp.dot(sc, w_ref[0].astype(BF16), preferred_element_type=F32) + b_ref[0]


def _ada(c, w_ada, b_ada):
    depth, d, n = w_ada.shape
    bn = c.shape[0]
    return pl.pallas_call(
        _ada_kernel,
        grid=(depth, n // ADA_TILE),
        in_specs=[pl.BlockSpec((bn, d), lambda l, j: (0, 0)),
                  pl.BlockSpec((1, d, ADA_TILE), lambda l, j: (l, 0, j)),
                  pl.BlockSpec((1, 1, ADA_TILE), lambda l, j: (l, 0, j))],
        out_specs=pl.BlockSpec((1, bn, ADA_TILE), lambda l, j: (l, 0, j)),
        out_shape=jax.ShapeDtypeStruct((depth, bn, n), F32),
        compiler_params=_cparams(2),
        name="ada_mod",
    )(c, w_ada, b_ada.reshape(depth, 1, n))


def _rms(x, gain):
    ms = jnp.mean(x * x, axis=-1, keepdims=True)
    return x * lax.rsqrt(ms + RMS_EPS) * gain


def _prenorm(x, gain, mod, j):
    shift = mod[3 * j:3 * j + 1]
    scale = mod[3 * j + 1:3 * j + 2]
    ms = jnp.mean(x * x, axis=-1, keepdims=True)
    return (x * lax.rsqrt(ms + RMS_EPS)) * (gain * (1.0 + scale)) + shift


def _dot_nt(a, b):
    return lax.dot_general(a, b, (((1,), (1,)), ((), ())), preferred_element_type=F32)


def _ffn_kernel(*refs, j, final):
    if final:
        x_ref, mod_ref, gain_ref, wg_ref, wu_ref, wd_ref, fg_ref, o_ref = refs
    else:
        x_ref, mod_ref, gain_ref, wg_ref, wu_ref, wd_ref, o_ref = refs
    x = x_ref[0]
    mod = mod_ref[0]
    h = _prenorm(x, gain_ref[...], mod, j).astype(BF16)
    d_ff = wg_ref.shape[1]
    acc = jnp.zeros(x.shape, F32)
    for ci in range(d_ff // FF_CHUNK):
        cols = slice(ci * FF_CHUNK, (ci + 1) * FF_CHUNK)
        g = jnp.dot(h, wg_ref[:, cols], preferred_element_type=F32)
        u = jnp.dot(h, wu_ref[:, cols], preferred_element_type=F32)
        a = (g * jax.nn.sigmoid(g) * u).astype(BF16)
        acc = acc + jnp.dot(a, wd_ref[cols, :], preferred_element_type=F32)
    y = x + (0.5 * mod[3 * j + 2:3 * j + 3]) * acc
    if final:
        y = _rms(y, fg_ref[...])
    o_ref[0] = y


def _ffn(x, mod, gain, wg, wu, wd, layer, which, j, final_gain=None):
    bn, s, d = x.shape
    final = final_gain is not None
    in_specs = [pl.BlockSpec((1, ROW_TILE, d), lambda b, i: (b, i, 0)),
                pl.BlockSpec((1, ADA_CHUNKS, d), lambda b, i: (b, 0, 0)),
                _const_spec((1, d)),
                _slab_spec(wg, layer, which), _slab_spec(wu, layer, which), _slab_spec(wd, layer, which)]
    args = [x, mod, gain.reshape(1, d), wg, wu, wd]
    if final:
        in_specs.append(_const_spec((1, d)))
        args.append(final_gain.reshape(1, d))
    return pl.pallas_call(
        functools.partial(_ffn_kernel, j=j, final=final),
        grid=(bn, s // ROW_TILE),
        in_specs=in_specs,
        out_specs=pl.BlockSpec((1, ROW_TILE, d), lambda b, i: (b, i, 0)),
        out_shape=jax.ShapeDtypeStruct(x.shape, F32),
        compiler_params=_cparams(2),
        name="ffn_final" if final else f"ffn{j}",
    )(*args)


N_SB = 3 * H_SB * HEAD_DIM
N_DIL = 3 * H_DIL * HEAD_DIM
N_SWA = (H_SWA_Q + 2 * H_SWA_KV) * HEAD_DIM
N_QKV = N_SB + N_DIL + N_SWA
N_DIL_SLABS = N_DIL // LANES


N_INPROJ_HEAD = 7 * 256


def _inproj_kernel(x_ref, mod_ref, gain_ref, w_ref, wtail_ref, sb_ref, dil_ref, swa_ref, hout_ref):
    qs = HEAD_DIM ** -0.5
    n_q_sb = H_SB * HEAD_DIM
    n_groups = len(DIL_PATTERNS)
    n_q_swa = H_SWA_Q * HEAD_DIM
    h = _prenorm(x_ref[0], gain_ref[...], mod_ref[0], 1).astype(BF16)
    hout_ref[0] = h
    r = jnp.concatenate([jnp.dot(h, w_ref[:, :N_INPROJ_HEAD], preferred_element_type=F32),
                         jnp.dot(h, wtail_ref[...], preferred_element_type=F32)], axis=1)
    sb_ref[0, :, :n_q_sb] = (r[:, :n_q_sb] * (qs * LOG2E)).astype(BF16)
    sb_ref[0, :, n_q_sb:] = r[:, n_q_sb:N_SB].astype(BF16)
    for a in range(3):
        for g in range(n_groups):
            c0 = N_SB + (a * n_groups + g) * LANES
            slab = r[:, c0:c0 + LANES]
            dil_ref[0, 3 * g + a] = slab * qs if a == 0 else slab
    o = N_SB + N_DIL
    swa_ref[0, :, :n_q_swa] = (r[:, o:o + n_q_swa] * qs).astype(BF16)
    swa_ref[0, :, n_q_swa:] = r[:, o + n_q_swa:].astype(BF16)


def _inproj_tail(w_in_l, w_in_l_bf):
    q0 = N_SB + N_DIL
    q1 = q0 + H_SWA_Q * HEAD_DIM
    wq_swa = _reorder_heads(w_in_l[:, q0:q1], _SWA_HEAD_ORDER, 1).astype(BF16)
    return jnp.concatenate([w_in_l_bf[:, N_INPROJ_HEAD:q0], wq_swa, w_in_l_bf[:, q1:N_QKV]], axis=1)


def _inproj(x, mod, gain, w_in, layer, wq_swa):
    bn, s, d = x.shape
    nt = s // ROW_TILE
    return pl.pallas_call(
        _inproj_kernel,
        grid=(bn, nt),
        in_specs=[pl.BlockSpec((1, ROW_TILE, d), lambda b, i: (b, i, 0)),
                  pl.BlockSpec((1, ADA_CHUNKS, d), lambda b, i: (b, 0, 0)),
                  _const_spec((1, d)), _slab_spec(w_in, layer), _const_spec(wq_swa.shape)],
        out_specs=[pl.BlockSpec((1, ROW_TILE, N_SB), lambda b, i: (b, i, 0)),
                   pl.BlockSpec((1, N_DIL_SLABS, ROW_TILE, LANES), lambda b, i: (b, 0, i, 0)),
                   pl.BlockSpec((1, ROW_TILE, N_SWA), lambda b, i: (b, i, 0)),
                   pl.BlockSpec((1, ROW_TILE, d), lambda b, i: (b, i, 0))],
        out_shape=[jax.ShapeDtypeStruct((bn, s, N_SB), BF16),
                   jax.ShapeDtypeStruct((bn, N_DIL_SLABS, s, LANES), F32),
                   jax.ShapeDtypeStruct((bn, s, N_SWA), BF16),
                   jax.ShapeDtypeStruct((bn, s, d), BF16)],
        compiler_params=_cparams(2),
        name="in_proj",
    )(x, mod, gain.reshape(1, d), w_in, wq_swa)


def _sb_kernel(q_ref, k_ref, v_ref, o_ref, acc_ref, c_ref):
    s_len = q_ref.shape[1]
    n_pairs = q_ref.shape[2] // LANES
    per = SB_ROWS // BLK
    lane = lax.broadcasted_iota(jnp.int32, (BLK, LANES), 1)
    head0 = lane < HEAD_DIM
    row2 =lax.broadcasted_iota(jnp.int32, (BLK, 2 * BLK), 0)
    col2 = lax.broadcasted_iota(jnp.int32, (BLK, 2 * BLK), 1)
    strict2 = (col2 & (BLK - 1)) < row2
    rk = lax.broadcasted_iota(jnp.int32, (2 * BLK, 2 * BLK), 0)
    cn = lax.broadcasted_iota(jnp.int32, (2 * BLK, 2 * BLK), 1)
    neg_suffix_ones = jnp.where(((rk & (BLK - 1)) >= cn) | (cn >= BLK), -1.0, 0.0).astype(BF16)

    def by_head(t):
        zero = jnp.zeros_like(t)
        return jnp.concatenate([jnp.where(head0, t, zero), jnp.where(head0, zero, t)], axis=0)

    def mask_first_block(t):
        first = jnp.where(strict2, t[:BLK], 0.0)
        return first if t.shape[0] == BLK else jnp.concatenate([first, t[BLK:]], axis=0)

    def pair_lanes(p):
        return slice(p * LANES, (p + 1) * LANES)

    def scores(p, q, k0):
        return _dot_nt(q, by_head(k_ref[0, pl.ds(k0, BLK), pair_lanes(p)]))

    def carries(z, diag):
        neg_abs = lax.bitcast_convert_type(lax.bitcast_convert_type(z, jnp.int32) | jnp.int32(-2 ** 31), F32)
        softplus2 = jnp.maximum(z, 0.0) + jnp.log2(1.0 + jnp.exp2(neg_abs))
        if diag:
            softplus2 = mask_first_block(softplus2)
        hi = softplus2.astype(BF16)
        lo = (softplus2 - hi.astype(F32)).astype(BF16)
        return [jnp.dot(jnp.concatenate([hi[:, h * BLK:(h + 1) * BLK], lo[:, h * BLK:(h + 1) * BLK]], axis=1),
                        neg_suffix_ones, preferred_element_type=F32) for h in range(2)]

    def weighted_values(p, z, cts, c, k0, diag):
        ws = [jnp.exp2(z[:, h * BLK:(h + 1) * BLK] + cts[h][:, :BLK] + c[h]) for h in range(2)]
        w = jnp.concatenate(ws, axis=1)
        if diag:
            w = mask_first_block(w)
        vv = by_head(v_ref[0, pl.ds(k0, BLK), pair_lanes(p)])
        return jnp.dot(w.astype(BF16), vv, preferred_element_type=F32)

    def superblock(sb, carry):
        base = pl.multiple_of(sb * SB_ROWS, SB_ROWS)
        acc_ref[...] = jnp.zeros_like(acc_ref)
        c_ref[...] = jnp.zeros_like(c_ref)

        def run(units):
            zs = [scores(p, q_ref[0, pl.ds(base + lo, n), pair_lanes(p)], k0) for p, lo, n, k0, _ in units]
            cts = [carries(z, unit[4]) for z, unit in zip(zs, units)]
            for z, ct, (p, lo, n, k0, diag) in zip(zs, cts, units):
                rows = slice(lo, lo + n)
                c = (c_ref[p, 0, rows, :], c_ref[p, 1, rows, :])
                acc_ref[p, rows, :] += weighted_values(p, z, ct, c, k0, diag)
                c_ref[p, 0, rows, :] = c[0] + ct[0][:, BLK:]
                c_ref[p, 1, rows, :] = c[1] + ct[1][:, BLK:]

        def alive(lo, n):
            c_max = None
            for p in range(n_pairs):
                for h in range(2):
                    c = c_ref[p, h, lo:lo + n, :]
                    c_max = c if c_max is None else jnp.maximum(c_max, c)
            return jnp.max(c_max) > SB_DEAD_LOG2

        run([(p, kq * BLK, SB_ROWS - kq * BLK, base + kq * BLK, True)
             for kq in reversed(range(per)) for p in range(n_pairs)])

        n_iter = sb * (per // SB_KEY_BLOCKS)

        def sweep(window, watch, jj0):
            def kstep(state):
                jj, _ = state
                run([(p, window[0], window[1],
                      pl.multiple_of(base - (SB_KEY_BLOCKS * jj + u + 1) * BLK, BLK), False)
                     for u in range(SB_KEY_BLOCKS) for p in range(n_pairs)])
                return jj + 1, alive(*watch)

            jj, _ = lax.while_loop(lambda st: jnp.logical_and(st[0] < n_iter, st[1]), kstep,
                                   (jj0, alive(*watch)))
            return jj

        half = SB_ROWS // 2
        jj = sweep((0, SB_ROWS), (half, half), jnp.int32(0))
        sweep((0, half), (0, half), jj)
        for p in range(n_pairs):
            o_ref[0, pl.ds(base, SB_ROWS), pair_lanes(p)] = acc_ref[p].astype(o_ref.dtype)
        return carry

    lax.fori_loop(0, s_len // SB_ROWS, superblock, 0)


def _sb_attention(qkv_sb):
    bn, s, _ = qkv_sb.shape
    n_pairs = H_SB // 2
    width = n_pairs * LANES
    spec = lambda t: pl.BlockSpec((1, s, width), lambda b: (b, 0, t))
    return pl.pallas_call(
        _sb_kernel,
        grid=(bn,),
        in_specs=[spec(0), spec(1), spec(2)],
        out_specs=pl.BlockSpec((1, s, width), lambda b: (b, 0, 0)),
        out_shape=jax.ShapeDtypeStruct((bn, s, width), BF16),
        scratch_shapes=[pltpu.VMEM((n_pairs, SB_ROWS, LANES), F32),
                        pltpu.VMEM((n_pairs, 2, SB_ROWS, LANES), F32)],
        compiler_params=_cparams(1),
        name="sb_attn",
    )(qkv_sb, qkv_sb, qkv_sb)


def _band_bias_tiles(bucket, table_ref, col, max_dist, sink=None):
    row = lax.broadcasted_iota(jnp.int32, (BLK, 2 * BLK), 0)
    kcol = lax.broadcasted_iota(jnp.int32, (BLK, 2 * BLK), 1)
    rel = row + BLK - kcol
    in_band = (rel >= 0) & (rel <= max_dist)
    bias = jnp.zeros((BLK, 2 * BLK), F32)
    for n in range(N_BUCKETS):
        bias = jnp.where(bucket == n, table_ref[n * N_SOFT_HEADS + col], bias)
    tiles = [jnp.where(in_band, bias, NEG_BIG), jnp.where(in_band & (kcol >= BLK), bias, NEG_BIG)]
    if sink is not None:
        tiles = [jnp.where(kcol == 0, sink, t) for t in tiles]
    return tiles


def _band_pair(q, kcat, vcat, biases):
    return _band_finish(*_band_pv(_band_scores(q, kcat, biases), vcat))


def _band_scores(q, kcat, biases):
    head0 = lax.broadcasted_iota(jnp.int32, q.shape, 1) < HEAD_DIM
    zero = jnp.zeros_like(q)
    return [_dot_nt(jnp.where(head0, q, zero) if h == 0 else jnp.where(head0, zero, q), kcat) + biases[h]
            for h in range(2)]


def _band_pv(scores, vcat):
    one = jnp.ones_like(vcat)
    v_head0 = lax.broadcasted_iota(jnp.int32, vcat.shape, 1) < HEAD_DIM
    outs, ms = [], []
    for h, s in enumerate(scores):
        vh = jnp.where(v_head0, vcat, one) if h == 0 else jnp.where(v_head0, one, vcat)
        m = jnp.max(s, axis=-1, keepdims=True)
        p = jnp.exp(s - m)
        outs.append(jnp.dot(p.astype(BF16), vh, preferred_element_type=F32))
        ms.append(m)
    return outs, ms


def _band_finish(outs, ms):
    head0 = lax.broadcasted_iota(jnp.int32, outs[0].shape, 1) < HEAD_DIM
    num = jnp.where(head0, outs[0], outs[1])
    den = pltpu.roll(jnp.where(head0, outs[1], outs[0]), HEAD_DIM, axis=1)
    return num, den, jnp.where(head0, ms[0], ms[1])


def _dil_kernel(table_ref, qkv_ref, bucket_ref, o_ref, bias_ref, og_ref, dg_ref, mg_ref, stage_ref, ostage_ref):
    s_len = o_ref.shape[1]
    n_blocks = s_len // BLK

    @pl.when(pl.program_id(0) == 0)
    def _():
        for g, (window, d) in enumerate(DIL_PATTERNS):
            for h in range(H_PER_DIL):
                head = g * H_PER_DIL + h
                normal, first = _band_bias_tiles(bucket_ref[g], table_ref, head, window // d)
                bias_ref[head, 0] = normal
                bias_ref[head, 1] = first

    for g, (window, d) in enumerate(DIL_PATTERNS):
        nb = s_len // d // BLK
        two_pass = d > MAX_STRIDE
        d_in = d // MAX_STRIDE if two_pass else d
        seg = s_len // MAX_STRIDE

        if two_pass:
            def stage_in(cc, carry):
                c1, j = cc // (seg // BLK), cc % (seg // BLK)
                for a in range(3):
                    stage_ref[a, pl.ds(pl.multiple_of(c1 * seg + j * BLK, BLK), BLK), :] = \
                        qkv_ref[0, 3 * g + a, pl.ds(c1 + MAX_STRIDE * BLK * j, BLK, stride=MAX_STRIDE), :]
                return carry

            lax.fori_loop(0, n_blocks, stage_in, 0)

        def src(a, start):
            if d == 1:
                return qkv_ref[0, 3 * g + a, pl.ds(start, BLK), :]
            if not two_pass:
                return qkv_ref[0, 3 * g + a, pl.ds(start, BLK, stride=d), :]
            return stage_ref[a, pl.ds(staged(start), BLK, stride=d_in), :]

        def staged(start):
            return (start % MAX_STRIDE) * seg + start // MAX_STRIDE

        def put(dst_ref, ostage_slot, start, val):
            if d == 1:
                dst_ref[g, pl.ds(start, BLK), :] = val
            elif not two_pass:
                dst_ref[g, pl.ds(start, BLK, stride=d), :] = val
            else:
                ostage_ref[ostage_slot, pl.ds(staged(start), BLK, stride=d_in), :] = val

        def scores(idx):
            r, i = idx // nb, idx % nb
            start = r + d * BLK * i
            q = src(0, start).astype(BF16)
            heads = (g * H_PER_DIL, g * H_PER_DIL + 1)
            prev = r + d * BLK * jnp.maximum(i - 1, 0)
            kcat = jnp.concatenate([src(1, prev), src(1, start)], axis=0).astype(BF16)
            vcat = jnp.concatenate([src(2, prev), src(2, start)], axis=0).astype(BF16)
            variant = jnp.where(i > 0, 0, 1)
            biases = [bias_ref[hd, variant] for hd in heads]
            return start, _band_scores(q, kcat, biases), vcat

        def body(it, carry):
            staged1 = [scores(it * BAND_UNROLL + u) for u in range(BAND_UNROLL)]
            staged2 = [(start, _band_pv(s, vcat)) for start, s, vcat in staged1]
            for start, (outs, ms) in staged2:
                num, den, m = _band_finish(outs, ms)
                put(og_ref, 0, start, num)
                put(dg_ref, 1, start, den)
                put(mg_ref, 2, start, m)
            return carry

        lax.fori_loop(0, n_blocks // BAND_UNROLL, body, 0)

        if two_pass:
            def stage_out(cc, carry):
                c1, j = cc // (seg // BLK), cc % (seg // BLK)
                rows_in = pl.ds(pl.multiple_of(c1 * seg + j * BLK, BLK), BLK)
                rows_out = pl.ds(c1 + MAX_STRIDE * BLK * j, BLK, stride=MAX_STRIDE)
                og_ref[g, rows_out, :] = ostage_ref[0, rows_in, :]
                dg_ref[g, rows_out, :] = ostage_ref[1, rows_in, :]
                mg_ref[g, rows_out, :] = ostage_ref[2, rows_in, :]
                return carry

            lax.fori_loop(0, n_blocks, stage_out, 0)

    def merge(i, carry):
        sl = pl.ds(pl.multiple_of(i * BLK, BLK), BLK)
        m0, m1, m2 = mg_ref[0, sl, :], mg_ref[1, sl, :], mg_ref[2, sl, :]
        m = jnp.maximum(jnp.maximum(m0, m1), m2)
        e0, e1, e2 = jnp.exp(m0 - m), jnp.exp(m1 - m), jnp.exp(m2 - m)
        num = e0 * og_ref[0, sl, :] + e1 * og_ref[1, sl, :] + e2 * og_ref[2, sl, :]
        den = e0 * dg_ref[0, sl, :] + e1 * dg_ref[1, sl, :] + e2 * dg_ref[2, sl, :]
        o_ref[0, sl, :] = (num / den).astype(o_ref.dtype)
        return carry

    lax.fori_loop(0, n_blocks, merge, 0)


def _dil_attention(qkv_dil, buckets, table):
    bn, n_slabs, s, _ = qkv_dil.shape
    n_groups = len(DIL_PATTERNS)
    return pl.pallas_call(
        _dil_kernel,
        grid_spec=pltpu.PrefetchScalarGridSpec(
            num_scalar_prefetch=1,
            grid=(bn,),
            in_specs=[pl.BlockSpec((1, n_slabs, s, LANES), lambda b, tb: (b, 0, 0, 0)),
                      pl.BlockSpec(buckets.shape, lambda b, tb: (0, 0, 0))],
            out_specs=pl.BlockSpec((1, s, LANES), lambda b, tb: (b, 0, 0)),
            scratch_shapes=[pltpu.VMEM((H_DIL, 2, BLK, 2 * BLK), F32),
                            pltpu.VMEM((n_groups, s, LANES), F32),
                            pltpu.VMEM((n_groups, s, LANES), F32),
                            pltpu.VMEM((n_groups, s, LANES), F32),
                            pltpu.VMEM((3, s, LANES), F32),
                            pltpu.VMEM((3, s, LANES), F32)],
        ),
        out_shape=jax.ShapeDtypeStruct((bn, s, LANES), BF16),
        compiler_params=pltpu.CompilerParams(dimension_semantics=("arbitrary",),
                                             vmem_limit_bytes=VMEM_LIMIT),
        name="dil_attn",
    )(table, qkv_dil, buckets)


def _swa_kernel(table_ref, sink_ref, qkv_ref, bucket_ref, o_ref, bias_ref):
    s_len = o_ref.shape[1]
    k_col = N_SWA_PAIRS * LANES
    v_col = k_col + LANES
    key_row = lax.broadcasted_iota(jnp.int32, (2 * BLK, LANES), 0)

    @pl.when(pl.program_id(0) == 0)
    def _():
        for slot, hd in enumerate(_SWA_HEAD_ORDER):
            normal, first = _band_bias_tiles(bucket_ref[0], table_ref, H_DIL + hd, SWA_WINDOW - 1, sink_ref[hd])
            bias_ref[slot, 0] = normal
            bias_ref[slot, 1] = first

    def scores(i):
        start = pl.multiple_of(i * BLK, BLK)
        prev = pl.multiple_of(jnp.maximum(i - 1, 0) * BLK, BLK)
        kcat = jnp.concatenate([qkv_ref[0, pl.ds(prev, BLK), k_col:k_col + LANES],
                                qkv_ref[0, pl.ds(start, BLK), k_col:k_col + LANES]], axis=0)
        vcat = jnp.concatenate([qkv_ref[0, pl.ds(prev, BLK), v_col:v_col + LANES],
                                qkv_ref[0, pl.ds(start, BLK), v_col:v_col + LANES]], axis=0)
        kcat = jnp.where(key_row == 0, jnp.zeros_like(kcat), kcat)
        vcat = jnp.where(key_row == 0, jnp.zeros_like(vcat), vcat)
        variant = jnp.where(i > 0, 0, 1)
        out = []
        for p in range(N_SWA_PAIRS):
            q = qkv_ref[0, pl.ds(start, BLK), p * LANES:(p + 1) * LANES]
            biases = [bias_ref[2 * p + h, variant] for h in range(2)]
            out.append((start, p, _band_scores(q, kcat, biases), vcat))
        return out

    def body(it, carry):
        staged1 = [unit for u in range(SWA_UNROLL) for unit in scores(it * SWA_UNROLL + u)]
        staged2 = [(start, p, _band_pv(s, vcat)) for start, p, s, vcat in staged1]
        for start, p, (outs, ms) in staged2:
            num, den, _ = _band_finish(outs, ms)
            o_ref[0, pl.ds(start, BLK), p * LANES:(p + 1) * LANES] = (num * (1.0 / den)).astype(o_ref.dtype)
        return carry

    lax.fori_loop(0, s_len // BLK // SWA_UNROLL, body, 0)


def _swa_attention(qkv_swa, bucket, table, sinks_l):
    bn, s, n = qkv_swa.shape
    return pl.pallas_call(
        _swa_kernel,
        grid_spec=pltpu.PrefetchScalarGridSpec(
            num_scalar_prefetch=2,
            grid=(bn,),
            in_specs=[pl.BlockSpec((1, s, n), lambda b, tb, sk: (b, 0, 0)),
                      pl.BlockSpec(bucket.shape, lambda b, tb, sk: (0, 0, 0))],
            out_specs=pl.BlockSpec((1, s, N_SWA_PAIRS * LANES), lambda b, tb, sk: (b, 0, 0)),
            scratch_shapes=[pltpu.VMEM((H_SWA_Q, 2, BLK, 2 * BLK), F32)],
        ),
        out_shape=jax.ShapeDtypeStruct((bn, s, N_SWA_PAIRS * LANES), BF16),
        compiler_params=pltpu.CompilerParams(dimension_semantics=("arbitrary",),
                                             vmem_limit_bytes=VMEM_LIMIT),
        name="swa_attn",
    )(table, sinks_l, qkv_swa, bucket)


def _merge_kernel(x_ref, h_ref, mod_ref, osb_ref, odil_ref, oswa_ref,
                  win_ref, wsb_ref, wdil_ref, wswa_ref, wout_ref, o_ref):
    x = x_ref[0]
    mod = mod_ref[0]
    d = x.shape[1]
    gate0 = win_ref.shape[1] - 3 * d
    h = h_ref[0]
    merged = None
    for t, (o_br, w_br) in enumerate(((osb_ref, wsb_ref), (odil_ref, wdil_ref), (oswa_ref, wswa_ref))):
        w_gate = win_ref[:, gate0 + t * d:gate0 + (t + 1) * d]
        gate = jax.nn.sigmoid(jnp.dot(h, w_gate, preferred_element_type=F32))
        term = gate * jnp.dot(o_br[0], w_br[...], preferred_element_type=F32)
        merged = term if merged is None else merged + term
    y = jnp.dot(merged.astype(BF16), wout_ref[...], preferred_element_type=F32)
    o_ref[0] = x + mod[5:6] * y


def _merge(x, h, mod, o_sb, o_dil, o_swa, w_in, w_sb, w_dil, w_swa, w_out, layer):
    bn, s, d = x.shape
    row = lambda n: pl.BlockSpec((1, ROW_TILE, n), lambda b, i: (b, i, 0))
    return pl.pallas_call(
        _merge_kernel,
        grid=(bn, s // ROW_TILE),
        in_specs=[row(d), row(d),
                  pl.BlockSpec((1, ADA_CHUNKS, d), lambda b, i: (b, 0, 0)),
                  row(o_sb.shape[2]), row(o_dil.shape[2]), row(o_swa.shape[2]),
                  _slab_spec(w_in, layer), _slab_spec(w_sb, layer), _slab_spec(w_dil, layer),
                  _const_spec(w_swa.shape), _slab_spec(w_out, layer)],
        out_specs=row(d),
        out_shape=jax.ShapeDtypeStruct(x.shape, F32),
        compiler_params=_cparams(2),
        name="merge",
    )(x, h, mod, o_sb, o_dil, o_swa, w_in, w_sb, w_dil, w_swa, w_out)


def _t5_bucket(n):
    max_exact = N_BUCKETS // 2
    nf = jnp.maximum(n, 1).astype(jnp.float32)
    large = max_exact + (jnp.log(nf / max_exact) / math.log(MAX_REL_DIST / max_exact)
                         * (N_BUCKETS - max_exact)).astype(jnp.int32)
    large = jnp.minimum(large, N_BUCKETS - 1)
    return jnp.where(n < max_exact, n, large)


def _band_buckets(dilation):
    rel = jnp.arange(BLK)[:, None] + BLK - jnp.arange(2 * BLK)[None, :]
    return _t5_bucket(jnp.maximum(rel, 0) * dilation).astype(jnp.int32)


def _reorder_heads(w, order, axis):
    parts = [lax.slice_in_dim(w, hd * HEAD_DIM, (hd + 1) * HEAD_DIM, axis=axis) for hd in order]
    return jnp.concatenate(parts, axis=axis)


def kernel(x, c, w_ada, b_ada, norm_gain, w_ffn_gate, w_ffn_up, w_ffn_down, w_in,
           w_br_sb, w_br_dil, w_br_swa, w_out, sinks, rel_bias, final_gain):
    depth = w_ada.shape[0]
    bn, s, d = x.shape
    mods = _ada(c, w_ada, b_ada).reshape(depth, bn, ADA_CHUNKS, d)

    buckets = jnp.stack([_band_buckets(dil) for _, dil in DIL_PATTERNS])
    table = rel_bias.reshape(-1)
    wg, wu, wd = w_ffn_gate.astype(BF16), w_ffn_up.astype(BF16), w_ffn_down.astype(BF16)
    w_in_bf, w_sb_bf, w_dil_bf, w_out_bf = (w_in.astype(BF16), w_br_sb.astype(BF16),
                                            w_br_dil.astype(BF16), w_out.astype(BF16))

    for l in range(depth):
        mod = mods[l]
        x = _ffn(x, mod, norm_gain[l, 0], wg, wu, wd, l, 0, 0)

        qkv_sb, qkv_dil, qkv_swa, h_mix = _inproj(x, mod, norm_gain[l, 1], w_in_bf, l,
                                                  _inproj_tail(w_in[l], w_in_bf[l]))
        o_sb = _sb_attention(qkv_sb)
        o_dil = _dil_attention(qkv_dil, buckets, table)
        o_swa = _swa_attention(qkv_swa, buckets[:1], table, sinks[l])
        w_swa = _reorder_heads(w_br_swa[l], _SWA_HEAD_ORDER, 0).astype(BF16)
        x = _merge(x, h_mix, mod, o_sb, o_dil, o_swa, w_in_bf, w_sb_bf, w_dil_bf, w_swa, w_out_bf, l)

        last = l == depth - 1
        x = _ffn(x, mod, norm_gain[l, 2], wg, wu, wd, l, 1, 2, final_gain=final_gain if last else None)
    return x
```

```python
import functools
import math

import jax
import jax.numpy as jnp
from jax import lax
from jax.experimental import pallas as pl
from jax.experimental.pallas import tpu as pltpu

F32 = jnp.float32
BF16 = jnp.bfloat16

HEAD_DIM = 64
LANES = 128
BLK = 128
H_SB = 4
DIL_PATTERNS = ((128, 1), (512, 4), (2048, 16))
H_PER_DIL = 2
H_DIL = H_PER_DIL * len(DIL_PATTERNS)
H_SWA_Q = 6
H_SWA_KV = 2
SWA_WINDOW = 128
N_SOFT_HEADS = H_DIL + H_SWA_Q
N_SWA_PAIRS = H_SWA_Q // 2
N_BUCKETS = 32
MAX_REL_DIST = 2048
RMS_EPS = 1e-6
ADA_CHUNKS = 9
NEG_BIG = -1e30
LOG2E = 1.4426950408889634

VMEM_LIMIT = 56 * 1024 * 1024
ROW_TILE = 512
FF_CHUNK = 256
ADA_TILE = 3072
SB_ROWS = 512
SB_KEY_BLOCKS = 2
SB_DEAD_LOG2 = -150.0
BAND_UNROLL = 16
MAX_STRIDE = 4
SWA_UNROLL = 8

_SWA_HEAD_ORDER = tuple(p + (H_SWA_Q // H_SWA_KV) * h for p in range(N_SWA_PAIRS) for h in range(2))


def _cparams(n_axes):
    return pltpu.CompilerParams(dimension_semantics=("parallel",) * n_axes,
                                vmem_limit_bytes=VMEM_LIMIT)


def _const_spec(shape):
    nd = len(shape)
    return pl.BlockSpec(shape, lambda *_: (0,) * nd, pipeline_mode=pl.Buffered(1))


def _slab_spec(arr, *lead):
    block = (None,) * len(lead) + tuple(arr.shape[len(lead):])
    return pl.BlockSpec(block, lambda *_: tuple(lead) + (0, 0), pipeline_mode=pl.Buffered(1))


def _ada_kernel(c_ref, w_ref, b_ref, o_ref):
    c = c_ref[...]
    sc = (c * jax.nn.sigmoid(c)).astype(BF16)
    o_ref[0] = jnp.dot(sc, w_ref[0].astype(BF16), preferred_element_type=F32) + b_ref[0]


def _ada(c, w_ada, b_ada):
    depth, d, n = w_ada.shape
    bn = c.shape[0]
    return pl.pallas_call(
        _ada_kernel,
        grid=(depth, n // ADA_TILE),
        in_specs=[pl.BlockSpec((bn, d), lambda l, j: (0, 0)),
                  pl.BlockSpec((1, d, ADA_TILE), lambda l, j: (l, 0, j)),
                  pl.BlockSpec((1, 1, ADA_TILE), lambda l, j: (l, 0, j))],
        out_specs=pl.BlockSpec((1, bn, ADA_TILE), lambda l, j: (l, 0, j)),
        out_shape=jax.ShapeDtypeStruct((depth, bn, n), F32),
        compiler_params=_cparams(2),
        name="ada_mod",
    )(c, w_ada, b_ada.reshape(depth, 1, n))


def _rms(x, gain):
    ms = jnp.mean(x * x, axis=-1, keepdims=True)
    return x * lax.rsqrt(ms + RMS_EPS) * gain


def _prenorm(x, gain, mod, j):
    shift = mod[3 * j:3 * j + 1]
    scale = mod[3 * j + 1:3 * j + 2]
    ms = jnp.mean(x * x, axis=-1, keepdims=True)
    return (x * lax.rsqrt(ms + RMS_EPS)) * (gain * (1.0 + scale)) + shift


def _dot_nt(a, b):
    return lax.dot_general(a, b, (((1,), (1,)), ((), ())), preferred_element_type=F32)


def _next_tile(b, i, n_batch, n_tiles):
    t = jnp.minimum(b * n_tiles + i + 1, n_batch * n_tiles - 1)
    return t // n_tiles, t % n_tiles


def _ffn_kernel(*refs, j, final):
    if final:
        x_ref, xn_ref, mod_ref, modn_ref, gain_ref, wg_ref, wu_ref, wd_ref, fg_ref, o_ref = refs[:10]
    else:
        x_ref, xn_ref, mod_ref, modn_ref, gain_ref, wg_ref, wu_ref, wd_ref, o_ref = refs[:9]
        fg_ref = None
    bufs = (refs[-6:-3], refs[-3:])
    step = pl.program_id(0) * pl.num_programs(1) + pl.program_id(1)
    n_chunks = wg_ref.shape[1] // FF_CHUNK
    chunk = lambda ci: slice(ci * FF_CHUNK, (ci + 1) * FF_CHUNK)

    def head(x_tile_ref, mod_tile_ref, buf):
        h_ref, g_ref, u_ref = buf
        h = _prenorm(x_tile_ref[0], gain_ref[...], mod_tile_ref[0], j).astype(BF16)
        h_ref[...] = h
        g_ref[...] = jnp.dot(h, wg_ref[:, chunk(0)], preferred_element_type=F32)
        u_ref[...] = jnp.dot(h, wu_ref[:, chunk(0)], preferred_element_type=F32)

    @pl.when(step == 0)
    def _():
        head(x_ref, mod_ref, bufs[0])

    def body(cur, nxt):
        h_ref, g_ref, u_ref = cur
        acc = None
        for ci in range(n_chunks):
            if ci == 0:
                g, u = g_ref[...], u_ref[...]
            else:
                g = jnp.dot(h_ref[...], wg_ref[:, chunk(ci)], preferred_element_type=F32)
                u = jnp.dot(h_ref[...], wu_ref[:, chunk(ci)], preferred_element_type=F32)
            a = (g * jax.nn.sigmoid(g) * u).astype(BF16)
            down = jnp.dot(a, wd_ref[chunk(ci), :], preferred_element_type=F32)
            acc = down if acc is None else acc + down
        head(xn_ref, modn_ref, nxt)
        y = x_ref[0] + (0.5 * mod_ref[0][3 * j + 2:3 * j + 3]) * acc
        if final:
            y = _rms(y, fg_ref[...])
        o_ref[0] = y

    @pl.when(lax.rem(step, 2) == 0)
    def _():
        body(bufs[0], bufs[1])

    @pl.when(lax.rem(step, 2) == 1)
    def _():
        body(bufs[1], bufs[0])


def _ffn(x, mod, gain, wg, wu, wd, layer, which, j, final_gain=None):
    bn, s, d = x.shape
    nt = s // ROW_TILE
    final = final_gain is not None
    in_specs = [pl.BlockSpec((1, ROW_TILE, d), lambda b, i: (b, i, 0)),
                pl.BlockSpec((1, ROW_TILE, d), lambda b, i: (*_next_tile(b, i, bn, nt), 0)),
                pl.BlockSpec((1, ADA_CHUNKS, d), lambda b, i: (b, 0, 0)),
                pl.BlockSpec((1, ADA_CHUNKS, d), lambda b, i: (_next_tile(b, i, bn, nt)[0], 0, 0)),
                _const_spec((1, d)),
                _slab_spec(wg, layer, which), _slab_spec(wu, layer, which), _slab_spec(wd, layer, which)]
    args = [x, x, mod, mod, gain.reshape(1, d), wg, wu, wd]
    if final:
        in_specs.append(_const_spec((1, d)))
        args.append(final_gain.reshape(1, d))
    buf_set = [pltpu.VMEM((ROW_TILE, d), BF16), pltpu.VMEM((ROW_TILE, FF_CHUNK), F32),
               pltpu.VMEM((ROW_TILE, FF_CHUNK), F32)]
    return pl.pallas_call(
        functools.partial(_ffn_kernel, j=j, final=final),
        grid=(bn, nt),
        in_specs=in_specs,
        out_specs=pl.BlockSpec((1, ROW_TILE, d), lambda b, i: (b, i, 0)),
        out_shape=jax.ShapeDtypeStruct(x.shape, F32),
        scratch_shapes=buf_set + buf_set,
        compiler_params=pltpu.CompilerParams(dimension_semantics=("arbitrary", "arbitrary"),
                                             vmem_limit_bytes=VMEM_LIMIT),
        name="ffn_final" if final else f"ffn{j}",
    )(*args)


N_SB = 3 * H_SB * HEAD_DIM
N_DIL = 3 * H_DIL * HEAD_DIM
N_SWA = (H_SWA_Q + 2 * H_SWA_KV) * HEAD_DIM
N_QKV = N_SB + N_DIL + N_SWA
N_DIL_SLABS = N_DIL // LANES


N_INPROJ_HEAD = 7 * 256


def _inproj_kernel(x_ref, xn_ref, mod_ref, modn_ref, gain_ref, w_ref, sb_ref, dil_ref, swa_ref, hout_ref,
                   wtail_ref, h0_ref, t0_ref, h1_ref, t1_ref):
    qs = HEAD_DIM ** -0.5
    n_q_sb = H_SB * HEAD_DIM
    n_groups = len(DIL_PATTERNS)
    n_q_swa = H_SWA_Q * HEAD_DIM
    q0 = N_SB + N_DIL
    step = pl.program_id(0) * pl.num_programs(1) + pl.program_id(1)

    def head(x_tile_ref, mod_tile_ref, h_ref, t_ref):
        h = _prenorm(x_tile_ref[0], gain_ref[...], mod_tile_ref[0], 1).astype(BF16)
        h_ref[...] = h
        t_ref[...] = jnp.dot(h, wtail_ref[...], preferred_element_type=F32)

    @pl.when(step == 0)
    def _():
        n_dil_tail = q0 - N_INPROJ_HEAD
        wtail_ref[:, :n_dil_tail] = w_ref[:, N_INPROJ_HEAD:q0]
        for slot, hd in enumerate(_SWA_HEAD_ORDER):
            wtail_ref[:, n_dil_tail + slot * HEAD_DIM:n_dil_tail + (slot + 1) * HEAD_DIM] = \
                w_ref[:, q0 + hd * HEAD_DIM:q0 + (hd + 1) * HEAD_DIM]
        wtail_ref[:, n_dil_tail + n_q_swa:] = w_ref[:, q0 + n_q_swa:N_QKV]
        head(x_ref, mod_ref, h0_ref, t0_ref)

    def body(h_ref, t_ref, hn_ref, tn_ref):
        hout_ref[0] = h_ref[...]
        r = jnp.concatenate([jnp.dot(h_ref[...], w_ref[:, :N_INPROJ_HEAD], preferred_element_type=F32),
                             t_ref[...]], axis=1)
        head(xn_ref, modn_ref, hn_ref, tn_ref)
        sb_ref[0, :, :n_q_sb] = (r[:, :n_q_sb] * (qs * LOG2E)).astype(BF16)
        sb_ref[0, :, n_q_sb:] = r[:, n_q_sb:N_SB].astype(BF16)
        for a in range(3):
            for g in range(n_groups):
                c0 = N_SB + (a * n_groups + g) * LANES
                slab = r[:, c0:c0 + LANES]
                dil_ref[0, 3 * g + a] = slab * qs if a == 0 else slab
        swa_ref[0, :, :n_q_swa] = (r[:, q0:q0 + n_q_swa] * qs).astype(BF16)
        swa_ref[0, :, n_q_swa:] = r[:, q0 + n_q_swa:].astype(BF16)

    @pl.when(lax.rem(step, 2) == 0)
    def _():
        body(h0_ref, t0_ref, h1_ref, t1_ref)

    @pl.when(lax.rem(step, 2) == 1)
    def _():
        body(h1_ref, t1_ref, h0_ref, t0_ref)


def _inproj(x, mod, gain, w_in, layer):
    bn, s, d = x.shape
    nt = s // ROW_TILE
    n_tail = N_QKV - N_INPROJ_HEAD
    return pl.pallas_call(
        _inproj_kernel,
        grid=(bn, nt),
        in_specs=[pl.BlockSpec((1, ROW_TILE, d), lambda b, i: (b, i, 0)),
                  pl.BlockSpec((1, ROW_TILE, d), lambda b, i: (*_next_tile(b, i, bn, nt), 0)),
                  pl.BlockSpec((1, ADA_CHUNKS, d), lambda b, i: (b, 0, 0)),
                  pl.BlockSpec((1, ADA_CHUNKS, d), lambda b, i: (_next_tile(b, i, bn, nt)[0], 0, 0)),
                  _const_spec((1, d)), _slab_spec(w_in, layer)],
        out_specs=[pl.BlockSpec((1, ROW_TILE, N_SB), lambda b, i: (b, i, 0)),
                   pl.BlockSpec((1, N_DIL_SLABS, ROW_TILE, LANES), lambda b, i: (b, 0, i, 0)),
                   pl.BlockSpec((1, ROW_TILE, N_SWA), lambda b, i: (b, i, 0)),
                   pl.BlockSpec((1, ROW_TILE, d), lambda b, i: (b, i, 0))],
        out_shape=[jax.ShapeDtypeStruct((bn, s, N_SB), BF16),
                   jax.ShapeDtypeStruct((bn, N_DIL_SLABS, s, LANES), F32),
                   jax.ShapeDtypeStruct((bn, s, N_SWA), BF16),
                   jax.ShapeDtypeStruct((bn, s, d), BF16)],
        scratch_shapes=[pltpu.VMEM((d, n_tail), BF16)] + 2 * [pltpu.VMEM((ROW_TILE, d), BF16),
                                                             pltpu.VMEM((ROW_TILE, n_tail), F32)],
        compiler_params=pltpu.CompilerParams(dimension_semantics=("arbitrary", "arbitrary"),
                                             vmem_limit_bytes=VMEM_LIMIT),
        name="in_proj",
    )(x, x, mod, mod, gain.reshape(1, d), w_in)


def _sb_kernel(q_ref, k_ref, v_ref, o_ref, acc_ref, c_ref):
    s_len = q_ref.shape[1]
    n_pairs = q_ref.shape[2] // LANES
    per = SB_ROWS // BLK
    lane = lax.broadcasted_iota(jnp.int32, (BLK, LANES), 1)
    head0 = lane < HEAD_DIM
    row2 =lax.broadcasted_iota(jnp.int32, (BLK, 2 * BLK), 0)
    col2 = lax.broadcasted_iota(jnp.int32, (BLK, 2 * BLK), 1)
    strict2 = (col2 & (BLK - 1)) < row2
    rk = lax.broadcasted_iota(jnp.int32, (2 * BLK, 2 * BLK), 0)
    cn = lax.broadcasted_iota(jnp.int32, (2 * BLK, 2 * BLK), 1)
    neg_suffix_ones = jnp.where(((rk & (BLK - 1)) >= cn) | (cn >= BLK), -1.0, 0.0).astype(BF16)

    def by_head(t):
        zero = jnp.zeros_like(t)
        return jnp.concatenate([jnp.where(head0, t, zero), jnp.where(head0, zero, t)], axis=0)

    def mask_first_block(t):
        first = jnp.where(strict2, t[:BLK], 0.0)
        return first if t.shape[0] == BLK else jnp.concatenate([first, t[BLK:]], axis=0)

    def pair_lanes(p):
        return slice(p * LANES, (p + 1) * LANES)

    def scores(p, q, k0):
        return _dot_nt(q, by_head(k_ref[0, pl.ds(k0, BLK), pair_lanes(p)]))

    def carries(z, diag):
        neg_abs = lax.bitcast_convert_type(lax.bitcast_convert_type(z, jnp.int32) | jnp.int32(-2 ** 31), F32)
        softplus2 = jnp.maximum(z, 0.0) + jnp.log2(1.0 + jnp.exp2(neg_abs))
        if diag:
            softplus2 = mask_first_block(softplus2)
        hi = softplus2.astype(BF16)
        lo = (softplus2 - hi.astype(F32)).astype(BF16)
        return [jnp.dot(jnp.concatenate([hi[:, h * BLK:(h + 1) * BLK], lo[:, h * BLK:(h + 1) * BLK]], axis=1),
                        neg_suffix_ones, preferred_element_type=F32) for h in range(2)]

    def weighted_values(p, z, cts, c, k0, diag):
        ws = [jnp.exp2(z[:, h * BLK:(h + 1) * BLK] + cts[h][:, :BLK] + c[h]) for h in range(2)]
        w = jnp.concatenate(ws, axis=1)
        if diag:
            w = mask_first_block(w)
        vv = by_head(v_ref[0, pl.ds(k0, BLK), pair_lanes(p)])
        return jnp.dot(w.astype(BF16), vv, preferred_element_type=F32)

    def superblock(sb, carry):
        base = pl.multiple_of(sb * SB_ROWS, SB_ROWS)
        acc_ref[...] = jnp.zeros_like(acc_ref)
        c_ref[...] = jnp.zeros_like(c_ref)

        def run(units):
            zs = [scores(p, q_ref[0, pl.ds(base + lo, n), pair_lanes(p)], k0) for p, lo, n, k0, _ in units]
            cts = [carries(z, unit[4]) for z, unit in zip(zs, units)]
            for z, ct, (p, lo, n, k0, diag) in zip(zs, cts, units):
                rows = slice(lo, lo + n)
                c = (c_ref[p, 0, rows, :], c_ref[p, 1, rows, :])
                acc_ref[p, rows, :] += weighted_values(p, z, ct, c, k0, diag)
                c_ref[p, 0, rows, :] = c[0] + ct[0][:, BLK:]
                c_ref[p, 1, rows, :] = c[1] + ct[1][:, BLK:]

        def alive(lo, n):
            c_max = None
            for p in range(n_pairs):
                for h in range(2):
                    c = c_ref[p, h, lo:lo + n, :]
                    c_max = c if c_max is None else jnp.maximum(c_max, c)
            return jnp.max(c_max) > SB_DEAD_LOG2

        run([(p, kq * BLK, SB_ROWS - kq * BLK, base + kq * BLK, True)
             for kq in reversed(range(per)) for p in range(n_pairs)])

        n_iter = sb * (per // SB_KEY_BLOCKS)

        def sweep(window, watch, jj0):
            def kstep(state):
                jj, _ = state
                run([(p, window[0], window[1],
                      pl.multiple_of(base - (SB_KEY_BLOCKS * jj + u + 1) * BLK, BLK), False)
                     for u in range(SB_KEY_BLOCKS) for p in range(n_pairs)])
                return jj + 1, alive(*watch)

            jj, _ = lax.while_loop(lambda st: jnp.logical_and(st[0] < n_iter, st[1]), kstep,
                                   (jj0, alive(*watch)))
            return jj

        half = SB_ROWS // 2
        jj = sweep((0, SB_ROWS), (half, half), jnp.int32(0))
        sweep((0, half), (0, half), jj)
        for p in range(n_pairs):
            o_ref[0, pl.ds(base, SB_ROWS), pair_lanes(p)] = acc_ref[p].astype(o_ref.dtype)
        return carry

    lax.fori_loop(0, s_len // SB_ROWS, superblock, 0)


def _sb_attention(qkv_sb):
    bn, s, _ = qkv_sb.shape
    n_pairs = H_SB // 2
    width = n_pairs * LANES
    spec = lambda t: pl.BlockSpec((1, s, width), lambda b: (b, 0, t))
    return pl.pallas_call(
        _sb_kernel,
        grid=(bn,),
        in_specs=[spec(0), spec(1), spec(2)],
        out_specs=pl.BlockSpec((1, s, width), lambda b: (b, 0, 0)),
        out_shape=jax.ShapeDtypeStruct((bn, s, width), BF16),
        scratch_shapes=[pltpu.VMEM((n_pairs, SB_ROWS, LANES), F32),
                        pltpu.VMEM((n_pairs, 2, SB_ROWS, LANES), F32)],
        compiler_params=_cparams(1),
        name="sb_attn",
    )(qkv_sb, qkv_sb, qkv_sb)


def _band_bias_tiles(bucket, table_ref, col, max_dist, sink=None):
    row = lax.broadcasted_iota(jnp.int32, (BLK, 2 * BLK), 0)
    kcol = lax.broadcasted_iota(jnp.int32, (BLK, 2 * BLK), 1)
    rel = row + BLK - kcol
    in_band = (rel >= 0) & (rel <= max_dist)
    bias = jnp.zeros((BLK, 2 * BLK), F32)
    for n in range(N_BUCKETS):
        bias = jnp.where(bucket == n, table_ref[n * N_SOFT_HEADS + col], bias)
    tiles = [jnp.where(in_band, bias, NEG_BIG), jnp.where(in_band & (kcol >= BLK), bias, NEG_BIG)]
    if sink is not None:
        tiles = [jnp.where(kcol == 0, sink, t) for t in tiles]
    return tiles


def _band_pair(q, kcat, vcat, biases):
    return _band_finish(*_band_pv(_band_scores(q, kcat, biases), vcat))


def _band_scores(q, kcat, biases):
    head0 = lax.broadcasted_iota(jnp.int32, q.shape, 1) < HEAD_DIM
    zero = jnp.zeros_like(q)
    return [_dot_nt(jnp.where(head0, q, zero) if h == 0 else jnp.where(head0, zero, q), kcat) + biases[h]
            for h in range(2)]


def _band_pv(scores, vcat):
    one = jnp.ones_like(vcat)
    v_head0 = lax.broadcasted_iota(jnp.int32, vcat.shape, 1) < HEAD_DIM
    outs, ms = [], []
    for h, s in enumerate(scores):
        vh = jnp.where(v_head0, vcat, one) if h == 0 else jnp.where(v_head0, one, vcat)
        m = jnp.max(s, axis=-1, keepdims=True)
        p = jnp.exp(s - m)
        outs.append(jnp.dot(p.astype(BF16), vh, preferred_element_type=F32))
        ms.append(m)
    return outs, ms


def _band_finish(outs, ms):
    head0 = lax.broadcasted_iota(jnp.int32, outs[0].shape, 1) < HEAD_DIM
    num = jnp.where(head0, outs[0], outs[1])
    den = pltpu.roll(jnp.where(head0, outs[1], outs[0]), HEAD_DIM, axis=1)
    return num, den, jnp.where(head0, ms[0], ms[1])


def _dil_kernel(table_ref, qkv_ref, bucket_ref, o_ref, bias_ref, og_ref, dg_ref, mg_ref, stage_ref, ostage_ref):
    s_len = o_ref.shape[1]
    n_blocks = s_len // BLK

    @pl.when(pl.program_id(0) == 0)
    def _():
        for g, (window, d) in enumerate(DIL_PATTERNS):
            for h in range(H_PER_DIL):
                head = g * H_PER_DIL + h
                normal, first = _band_bias_tiles(bucket_ref[g], table_ref, head, window // d)
                bias_ref[head, 0] = normal
                bias_ref[head, 1] = first

    for g, (window, d) in enumerate(DIL_PATTERNS):
        nb = s_len // d // BLK
        two_pass = d > MAX_STRIDE
        d_in = d // MAX_STRIDE if two_pass else d
        seg = s_len // MAX_STRIDE

        if two_pass:
            def stage_in(cc, carry):
                c1, j = cc // (seg // BLK), cc % (seg // BLK)
                for a in range(3):
                    stage_ref[a, pl.ds(pl.multiple_of(c1 * seg + j * BLK, BLK), BLK), :] = \
                        qkv_ref[0, 3 * g + a, pl.ds(c1 + MAX_STRIDE * BLK * j, BLK, stride=MAX_STRIDE), :]
                return carry

            lax.fori_loop(0, n_blocks, stage_in, 0)

        def src(a, start):
            if d == 1:
                return qkv_ref[0, 3 * g + a, pl.ds(start, BLK), :]
            if not two_pass:
                return qkv_ref[0, 3 * g + a, pl.ds(start, BLK, stride=d), :]
            return stage_ref[a, pl.ds(staged(start), BLK, stride=d_in), :]

        def staged(start):
            return (start % MAX_STRIDE) * seg + start // MAX_STRIDE

        def put(dst_ref, ostage_slot, start, val):
            if d == 1:
                dst_ref[g, pl.ds(start, BLK), :] = val
            elif not two_pass:
                dst_ref[g, pl.ds(start, BLK, stride=d), :] = val
            else:
                ostage_ref[ostage_slot, pl.ds(staged(start), BLK, stride=d_in), :] = val

        def scores(idx):
            r, i = idx // nb, idx % nb
            start = r + d * BLK * i
            q = src(0, start).astype(BF16)
            heads = (g * H_PER_DIL, g * H_PER_DIL + 1)
            prev = r + d * BLK * jnp.maximum(i - 1, 0)
            kcat = jnp.concatenate([src(1, prev), src(1, start)], axis=0).astype(BF16)
            vcat = jnp.concatenate([src(2, prev), src(2, start)], axis=0).astype(BF16)
            variant = jnp.where(i > 0, 0, 1)
            biases = [bias_ref[hd, variant] for hd in heads]
            return start, _band_scores(q, kcat, biases), vcat

        def body(it, carry):
            staged1 = [scores(it * BAND_UNROLL + u) for u in range(BAND_UNROLL)]
            staged2 = [(start, _band_pv(s, vcat)) for start, s, vcat in staged1]
            for start, (outs, ms) in staged2:
                num, den, m = _band_finish(outs, ms)
                put(og_ref, 0, start, num)
                put(dg_ref, 1, start, den)
                put(mg_ref, 2, start, m)
            return carry

        lax.fori_loop(0, n_blocks // BAND_UNROLL, body, 0)

        if two_pass:
            def stage_out(cc, carry):
                c1, j = cc // (seg // BLK), cc % (seg // BLK)
                rows_in = pl.ds(pl.multiple_of(c1 * seg + j * BLK, BLK), BLK)
                rows_out = pl.ds(c1 + MAX_STRIDE * BLK * j, BLK, stride=MAX_STRIDE)
                og_ref[g, rows_out, :] = ostage_ref[0, rows_in, :]
                dg_ref[g, rows_out, :] = ostage_ref[1, rows_in, :]
                mg_ref[g, rows_out, :] = ostage_ref[2, rows_in, :]
                return carry

            lax.fori_loop(0, n_blocks, stage_out, 0)

    def merge(i, carry):
        sl = pl.ds(pl.multiple_of(i * BLK, BLK), BLK)
        m0, m1, m2 = mg_ref[0, sl, :], mg_ref[1, sl, :], mg_ref[2, sl, :]
        m = jnp.maximum(jnp.maximum(m0, m1), m2)
        e0, e1, e2 = jnp.exp(m0 - m), jnp.exp(m1 - m), jnp.exp(m2 - m)
        num = e0 * og_ref[0, sl, :] + e1 * og_ref[1, sl, :] + e2 * og_ref[2, sl, :]
        den = e0 * dg_ref[0, sl, :] + e1 * dg_ref[1, sl, :] + e2 * dg_ref[2, sl, :]
        o_ref[0, sl, :] = (num / den).astype(o_ref.dtype)
        return carry

    lax.fori_loop(0, n_blocks, merge, 0)


def _dil_attention(qkv_dil, buckets, table):
    bn, n_slabs, s, _ = qkv_dil.shape
    n_groups = len(DIL_PATTERNS)
    return pl.pallas_call(
        _dil_kernel,
        grid_spec=pltpu.PrefetchScalarGridSpec(
            num_scalar_prefetch=1,
            grid=(bn,),
            in_specs=[pl.BlockSpec((1, n_slabs, s, LANES), lambda b, tb: (b, 0, 0, 0)),
                      pl.BlockSpec(buckets.shape, lambda b, tb: (0, 0, 0))],
            out_specs=pl.BlockSpec((1, s, LANES), lambda b, tb: (b, 0, 0)),
            scratch_shapes=[pltpu.VMEM((H_DIL, 2, BLK, 2 * BLK), F32),
                            pltpu.VMEM((n_groups, s, LANES), F32),
                            pltpu.VMEM((n_groups, s, LANES), F32),
                            pltpu.VMEM((n_groups, s, LANES), F32),
                            pltpu.VMEM((3, s, LANES), F32),
                            pltpu.VMEM((3, s, LANES), F32)],
        ),
        out_shape=jax.ShapeDtypeStruct((bn, s, LANES), BF16),
        compiler_params=pltpu.CompilerParams(dimension_semantics=("arbitrary",),
                                             vmem_limit_bytes=VMEM_LIMIT),
        name="dil_attn",
    )(table, qkv_dil, buckets)


def _swa_kernel(table_ref, sink_ref, qkv_ref, bucket_ref, o_ref, bias_ref):
    s_len = o_ref.shape[1]
    k_col = N_SWA_PAIRS * LANES
    v_col = k_col + LANES
    key_row = lax.broadcasted_iota(jnp.int32, (2 * BLK, LANES), 0)

    @pl.when(pl.program_id(0) == 0)
    def _():
        for slot, hd in enumerate(_SWA_HEAD_ORDER):
            normal, first = _band_bias_tiles(bucket_ref[0], table_ref, H_DIL + hd, SWA_WINDOW - 1, sink_ref[hd])
            bias_ref[slot, 0] = normal
            bias_ref[slot, 1] = first

    def scores(i):
        start = pl.multiple_of(i * BLK, BLK)
        prev = pl.multiple_of(jnp.maximum(i - 1, 0) * BLK, BLK)
        kcat = jnp.concatenate([qkv_ref[0, pl.ds(prev, BLK), k_col:k_col + LANES],
                                qkv_ref[0, pl.ds(start, BLK), k_col:k_col + LANES]], axis=0)
        vcat = jnp.concatenate([qkv_ref[0, pl.ds(prev, BLK), v_col:v_col + LANES],
                                qkv_ref[0, pl.ds(start, BLK), v_col:v_col + LANES]], axis=0)
        kcat = jnp.where(key_row == 0, jnp.zeros_like(kcat), kcat)
        vcat = jnp.where(key_row == 0, jnp.zeros_like(vcat), vcat)
        variant = jnp.where(i > 0, 0, 1)
        out = []
        for p in range(N_SWA_PAIRS):
            q = qkv_ref[0, pl.ds(start, BLK), p * LANES:(p + 1) * LANES]
            biases = [bias_ref[2 * p + h, variant] for h in range(2)]
            out.append((start, p, _band_scores(q, kcat, biases), vcat))
        return out

    def body(it, carry):
        staged1 = [unit for u in range(SWA_UNROLL) for unit in scores(it * SWA_UNROLL + u)]
        staged2 = [(start, p, _band_pv(s, vcat)) for start, p, s, vcat in staged1]
        for start, p, (outs, ms) in staged2:
            num, den, _ = _band_finish(outs, ms)
            o_ref[0, pl.ds(start, BLK), p * LANES:(p + 1) * LANES] = (num * (1.0 / den)).astype(o_ref.dtype)
        return carry

    lax.fori_loop(0, s_len // BLK // SWA_UNROLL, body, 0)


def _swa_attention(qkv_swa, bucket, table, sinks_l):
    bn, s, n = qkv_swa.shape
    return pl.pallas_call(
        _swa_kernel,
        grid_spec=pltpu.PrefetchScalarGridSpec(
            num_scalar_prefetch=2,
            grid=(bn,),
            in_specs=[pl.BlockSpec((1, s, n), lambda b, tb, sk: (b, 0, 0)),
                      pl.BlockSpec(bucket.shape, lambda b, tb, sk: (0, 0, 0))],
            out_specs=pl.BlockSpec((1, s, N_SWA_PAIRS * LANES), lambda b, tb, sk: (b, 0, 0)),
            scratch_shapes=[pltpu.VMEM((H_SWA_Q, 2, BLK, 2 * BLK), F32)],
        ),
        out_shape=jax.ShapeDtypeStruct((bn, s, N_SWA_PAIRS * LANES), BF16),
        compiler_params=pltpu.CompilerParams(dimension_semantics=("arbitrary",),
                                             vmem_limit_bytes=VMEM_LIMIT),
        name="swa_attn",
    )(table, sinks_l, qkv_swa, bucket)


def _merge_kernel(x_ref, h_ref, mod_ref, osb_ref, odil_ref, oswa_ref,
                  win_ref, wsb_ref, wdil_ref, wswa_ref, wout_ref, o_ref):
    x = x_ref[0]
    mod = mod_ref[0]
    d = x.shape[1]
    gate0 = win_ref.shape[1] - 3 * d
    h = h_ref[0]
    merged = None
    for t, (o_br, w_br) in enumerate(((osb_ref, wsb_ref), (odil_ref, wdil_ref), (oswa_ref, wswa_ref))):
        w_gate = win_ref[:, gate0 + t * d:gate0 + (t + 1) * d]
        gate = jax.nn.sigmoid(jnp.dot(h, w_gate, preferred_element_type=F32))
        term = gate * jnp.dot(o_br[0], w_br[...], preferred_element_type=F32)
        merged = term if merged is None else merged + term
    y = jnp.dot(merged.astype(BF16), wout_ref[...], preferred_element_type=F32)
    o_ref[0] = x + mod[5:6] * y


def _merge(x, h, mod, o_sb, o_dil, o_swa, w_in, w_sb, w_dil, w_swa, w_out, layer):
    bn, s, d = x.shape
    row = lambda n: pl.BlockSpec((1, ROW_TILE, n), lambda b, i: (b, i, 0))
    return pl.pallas_call(
        _merge_kernel,
        grid=(bn, s // ROW_TILE),
        in_specs=[row(d), row(d),
                  pl.BlockSpec((1, ADA_CHUNKS, d), lambda b, i: (b, 0, 0)),
                  row(o_sb.shape[2]), row(o_dil.shape[2]), row(o_swa.shape[2]),
                  _slab_spec(w_in, layer), _slab_spec(w_sb, layer), _slab_spec(w_dil, layer),
                  _const_spec(w_swa.shape), _slab_spec(w_out, layer)],
        out_specs=row(d),
        out_shape=jax.ShapeDtypeStruct(x.shape, F32),
        compiler_params=_cparams(2),
        name="merge",
    )(x, h, mod, o_sb, o_dil, o_swa, w_in, w_sb, w_dil, w_swa, w_out)


def _t5_bucket(n):
    max_exact = N_BUCKETS // 2
    nf = jnp.maximum(n, 1).astype(jnp.float32)
    large = max_exact + (jnp.log(nf / max_exact) / math.log(MAX_REL_DIST / max_exact)
                         * (N_BUCKETS - max_exact)).astype(jnp.int32)
    large = jnp.minimum(large, N_BUCKETS - 1)
    return jnp.where(n < max_exact, n, large)


def _band_buckets(dilation):
    rel = jnp.arange(BLK)[:, None] + BLK - jnp.arange(2 * BLK)[None, :]
    return _t5_bucket(jnp.maximum(rel, 0) * dilation).astype(jnp.int32)


def _reorder_heads(w, order, axis):
    parts = [lax.slice_in_dim(w, hd * HEAD_DIM, (hd + 1) * HEAD_DIM, axis=axis) for hd in order]
    return jnp.concatenate(parts, axis=axis)


def kernel(x, c, w_ada, b_ada, norm_gain, w_ffn_gate, w_ffn_up, w_ffn_down, w_in,
           w_br_sb, w_br_dil, w_br_swa, w_out, sinks, rel_bias, final_gain):
    depth = w_ada.shape[0]
    bn, s, d = x.shape
    mods = _ada(c, w_ada, b_ada).reshape(depth, bn, ADA_CHUNKS, d)

    buckets = jnp.stack([_band_buckets(dil) for _, dil in DIL_PATTERNS])
    table = rel_bias.reshape(-1)
    wg, wu, wd = w_ffn_gate.astype(BF16), w_ffn_up.astype(BF16), w_ffn_down.astype(BF16)
    w_in_bf, w_sb_bf, w_dil_bf, w_out_bf = (w_in.astype(BF16), w_br_sb.astype(BF16),
                                            w_br_dil.astype(BF16), w_out.astype(BF16))

    for l in range(depth):
        mod = mods[l]
        x = _ffn(x, mod, norm_gain[l, 0], wg, wu, wd, l, 0, 0)

        qkv_sb, qkv_dil, qkv_swa, h_mix = _inproj(x, mod, norm_gain[l, 1], w_in_bf, l)
        o_sb = _sb_attention(qkv_sb)
        o_dil = _dil_attention(qkv_dil, buckets, table)
        o_swa = _swa_attention(qkv_swa, buckets[:1], table, sinks[l])
        w_swa = _reorder_heads(w_br_swa[l], _SWA_HEAD_ORDER, 0).astype(BF16)
        x = _merge(x, h_mix, mod, o_sb, o_dil, o_swa, w_in_bf, w_sb_bf, w_dil_bf, w_swa, w_out_bf, l)

        last = l == depth - 1
        x = _ffn(x, mod, norm_gain[l, 2], wg, wu, wd, l, 1, 2, final_gain=final_gain if last else None)
    return x
```

```python
import functools
import math

import jax
import jax.numpy as jnp
from jax import lax
from jax.experimental import pallas as pl
from jax.experimental.pallas import tpu as pltpu

F32 = jnp.float32
BF16 = jnp.bfloat16

HEAD_DIM = 64
LANES = 128
BLK = 128
H_SB = 4
DIL_PATTERNS = ((128, 1), (512, 4), (2048, 16))
H_PER_DIL = 2
H_DIL = H_PER_DIL * len(DIL_PATTERNS)
H_SWA_Q = 6
H_SWA_KV = 2
SWA_WINDOW = 128
N_SOFT_HEADS = H_DIL + H_SWA_Q
N_SWA_PAIRS = H_SWA_Q // 2
N_BUCKETS = 32
MAX_REL_DIST = 2048
RMS_EPS = 1e-6
ADA_CHUNKS = 9
NEG_BIG = -1e30
LOG2E = 1.4426950408889634

VMEM_LIMIT = 56 * 1024 * 1024
ROW_TILE = 512
FF_CHUNK = 256
W_STAGE_ROWS_UP = 128
W_STAGE_ROWS_DOWN = 256
ADA_TILE = 3072
SB_ROWS = 512
SB_KEY_BLOCKS = 2
SB_DEAD_LOG2 = -150.0
BAND_UNROLL = 16
MAX_STRIDE = 4
SWA_UNROLL = 8

_SWA_HEAD_ORDER = tuple(p + (H_SWA_Q // H_SWA_KV) * h for p in range(N_SWA_PAIRS) for h in range(2))


def _cparams(n_axes):
    return pltpu.CompilerParams(dimension_semantics=("parallel",) * n_axes,
                                vmem_limit_bytes=VMEM_LIMIT)


def _const_spec(shape):
    nd = len(shape)
    return pl.BlockSpec(shape, lambda *_: (0,) * nd, pipeline_mode=pl.Buffered(1))


def _slab_spec(arr, *lead):
    block = (None,) * len(lead) + tuple(arr.shape[len(lead):])
    return pl.BlockSpec(block, lambda *_: tuple(lead) + (0, 0), pipeline_mode=pl.Buffered(1))


def _ada_kernel(c_ref, w_ref, b_ref, o_ref):
    c = c_ref[...]
    sc = (c * jax.nn.sigmoid(c)).astype(BF16)
    o_ref[0] = jnp.dot(sc, w_ref[0].astype(BF16), preferred_element_type=F32) + b_ref[0]


def _ada(c, w_ada, b_ada):
    depth, d, n = w_ada.shape
    bn = c.shape[0]
    return pl.pallas_call(
        _ada_kernel,
        grid=(depth, n // ADA_TILE),
        in_specs=[pl.BlockSpec((bn, d), lambda l, j: (0, 0)),
                  pl.BlockSpec((1, d, ADA_TILE), lambda l, j: (l, 0, j)),
                  pl.BlockSpec((1, 1, ADA_TILE), lambda l, j: (l, 0, j))],
        out_specs=pl.BlockSpec((1, bn, ADA_TILE), lambda l, j: (l, 0, j)),
        out_shape=jax.ShapeDtypeStruct((depth, bn, n), F32),
        compiler_params=_cparams(2),
        name="ada_mod",
    )(c, w_ada, b_ada.reshape(depth, 1, n))


def _rms(x, gain):
    ms = jnp.mean(x * x, axis=-1, keepdims=True)
    return x * lax.rsqrt(ms + RMS_EPS) * gain


def _prenorm(x, gain, mod, j):
    shift = mod[3 * j:3 * j + 1]
    scale = mod[3 * j + 1:3 * j + 2]
    ms = jnp.mean(x * x, axis=-1, keepdims=True)
    return (x * lax.rsqrt(ms + RMS_EPS)) * (gain * (1.0 + scale)) + shift


def _dot_nt(a, b):
    return lax.dot_general(a, b, (((1,), (1,)), ((), ())), preferred_element_type=F32)


def _stream_cast(src_hbm, dst_ref, stage_ref, sem_ref):
    chunk_rows = stage_ref.shape[1]
    n_chunks = src_hbm.shape[0] // chunk_rows
    copies = [pltpu.make_async_copy(src_hbm.at[pl.ds(c * chunk_rows, chunk_rows)],
                                    stage_ref.at[c % 2], sem_ref.at[c % 2]) for c in range(n_chunks)]
    copies[0].start()
    for c in range(n_chunks):
        if c + 1 < n_chunks:
            copies[c + 1].start()
        copies[c].wait()
        dst_ref[pl.ds(c * chunk_rows, chunk_rows), :] = stage_ref[c % 2].astype(BF16)


def _ffn_kernel(*refs, j, final, layer, which):
    n_in = 7 if final else 6
    x_ref, mod_ref, gain_ref, wg_hbm, wu_hbm, wd_hbm = refs[:6]
    fg_ref = refs[6] if final else None
    o_ref, wg_ref, wu_ref, wd_ref, stage_up_ref, stage_down_ref, sem_ref = refs[n_in:]

    @pl.when((pl.program_id(0) == 0) & (pl.program_id(1) == 0))
    def _():
        _stream_cast(wg_hbm.at[layer, which], wg_ref, stage_up_ref, sem_ref)
        _stream_cast(wu_hbm.at[layer, which], wu_ref, stage_up_ref, sem_ref)
        _stream_cast(wd_hbm.at[layer, which], wd_ref, stage_down_ref, sem_ref)

    x = x_ref[0]
    mod = mod_ref[0]
    h = _prenorm(x, gain_ref[...], mod, j).astype(BF16)
    d_ff = wg_ref.shape[1]
    acc = jnp.zeros(x.shape, F32)
    for ci in range(d_ff // FF_CHUNK):
        cols = slice(ci * FF_CHUNK, (ci + 1) * FF_CHUNK)
        g = jnp.dot(h, wg_ref[:, cols], preferred_element_type=F32)
        u = jnp.dot(h, wu_ref[:, cols], preferred_element_type=F32)
        a = (g * jax.nn.sigmoid(g) * u).astype(BF16)
        acc = acc + jnp.dot(a, wd_ref[cols, :], preferred_element_type=F32)
    y = x + (0.5 * mod[3 * j + 2:3 * j + 3]) * acc
    if final:
        y = _rms(y, fg_ref[...])
    o_ref[0] = y


def _ffn(x, mod, gain, wg, wu, wd, layer, which, j, final_gain=None):
    bn, s, d = x.shape
    d_ff = wg.shape[-1]
    final = final_gain is not None
    hbm = pl.BlockSpec(memory_space=pl.ANY)
    in_specs = [pl.BlockSpec((1, ROW_TILE, d), lambda b, i: (b, i, 0)),
                pl.BlockSpec((1, ADA_CHUNKS, d), lambda b, i: (b, 0, 0)),
                _const_spec((1, d)), hbm, hbm, hbm]
    args = [x, mod, gain.reshape(1, d), wg, wu, wd]
    if final:
        in_specs.append(_const_spec((1, d)))
        args.append(final_gain.reshape(1, d))
    return pl.pallas_call(
        functools.partial(_ffn_kernel, j=j, final=final, layer=layer, which=which),
        grid=(bn, s // ROW_TILE),
        in_specs=in_specs,
        out_specs=pl.BlockSpec((1, ROW_TILE, d), lambda b, i: (b, i, 0)),
        out_shape=jax.ShapeDtypeStruct(x.shape, F32),
        scratch_shapes=[pltpu.VMEM((d, d_ff), BF16), pltpu.VMEM((d, d_ff), BF16), pltpu.VMEM((d_ff, d), BF16),
                        pltpu.VMEM((2, W_STAGE_ROWS_UP, d_ff), F32), pltpu.VMEM((2, W_STAGE_ROWS_DOWN, d), F32),
                        pltpu.SemaphoreType.DMA((2,))],
        compiler_params=pltpu.CompilerParams(dimension_semantics=("arbitrary", "arbitrary"),
                                             vmem_limit_bytes=VMEM_LIMIT),
        name="ffn_final" if final else f"ffn{j}",
    )(*args)


N_SB = 3 * H_SB * HEAD_DIM
N_DIL = 3 * H_DIL * HEAD_DIM
N_SWA = (H_SWA_Q + 2 * H_SWA_KV) * HEAD_DIM
N_QKV = N_SB + N_DIL + N_SWA
N_DIL_SLABS = N_DIL // LANES


N_INPROJ_HEAD = 7 * 256


def _inproj_kernel(x_ref, mod_ref, gain_ref, w_ref, sb_ref, dil_ref, swa_ref, hout_ref, wtail_ref):
    qs = HEAD_DIM ** -0.5
    n_q_sb = H_SB * HEAD_DIM
    n_groups = len(DIL_PATTERNS)
    n_q_swa = H_SWA_Q * HEAD_DIM
    q0 = N_SB + N_DIL

    @pl.when((pl.program_id(0) == 0) & (pl.program_id(1) == 0))
    def _():
        n_dil_tail = q0 - N_INPROJ_HEAD
        wtail_ref[:, :n_dil_tail] = w_ref[:, N_INPROJ_HEAD:q0]
        for slot, hd in enumerate(_SWA_HEAD_ORDER):
            wtail_ref[:, n_dil_tail + slot * HEAD_DIM:n_dil_tail + (slot + 1) * HEAD_DIM] = \
                w_ref[:, q0 + hd * HEAD_DIM:q0 + (hd + 1) * HEAD_DIM]
        wtail_ref[:, n_dil_tail + n_q_swa:] = w_ref[:, q0 + n_q_swa:N_QKV]

    h = _prenorm(x_ref[0], gain_ref[...], mod_ref[0], 1).astype(BF16)
    hout_ref[0] = h
    r = jnp.concatenate([jnp.dot(h, w_ref[:, :N_INPROJ_HEAD], preferred_element_type=F32),
                         jnp.dot(h, wtail_ref[...], preferred_element_type=F32)], axis=1)
    sb_ref[0, :, :n_q_sb] = (r[:, :n_q_sb] * (qs * LOG2E)).astype(BF16)
    sb_ref[0, :, n_q_sb:] = r[:, n_q_sb:N_SB].astype(BF16)
    for a in range(3):
        for g in range(n_groups):
            c0 = N_SB + (a * n_groups + g) * LANES
            slab = r[:, c0:c0 + LANES]
            dil_ref[0, 3 * g + a] = slab * qs if a == 0 else slab
    swa_ref[0, :, :n_q_swa] = (r[:, q0:q0 + n_q_swa] * qs).astype(BF16)
    swa_ref[0, :, n_q_swa:] = r[:, q0 + n_q_swa:].astype(BF16)


def _inproj(x, mod, gain, w_in, layer):
    bn, s, d = x.shape
    nt = s // ROW_TILE
    n_tail = N_QKV - N_INPROJ_HEAD
    return pl.pallas_call(
        _inproj_kernel,
        grid=(bn, nt),
        in_specs=[pl.BlockSpec((1, ROW_TILE, d), lambda b, i: (b, i, 0)),
                  pl.BlockSpec((1, ADA_CHUNKS, d), lambda b, i: (b, 0, 0)),
                  _const_spec((1, d)), _slab_spec(w_in, layer)],
        out_specs=[pl.BlockSpec((1, ROW_TILE, N_SB), lambda b, i: (b, i, 0)),
                   pl.BlockSpec((1, N_DIL_SLABS, ROW_TILE, LANES), lambda b, i: (b, 0, i, 0)),
                   pl.BlockSpec((1, ROW_TILE, N_SWA), lambda b, i: (b, i, 0)),
                   pl.BlockSpec((1, ROW_TILE, d), lambda b, i: (b, i, 0))],
        out_shape=[jax.ShapeDtypeStruct((bn, s, N_SB), BF16),
                   jax.ShapeDtypeStruct((bn, N_DIL_SLABS, s, LANES), F32),
                   jax.ShapeDtypeStruct((bn, s, N_SWA), BF16),
                   jax.ShapeDtypeStruct((bn, s, d), BF16)],
        scratch_shapes=[pltpu.VMEM((d, n_tail), BF16)],
        compiler_params=pltpu.CompilerParams(dimension_semantics=("arbitrary", "arbitrary"),
                                             vmem_limit_bytes=VMEM_LIMIT),
        name="in_proj",
    )(x, mod, gain.reshape(1, d), w_in)


def _sb_kernel(q_ref, k_ref, v_ref, o_ref, acc_ref, c_ref):
    s_len = q_ref.shape[1]
    n_pairs = q_ref.shape[2] // LANES
    per = SB_ROWS // BLK
    lane = lax.broadcasted_iota(jnp.int32, (BLK, LANES), 1)
    head0 = lane < HEAD_DIM
    row2 =lax.broadcasted_iota(jnp.int32, (BLK, 2 * BLK), 0)
    col2 = lax.broadcasted_iota(jnp.int32, (BLK, 2 * BLK), 1)
    strict2 = (col2 & (BLK - 1)) < row2
    rk = lax.broadcasted_iota(jnp.int32, (2 * BLK, 2 * BLK), 0)
    cn = lax.broadcasted_iota(jnp.int32, (2 * BLK, 2 * BLK), 1)
    neg_suffix_ones = jnp.where(((rk & (BLK - 1)) >= cn) | (cn >= BLK), -1.0, 0.0).astype(BF16)

    def by_head(t):
        zero = jnp.zeros_like(t)
        return jnp.concatenate([jnp.where(head0, t, zero), jnp.where(head0, zero, t)], axis=0)

    def mask_first_block(t):
        first = jnp.where(strict2, t[:BLK], 0.0)
        return first if t.shape[0] == BLK else jnp.concatenate([first, t[BLK:]], axis=0)

    def pair_lanes(p):
        return slice(p * LANES, (p + 1) * LANES)

    def scores(p, q, k0):
        return _dot_nt(q, by_head(k_ref[0, pl.ds(k0, BLK), pair_lanes(p)]))

    def carries(z, diag):
        neg_abs = lax.bitcast_convert_type(lax.bitcast_convert_type(z, jnp.int32) | jnp.int32(-2 ** 31), F32)
        softplus2 = jnp.maximum(z, 0.0) + jnp.log2(1.0 + jnp.exp2(neg_abs))
        if diag:
            softplus2 = mask_first_block(softplus2)
        hi = softplus2.astype(BF16)
        lo = (softplus2 - hi.astype(F32)).astype(BF16)
        return [jnp.dot(jnp.concatenate([hi[:, h * BLK:(h + 1) * BLK], lo[:, h * BLK:(h + 1) * BLK]], axis=1),
                        neg_suffix_ones, preferred_element_type=F32) for h in range(2)]

    def weighted_values(p, z, cts, c, k0, diag):
        ws = [jnp.exp2(z[:, h * BLK:(h + 1) * BLK] + cts[h][:, :BLK] + c[h]) for h in range(2)]
        w = jnp.concatenate(ws, axis=1)
        if diag:
            w = mask_first_block(w)
        vv = by_head(v_ref[0, pl.ds(k0, BLK), pair_lanes(p)])
        return jnp.dot(w.astype(BF16), vv, preferred_element_type=F32)

    def superblock(sb, carry):
        base = pl.multiple_of(sb * SB_ROWS, SB_ROWS)
        acc_ref[...] = jnp.zeros_like(acc_ref)
        c_ref[...] = jnp.zeros_like(c_ref)

        def run(units):
            zs = [scores(p, q_ref[0, pl.ds(base + lo, n), pair_lanes(p)], k0) for p, lo, n, k0, _ in units]
            cts = [carries(z, unit[4]) for z, unit in zip(zs, units)]
            for z, ct, (p, lo, n, k0, diag) in zip(zs, cts, units):
                rows = slice(lo, lo + n)
                c = (c_ref[p, 0, rows, :], c_ref[p, 1, rows, :])
                acc_ref[p, rows, :] += weighted_values(p, z, ct, c, k0, diag)
                c_ref[p, 0, rows, :] = c[0] + ct[0][:, BLK:]
                c_ref[p, 1, rows, :] = c[1] + ct[1][:, BLK:]

        def alive(lo, n):
            c_max = None
            for p in range(n_pairs):
                for h in range(2):
                    c = c_ref[p, h, lo:lo + n, :]
                    c_max = c if c_max is None else jnp.maximum(c_max, c)
            return jnp.max(c_max) > SB_DEAD_LOG2

        run([(p, kq * BLK, SB_ROWS - kq * BLK, base + kq * BLK, True)
             for kq in reversed(range(per)) for p in range(n_pairs)])

        n_iter = sb * (per // SB_KEY_BLOCKS)

        def sweep(window, watch, jj0):
            def kstep(state):
                jj, _ = state
                run([(p, window[0], window[1],
                      pl.multiple_of(base - (SB_KEY_BLOCKS * jj + u + 1) * BLK, BLK), False)
                     for u in range(SB_KEY_BLOCKS) for p in range(n_pairs)])
                return jj + 1, alive(*watch)

            jj, _ = lax.while_loop(lambda st: jnp.logical_and(st[0] < n_iter, st[1]), kstep,
                                   (jj0, alive(*watch)))
            return jj

        half = SB_ROWS // 2
        jj = sweep((0, SB_ROWS), (half, half), jnp.int32(0))
        sweep((0, half), (0, half), jj)
        for p in range(n_pairs):
            o_ref[0, pl.ds(base, SB_ROWS), pair_lanes(p)] = acc_ref[p].astype(o_ref.dtype)
        return carry

    lax.fori_loop(0, s_len // SB_ROWS, superblock, 0)


def _sb_attention(qkv_sb):
    bn, s, _ = qkv_sb.shape
    n_pairs = H_SB // 2
    width = n_pairs * LANES
    spec = lambda t: pl.BlockSpec((1, s, width), lambda b: (b, 0, t))
    return pl.pallas_call(
        _sb_kernel,
        grid=(bn,),
        in_specs=[spec(0), spec(1), spec(2)],
        out_specs=pl.BlockSpec((1, s, width), lambda b: (b, 0, 0)),
        out_shape=jax.ShapeDtypeStruct((bn, s, width), BF16),
        scratch_shapes=[pltpu.VMEM((n_pairs, SB_ROWS, LANES), F32),
                        pltpu.VMEM((n_pairs, 2, SB_ROWS, LANES), F32)],
        compiler_params=_cparams(1),
        name="sb_attn",
    )(qkv_sb, qkv_sb, qkv_sb)


def _band_bias_tiles(bucket, table_ref, col, max_dist, sink=None):
    row = lax.broadcasted_iota(jnp.int32, (BLK, 2 * BLK), 0)
    kcol = lax.broadcasted_iota(jnp.int32, (BLK, 2 * BLK), 1)
    rel = row + BLK - kcol
    in_band = (rel >= 0) & (rel <= max_dist)
    bias = jnp.zeros((BLK, 2 * BLK), F32)
    for n in range(N_BUCKETS):
        bias = jnp.where(bucket == n, table_ref[n * N_SOFT_HEADS + col], bias)
    tiles = [jnp.where(in_band, bias, NEG_BIG), jnp.where(in_band & (kcol >= BLK), bias, NEG_BIG)]
    if sink is not None:
        tiles = [jnp.where(kcol == 0, sink, t) for t in tiles]
    return tiles


def _band_pair(q, kcat, vcat, biases):
    return _band_finish(*_band_pv(_band_scores(q, kcat, biases), vcat))


def _band_scores(q, kcat, biases):
    head0 = lax.broadcasted_iota(jnp.int32, q.shape, 1) < HEAD_DIM
    zero = jnp.zeros_like(q)
    return [_dot_nt(jnp.where(head0, q, zero) if h == 0 else jnp.where(head0, zero, q), kcat) + biases[h]
            for h in range(2)]


def _band_pv(scores, vcat):
    one = jnp.ones_like(vcat)
    v_head0 = lax.broadcasted_iota(jnp.int32, vcat.shape, 1) < HEAD_DIM
    outs, ms = [], []
    for h, s in enumerate(scores):
        vh = jnp.where(v_head0, vcat, one) if h == 0 else jnp.where(v_head0, one, vcat)
        m = jnp.max(s, axis=-1, keepdims=True)
        p = jnp.exp(s - m)
        outs.append(jnp.dot(p.astype(BF16), vh, preferred_element_type=F32))
        ms.append(m)
    return outs, ms


def _band_finish(outs, ms):
    head0 = lax.broadcasted_iota(jnp.int32, outs[0].shape, 1) < HEAD_DIM
    num = jnp.where(head0, outs[0], outs[1])
    den = pltpu.roll(jnp.where(head0, outs[1], outs[0]), HEAD_DIM, axis=1)
    return num, den, jnp.where(head0, ms[0], ms[1])


def _dil_kernel(table_ref, qkv_ref, bucket_ref, o_ref, bias_ref, og_ref, dg_ref, mg_ref, stage_ref, ostage_ref):
    s_len = o_ref.shape[1]
    n_blocks = s_len // BLK

    @pl.when(pl.program_id(0) == 0)
    def _():
        for g, (window, d) in enumerate(DIL_PATTERNS):
            for h in range(H_PER_DIL):
                head = g * H_PER_DIL + h
                normal, first = _band_bias_tiles(bucket_ref[g], table_ref, head, window // d)
                bias_ref[head, 0] = normal
                bias_ref[head, 1] = first

    for g, (window, d) in enumerate(DIL_PATTERNS):
        nb = s_len // d // BLK
        two_pass = d > MAX_STRIDE
        d_in = d // MAX_STRIDE if two_pass else d
        seg = s_len // MAX_STRIDE

        if two_pass:
            def stage_in(cc, carry):
                c1, j = cc // (seg // BLK), cc % (seg // BLK)
                for a in range(3):
                    stage_ref[a, pl.ds(pl.multiple_of(c1 * seg + j * BLK, BLK), BLK), :] = \
                        qkv_ref[0, 3 * g + a, pl.ds(c1 + MAX_STRIDE * BLK * j, BLK, stride=MAX_STRIDE), :]
                return carry

            lax.fori_loop(0, n_blocks, stage_in, 0)

        def src(a, start):
            if d == 1:
                return qkv_ref[0, 3 * g + a, pl.ds(start, BLK), :]
            if not two_pass:
                return qkv_ref[0, 3 * g + a, pl.ds(start, BLK, stride=d), :]
            return stage_ref[a, pl.ds(staged(start), BLK, stride=d_in), :]

        def staged(start):
            return (start % MAX_STRIDE) * seg + start // MAX_STRIDE

        def put(dst_ref, ostage_slot, start, val):
            if d == 1:
                dst_ref[g, pl.ds(start, BLK), :] = val
            elif not two_pass:
                dst_ref[g, pl.ds(start, BLK, stride=d), :] = val
            else:
                ostage_ref[ostage_slot, pl.ds(staged(start), BLK, stride=d_in), :] = val

        def scores(idx):
            r, i = idx // nb, idx % nb
            start = r + d * BLK * i
            q = src(0, start).astype(BF16)
            heads = (g * H_PER_DIL, g * H_PER_DIL + 1)
            prev = r + d * BLK * jnp.maximum(i - 1, 0)
            kcat = jnp.concatenate([src(1, prev), src(1, start)], axis=0).astype(BF16)
            vcat = jnp.concatenate([src(2, prev), src(2, start)], axis=0).astype(BF16)
            variant = jnp.where(i > 0, 0, 1)
            biases = [bias_ref[hd, variant] for hd in heads]
            return start, _band_scores(q, kcat, biases), vcat

        def body(it, carry):
            staged1 = [scores(it * BAND_UNROLL + u) for u in range(BAND_UNROLL)]
            staged2 = [(start, _band_pv(s, vcat)) for start, s, vcat in staged1]
            for start, (outs, ms) in staged2:
                num, den, m = _band_finish(outs, ms)
                put(og_ref, 0, start, num)
                put(dg_ref, 1, start, den)
                put(mg_ref, 2, start, m)
            return carry

        lax.fori_loop(0, n_blocks // BAND_UNROLL, body, 0)

        if two_pass:
            def stage_out(cc, carry):
                c1, j = cc // (seg // BLK), cc % (seg // BLK)
                rows_in = pl.ds(pl.multiple_of(c1 * seg + j * BLK, BLK), BLK)
                rows_out = pl.ds(c1 + MAX_STRIDE * BLK * j, BLK, stride=MAX_STRIDE)
                og_ref[g, rows_out, :] = ostage_ref[0, rows_in, :]
                dg_ref[g, rows_out, :] = ostage_ref[1, rows_in, :]
                mg_ref[g, rows_out, :] = ostage_ref[2, rows_in, :]
                return carry

            lax.fori_loop(0, n_blocks, stage_out, 0)

    def merge(i, carry):
        sl = pl.ds(pl.multiple_of(i * BLK, BLK), BLK)
        m0, m1, m2 = mg_ref[0, sl, :], mg_ref[1, sl, :], mg_ref[2, sl, :]
        m = jnp.maximum(jnp.maximum(m0, m1), m2)
        e0, e1, e2 = jnp.exp(m0 - m), jnp.exp(m1 - m), jnp.exp(m2 - m)
        num = e0 * og_ref[0, sl, :] + e1 * og_ref[1, sl, :] + e2 * og_ref[2, sl, :]
        den = e0 * dg_ref[0, sl, :] + e1 * dg_ref[1, sl, :] + e2 * dg_ref[2, sl, :]
        o_ref[0, sl, :] = (num / den).astype(o_ref.dtype)
        return carry

    lax.fori_loop(0, n_blocks, merge, 0)


def _dil_attention(qkv_dil, buckets, table):
    bn, n_slabs, s, _ = qkv_dil.shape
    n_groups = len(DIL_PATTERNS)
    return pl.pallas_call(
        _dil_kernel,
        grid_spec=pltpu.PrefetchScalarGridSpec(
            num_scalar_prefetch=1,
            grid=(bn,),
            in_specs=[pl.BlockSpec((1, n_slabs, s, LANES), lambda b, tb: (b, 0, 0, 0)),
                      pl.BlockSpec(buckets.shape, lambda b, tb: (0, 0, 0))],
            out_specs=pl.BlockSpec((1, s, LANES), lambda b, tb: (b, 0, 0)),
            scratch_shapes=[pltpu.VMEM((H_DIL, 2, BLK, 2 * BLK), F32),
                            pltpu.VMEM((n_groups, s, LANES), F32),
                            pltpu.VMEM((n_groups, s, LANES), F32),
                            pltpu.VMEM((n_groups, s, LANES), F32),
                            pltpu.VMEM((3, s, LANES), F32),
                            pltpu.VMEM((3, s, LANES), F32)],
        ),
        out_shape=jax.ShapeDtypeStruct((bn, s, LANES), BF16),
        compiler_params=pltpu.CompilerParams(dimension_semantics=("arbitrary",),
                                             vmem_limit_bytes=VMEM_LIMIT),
        name="dil_attn",
    )(table, qkv_dil, buckets)


def _swa_kernel(table_ref, sink_ref, qkv_ref, bucket_ref, o_ref, bias_ref):
    s_len = o_ref.shape[1]
    k_col = N_SWA_PAIRS * LANES
    v_col = k_col + LANES
    key_row = lax.broadcasted_iota(jnp.int32, (2 * BLK, LANES), 0)

    @pl.when(pl.program_id(0) == 0)
    def _():
        for slot, hd in enumerate(_SWA_HEAD_ORDER):
            normal, first = _band_bias_tiles(bucket_ref[0], table_ref, H_DIL + hd, SWA_WINDOW - 1, sink_ref[hd])
            bias_ref[slot, 0] = normal
            bias_ref[slot, 1] = first

    def scores(i):
        start = pl.multiple_of(i * BLK, BLK)
        prev = pl.multiple_of(jnp.maximum(i - 1, 0) * BLK, BLK)
        kcat = jnp.concatenate([qkv_ref[0, pl.ds(prev, BLK), k_col:k_col + LANES],
                                qkv_ref[0, pl.ds(start, BLK), k_col:k_col + LANES]], axis=0)
        vcat = jnp.concatenate([qkv_ref[0, pl.ds(prev, BLK), v_col:v_col + LANES],
                                qkv_ref[0, pl.ds(start, BLK), v_col:v_col + LANES]], axis=0)
        kcat = jnp.where(key_row == 0, jnp.zeros_like(kcat), kcat)
        vcat = jnp.where(key_row == 0, jnp.zeros_like(vcat), vcat)
        variant = jnp.where(i > 0, 0, 1)
        out = []
        for p in range(N_SWA_PAIRS):
            q = qkv_ref[0, pl.ds(start, BLK), p * LANES:(p + 1) * LANES]
            biases = [bias_ref[2 * p + h, variant] for h in range(2)]
            out.append((start, p, _band_scores(q, kcat, biases), vcat))
        return out

    def body(it, carry):
        staged1 = [unit for u in range(SWA_UNROLL) for unit in scores(it * SWA_UNROLL + u)]
        staged2 = [(start, p, _band_pv(s, vcat)) for start, p, s, vcat in staged1]
        for start, p, (outs, ms) in staged2:
            num, den, _ = _band_finish(outs, ms)
            o_ref[0, pl.ds(start, BLK), p * LANES:(p + 1) * LANES] = (num * (1.0 / den)).astype(o_ref.dtype)
        return carry

    lax.fori_loop(0, s_len // BLK // SWA_UNROLL, body, 0)


def _swa_attention(qkv_swa, bucket, table, sinks_l):
    bn, s, n = qkv_swa.shape
    return pl.pallas_call(
        _swa_kernel,
        grid_spec=pltpu.PrefetchScalarGridSpec(
            num_scalar_prefetch=2,
            grid=(bn,),
            in_specs=[pl.BlockSpec((1, s, n), lambda b, tb, sk: (b, 0, 0)),
                      pl.BlockSpec(bucket.shape, lambda b, tb, sk: (0, 0, 0))],
            out_specs=pl.BlockSpec((1, s, N_SWA_PAIRS * LANES), lambda b, tb, sk: (b, 0, 0)),
            scratch_shapes=[pltpu.VMEM((H_SWA_Q, 2, BLK, 2 * BLK), F32)],
        ),
        out_shape=jax.ShapeDtypeStruct((bn, s, N_SWA_PAIRS * LANES), BF16),
        compiler_params=pltpu.CompilerParams(dimension_semantics=("arbitrary",),
                                             vmem_limit_bytes=VMEM_LIMIT),
        name="swa_attn",
    )(table, sinks_l, qkv_swa, bucket)


def _merge_kernel(x_ref, h_ref, mod_ref, osb_ref, odil_ref, oswa_ref,
                  win_ref, wsb_ref, wdil_ref, wswa_ref, wout_ref, o_ref):
    x = x_ref[0]
    mod = mod_ref[0]
    d = x.shape[1]
    gate0 = win_ref.shape[1] - 3 * d
    h = h_ref[0]
    merged = None
    for t, (o_br, w_br) in enumerate(((osb_ref, wsb_ref), (odil_ref, wdil_ref), (oswa_ref, wswa_ref))):
        w_gate = win_ref[:, gate0 + t * d:gate0 + (t + 1) * d]
        gate = jax.nn.sigmoid(jnp.dot(h, w_gate, preferred_element_type=F32))
        term = gate * jnp.dot(o_br[0], w_br[...], preferred_element_type=F32)
        merged = term if merged is None else merged + term
    y = jnp.dot(merged.astype(BF16), wout_ref[...], preferred_element_type=F32)
    o_ref[0] = x + mod[5:6] * y


def _merge(x, h, mod, o_sb, o_dil, o_swa, w_in, w_sb, w_dil, w_swa, w_out, layer):
    bn, s, d = x.shape
    row = lambda n: pl.BlockSpec((1, ROW_TILE, n), lambda b, i: (b, i, 0))
    return pl.pallas_call(
        _merge_kernel,
        grid=(bn, s // ROW_TILE),
        in_specs=[row(d), row(d),
                  pl.BlockSpec((1, ADA_CHUNKS, d), lambda b, i: (b, 0, 0)),
                  row(o_sb.shape[2]), row(o_dil.shape[2]), row(o_swa.shape[2]),
                  _slab_spec(w_in, layer), _slab_spec(w_sb, layer), _slab_spec(w_dil, layer),
                  _const_spec(w_swa.shape), _slab_spec(w_out, layer)],
        out_specs=row(d),
        out_shape=jax.ShapeDtypeStruct(x.shape, F32),
        compiler_params=_cparams(2),
        name="merge",
    )(x, h, mod, o_sb, o_dil, o_swa, w_in, w_sb, w_dil, w_swa, w_out)


def _t5_bucket(n):
    max_exact = N_BUCKETS // 2
    nf = jnp.maximum(n, 1).astype(jnp.float32)
    large = max_exact + (jnp.log(nf / max_exact) / math.log(MAX_REL_DIST / max_exact)
                         * (N_BUCKETS - max_exact)).astype(jnp.int32)
    large = jnp.minimum(large, N_BUCKETS - 1)
    return jnp.where(n < max_exact, n, large)


def _band_buckets(dilation):
    rel = jnp.arange(BLK)[:, None] + BLK - jnp.arange(2 * BLK)[None, :]
    return _t5_bucket(jnp.maximum(rel, 0) * dilation).astype(jnp.int32)


def _reorder_heads(w, order, axis):
    parts = [lax.slice_in_dim(w, hd * HEAD_DIM, (hd + 1) * HEAD_DIM, axis=axis) for hd in order]
    return jnp.concatenate(parts, axis=axis)


def kernel(x, c, w_ada, b_ada, norm_gain, w_ffn_gate, w_ffn_up, w_ffn_down, w_in,
           w_br_sb, w_br_dil, w_br_swa, w_out, sinks, rel_bias, final_gain):
    depth = w_ada.shape[0]
    bn, s, d = x.shape
    mods = _ada(c, w_ada, b_ada).reshape(depth, bn, ADA_CHUNKS, d)

    buckets = jnp.stack([_band_buckets(dil) for _, dil in DIL_PATTERNS])
    table = rel_bias.reshape(-1)
    wg, wu, wd = w_ffn_gate, w_ffn_up, w_ffn_down
    w_in_bf, w_sb_bf, w_dil_bf, w_out_bf = (w_in.astype(BF16), w_br_sb.astype(BF16),
                                            w_br_dil.astype(BF16), w_out.astype(BF16))

    for l in range(depth):
        mod = mods[l]
        x = _ffn(x, mod, norm_gain[l, 0], wg, wu, wd, l, 0, 0)

        qkv_sb, qkv_dil, qkv_swa, h_mix = _inproj(x, mod, norm_gain[l, 1], w_in_bf, l)
        o_sb = _sb_attention(qkv_sb)
        o_dil = _dil_attention(qkv_dil, buckets, table)
        o_swa = _swa_attention(qkv_swa, buckets[:1], table, sinks[l])
        w_swa = _reorder_heads(w_br_swa[l], _SWA_HEAD_ORDER, 0).astype(BF16)
        x = _merge(x, h_mix, mod, o_sb, o_dil, o_swa, w_in_bf, w_sb_bf, w_dil_bf, w_swa, w_out_bf, l)

        last = l == depth - 1
        x = _ffn(x, mod, norm_gain[l, 2], wg, wu, wd, l, 1, 2, final_gain=final_gain if last else None)
    return x
```

```python
import functools
import math

import jax
import jax.numpy as jnp
from jax import lax
from jax.experimental import pallas as pl
from jax.experimental.pallas import tpu as pltpu

F32 = jnp.float32
BF16 = jnp.bfloat16

HEAD_DIM = 64
LANES = 128
BLK = 128
H_SB = 4
DIL_PATTERNS = ((128, 1), (512, 4), (2048, 16))
H_PER_DIL = 2
H_DIL = H_PER_DIL * len(DIL_PATTERNS)
H_SWA_Q = 6
H_SWA_KV = 2
SWA_WINDOW = 128
N_SOFT_HEADS = H_DIL + H_SWA_Q
N_SWA_PAIRS = H_SWA_Q // 2
N_BUCKETS = 32
MAX_REL_DIST = 2048
RMS_EPS = 1e-6
ADA_CHUNKS = 9
NEG_BIG = -1e30
LOG2E = 1.4426950408889634

VMEM_LIMIT = 56 * 1024 * 1024
ROW_TILE = 512
FF_CHUNK = 256
ADA_TILE = 3072
SB_ROWS = 512
SB_KEY_BLOCKS = 2
SB_DEAD_LOG2 = -150.0
BAND_UNROLL = 16
MAX_STRIDE = 4
SWA_UNROLL = 8

_SWA_HEAD_ORDER = tuple(p + (H_SWA_Q // H_SWA_KV) * h for p in range(N_SWA_PAIRS) for h in range(2))


def _cparams(n_axes):
    return pltpu.CompilerParams(dimension_semantics=("parallel",) * n_axes,
                                vmem_limit_bytes=VMEM_LIMIT)


def _const_spec(shape):
    nd = len(shape)
    return pl.BlockSpec(shape, lambda *_: (0,) * nd, pipeline_mode=pl.Buffered(1))


def _slab_spec(arr, *lead):
    block = (None,) * len(lead) + tuple(arr.shape[len(lead):])
    return pl.BlockSpec(block, lambda *_: tuple(lead) + (0, 0), pipeline_mode=pl.Buffered(1))


def _ada_kernel(c_ref, w_ref, b_ref, o_ref):
    c = c_ref[...]
    sc = (c * jax.nn.sigmoid(c)).astype(BF16)
    o_ref[0] = jnp.dot(sc, w_ref[0].astype(BF16), preferred_element_type=F32) + b_ref[0]


def _ada(c, w_ada, b_ada):
    depth, d, n = w_ada.shape
    bn = c.shape[0]
    return pl.pallas_call(
        _ada_kernel,
        grid=(depth, n // ADA_TILE),
        in_specs=[pl.BlockSpec((bn, d), lambda l, j: (0, 0)),
                  pl.BlockSpec((1, d, ADA_TILE), lambda l, j: (l, 0, j)),
                  pl.BlockSpec((1, 1, ADA_TILE), lambda l, j: (l, 0, j))],
        out_specs=pl.BlockSpec((1, bn, ADA_TILE), lambda l, j: (l, 0, j)),
        out_shape=jax.ShapeDtypeStruct((depth, bn, n), F32),
        compiler_params=_cparams(2),
        name="ada_mod",
    )(c, w_ada, b_ada.reshape(depth, 1, n))


def _rms(x, gain):
    ms = jnp.mean(x * x, axis=-1, keepdims=True)
    return x * lax.rsqrt(ms + RMS_EPS) * gain


def _prenorm(x, gain, mod, j):
    shift = mod[3 * j:3 * j + 1]
    scale = mod[3 * j + 1:3 * j + 2]
    ms = jnp.mean(x * x, axis=-1, keepdims=True)
    return (x * lax.rsqrt(ms + RMS_EPS)) * (gain * (1.0 + scale)) + shift


def _dot_nt(a, b):
    return lax.dot_general(a, b, (((1,), (1,)), ((), ())), preferred_element_type=F32)


def _ffn_kernel(*refs, j, final, layer, which):
    n_in = 7 if final else 6
    x_ref, mod_ref, gain_ref, wg_hbm, wu_hbm, wd_hbm = refs[:6]
    fg_ref = refs[6] if final else None
    o_ref, wg_ref, wu_ref, wd_ref, stage_g_ref, stage_u_ref, stage_d_ref, sem_ref = refs[n_in:]
    n_chunks = wg_ref.shape[1] // FF_CHUNK
    chunk = lambda ci: slice(ci * FF_CHUNK, (ci + 1) * FF_CHUNK)
    first = (pl.program_id(0) == 0) & (pl.program_id(1) == 0)

    def chunk_copies(ci):
        slot = ci % 2
        return [pltpu.make_async_copy(wg_hbm.at[layer, which, :, chunk(ci)], stage_g_ref.at[slot], sem_ref.at[slot, 0]),
                pltpu.make_async_copy(wu_hbm.at[layer, which, :, chunk(ci)], stage_u_ref.at[slot], sem_ref.at[slot, 1]),
                pltpu.make_async_copy(wd_hbm.at[layer, which, chunk(ci), :], stage_d_ref.at[slot], sem_ref.at[slot, 2])]

    def compute(stream_weights):
        if stream_weights:
            for ci in range(min(2, n_chunks)):
                for cp in chunk_copies(ci):
                    cp.start()
        x = x_ref[0]
        mod = mod_ref[0]
        h = _prenorm(x, gain_ref[...], mod, j).astype(BF16)
        acc = jnp.zeros(x.shape, F32)
        for ci in range(n_chunks):
            if stream_weights:
                for cp in chunk_copies(ci):
                    cp.wait()
                wg_ref[:, chunk(ci)] = stage_g_ref[ci % 2].astype(BF16)
                wu_ref[:, chunk(ci)] = stage_u_ref[ci % 2].astype(BF16)
                wd_ref[chunk(ci), :] = stage_d_ref[ci % 2].astype(BF16)
                if ci + 2 < n_chunks:
                    for cp in chunk_copies(ci + 2):
                        cp.start()
            g = jnp.dot(h, wg_ref[:, chunk(ci)], preferred_element_type=F32)
            u = jnp.dot(h, wu_ref[:, chunk(ci)], preferred_element_type=F32)
            a = (g * jax.nn.sigmoid(g) * u).astype(BF16)
            acc = acc + jnp.dot(a, wd_ref[chunk(ci), :], preferred_element_type=F32)
        y = x + (0.5 * mod[3 * j + 2:3 * j + 3]) * acc
        if final:
            y = _rms(y, fg_ref[...])
        o_ref[0] = y

    @pl.when(first)
    def _():
        compute(True)

    @pl.when(jnp.logical_not(first))
    def _():
        compute(False)


def _ffn(x, mod, gain, wg, wu, wd, layer, which, j, final_gain=None):
    bn, s, d = x.shape
    d_ff = wg.shape[-1]
    final = final_gain is not None
    hbm = pl.BlockSpec(memory_space=pl.ANY)
    in_specs = [pl.BlockSpec((1, ROW_TILE, d), lambda b, i: (b, i, 0)),
                pl.BlockSpec((1, ADA_CHUNKS, d), lambda b, i: (b, 0, 0)),
                _const_spec((1, d)), hbm, hbm, hbm]
    args = [x, mod, gain.reshape(1, d), wg, wu, wd]
    if final:
        in_specs.append(_const_spec((1, d)))
        args.append(final_gain.reshape(1, d))
    return pl.pallas_call(
        functools.partial(_ffn_kernel, j=j, final=final, layer=layer, which=which),
        grid=(bn, s // ROW_TILE),
        in_specs=in_specs,
        out_specs=pl.BlockSpec((1, ROW_TILE, d), lambda b, i: (b, i, 0)),
        out_shape=jax.ShapeDtypeStruct(x.shape, F32),
        scratch_shapes=[pltpu.VMEM((d, d_ff), BF16), pltpu.VMEM((d, d_ff), BF16), pltpu.VMEM((d_ff, d), BF16),
                        pltpu.VMEM((2, d, FF_CHUNK), F32), pltpu.VMEM((2, d, FF_CHUNK), F32),
                        pltpu.VMEM((2, FF_CHUNK, d), F32), pltpu.SemaphoreType.DMA((2, 3))],
        compiler_params=pltpu.CompilerParams(dimension_semantics=("arbitrary", "arbitrary"),
                                             vmem_limit_bytes=VMEM_LIMIT),
        name="ffn_final" if final else f"ffn{j}",
    )(*args)


N_SB = 3 * H_SB * HEAD_DIM
N_DIL = 3 * H_DIL * HEAD_DIM
N_SWA = (H_SWA_Q + 2 * H_SWA_KV) * HEAD_DIM
N_QKV = N_SB + N_DIL + N_SWA
N_DIL_SLABS = N_DIL // LANES


N_INPROJ_HEAD = 7 * 256


def _inproj_kernel(x_ref, mod_ref, gain_ref, w_ref, sb_ref, dil_ref, swa_ref, hout_ref, wtail_ref):
    qs = HEAD_DIM ** -0.5
    n_q_sb = H_SB * HEAD_DIM
    n_groups = len(DIL_PATTERNS)
    n_q_swa = H_SWA_Q * HEAD_DIM
    q0 = N_SB + N_DIL

    @pl.when((pl.program_id(0) == 0) & (pl.program_id(1) == 0))
    def _():
        n_dil_tail = q0 - N_INPROJ_HEAD
        wtail_ref[:, :n_dil_tail] = w_ref[:, N_INPROJ_HEAD:q0]
        for slot, hd in enumerate(_SWA_HEAD_ORDER):
            wtail_ref[:, n_dil_tail + slot * HEAD_DIM:n_dil_tail + (slot + 1) * HEAD_DIM] = \
                w_ref[:, q0 + hd * HEAD_DIM:q0 + (hd + 1) * HEAD_DIM]
        wtail_ref[:, n_dil_tail + n_q_swa:] = w_ref[:, q0 + n_q_swa:N_QKV]

    h = _prenorm(x_ref[0], gain_ref[...], mod_ref[0], 1).astype(BF16)
    hout_ref[0] = h
    r = jnp.concatenate([jnp.dot(h, w_ref[:, :N_INPROJ_HEAD], preferred_element_type=F32),
                         jnp.dot(h, wtail_ref[...], preferred_element_type=F32)], axis=1)
    sb_ref[0, :, :n_q_sb] = (r[:, :n_q_sb] * (qs * LOG2E)).astype(BF16)
    sb_ref[0, :, n_q_sb:] = r[:, n_q_sb:N_SB].astype(BF16)
    for a in range(3):
        for g in range(n_groups):
            c0 = N_SB + (a * n_groups + g) * LANES
            slab = r[:, c0:c0 + LANES]
            dil_ref[0, 3 * g + a] = slab * qs if a == 0 else slab
    swa_ref[0, :, :n_q_swa] = (r[:, q0:q0 + n_q_swa] * qs).astype(BF16)
    swa_ref[0, :, n_q_swa:] = r[:, q0 + n_q_swa:].astype(BF16)


def _inproj(x, mod, gain, w_in, layer):
    bn, s, d = x.shape
    nt = s // ROW_TILE
    n_tail = N_QKV - N_INPROJ_HEAD
    return pl.pallas_call(
        _inproj_kernel,
        grid=(bn, nt),
        in_specs=[pl.BlockSpec((1, ROW_TILE, d), lambda b, i: (b, i, 0)),
                  pl.BlockSpec((1, ADA_CHUNKS, d), lambda b, i: (b, 0, 0)),
                  _const_spec((1, d)), _slab_spec(w_in, layer)],
        out_specs=[pl.BlockSpec((1, ROW_TILE, N_SB), lambda b, i: (b, i, 0)),
                   pl.BlockSpec((1, N_DIL_SLABS, ROW_TILE, LANES), lambda b, i: (b, 0, i, 0)),
                   pl.BlockSpec((1, ROW_TILE, N_SWA), lambda b, i: (b, i, 0)),
                   pl.BlockSpec((1, ROW_TILE, d), lambda b, i: (b, i, 0))],
        out_shape=[jax.ShapeDtypeStruct((bn, s, N_SB), BF16),
                   jax.ShapeDtypeStruct((bn, N_DIL_SLABS, s, LANES), F32),
                   jax.ShapeDtypeStruct((bn, s, N_SWA), BF16),
                   jax.ShapeDtypeStruct((bn, s, d), BF16)],
        scratch_shapes=[pltpu.VMEM((d, n_tail), BF16)],
        compiler_params=pltpu.CompilerParams(dimension_semantics=("arbitrary", "arbitrary"),
                                             vmem_limit_bytes=VMEM_LIMIT),
        name="in_proj",
    )(x, mod, gain.reshape(1, d), w_in)


def _sb_kernel(q_ref, k_ref, v_ref, o_ref, acc_ref, c_ref):
    s_len = q_ref.shape[1]
    n_pairs = q_ref.shape[2] // LANES
    per = SB_ROWS // BLK
    lane = lax.broadcasted_iota(jnp.int32, (BLK, LANES), 1)
    head0 = lane < HEAD_DIM
    row2 =lax.broadcasted_iota(jnp.int32, (BLK, 2 * BLK), 0)
    col2 = lax.broadcasted_iota(jnp.int32, (BLK, 2 * BLK), 1)
    strict2 = (col2 & (BLK - 1)) < row2
    rk = lax.broadcasted_iota(jnp.int32, (2 * BLK, 2 * BLK), 0)
    cn = lax.broadcasted_iota(jnp.int32, (2 * BLK, 2 * BLK), 1)
    neg_suffix_ones = jnp.where(((rk & (BLK - 1)) >= cn) | (cn >= BLK), -1.0, 0.0).astype(BF16)

    def by_head(t):
        zero = jnp.zeros_like(t)
        return jnp.concatenate([jnp.where(head0, t, zero), jnp.where(head0, zero, t)], axis=0)

    def mask_first_block(t):
        first = jnp.where(strict2, t[:BLK], 0.0)
        return first if t.shape[0] == BLK else jnp.concatenate([first, t[BLK:]], axis=0)

    def pair_lanes(p):
        return slice(p * LANES, (p + 1) * LANES)

    def scores(p, q, k0):
        return _dot_nt(q, by_head(k_ref[0, pl.ds(k0, BLK), pair_lanes(p)]))

    def carries(z, diag):
        neg_abs = lax.bitcast_convert_type(lax.bitcast_convert_type(z, jnp.int32) | jnp.int32(-2 ** 31), F32)
        softplus2 = jnp.maximum(z, 0.0) + jnp.log2(1.0 + jnp.exp2(neg_abs))
        if diag:
            softplus2 = mask_first_block(softplus2)
        hi = softplus2.astype(BF16)
        lo = (softplus2 - hi.astype(F32)).astype(BF16)
        return [jnp.dot(jnp.concatenate([hi[:, h * BLK:(h + 1) * BLK], lo[:, h * BLK:(h + 1) * BLK]], axis=1),
                        neg_suffix_ones, preferred_element_type=F32) for h in range(2)]

    def weighted_values(p, z, cts, c, k0, diag):
        ws = [jnp.exp2(z[:, h * BLK:(h + 1) * BLK] + cts[h][:, :BLK] + c[h]) for h in range(2)]
        w = jnp.concatenate(ws, axis=1)
        if diag:
            w = mask_first_block(w)
        vv = by_head(v_ref[0, pl.ds(k0, BLK), pair_lanes(p)])
        return jnp.dot(w.astype(BF16), vv, preferred_element_type=F32)

    def superblock(sb, carry):
        base = pl.multiple_of(sb * SB_ROWS, SB_ROWS)
        acc_ref[...] = jnp.zeros_like(acc_ref)
        c_ref[...] = jnp.zeros_like(c_ref)

        def run(units):
            zs = [scores(p, q_ref[0, pl.ds(base + lo, n), pair_lanes(p)], k0) for p, lo, n, k0, _ in units]
            cts = [carries(z, unit[4]) for z, unit in zip(zs, units)]
            for z, ct, (p, lo, n, k0, diag) in zip(zs, cts, units):
                rows = slice(lo, lo + n)
                c = (c_ref[p, 0, rows, :], c_ref[p, 1, rows, :])
                acc_ref[p, rows, :] += weighted_values(p, z, ct, c, k0, diag)
                c_ref[p, 0, rows, :] = c[0] + ct[0][:, BLK:]
                c_ref[p, 1, rows, :] = c[1] + ct[1][:, BLK:]

        def alive(lo, n):
            c_max = None
            for p in range(n_pairs):
                for h in range(2):
                    c = c_ref[p, h, lo:lo + n, :]
                    c_max = c if c_max is None else jnp.maximum(c_max, c)
            return jnp.max(c_max) > SB_DEAD_LOG2

        run([(p, kq * BLK, SB_ROWS - kq * BLK, base + kq * BLK, True)
             for kq in reversed(range(per)) for p in range(n_pairs)])

        n_iter = sb * (per // SB_KEY_BLOCKS)

        def sweep(window, watch, jj0):
            def kstep(state):
                jj, _ = state
                run([(p, window[0], window[1],
                      pl.multiple_of(base - (SB_KEY_BLOCKS * jj + u + 1) * BLK, BLK), False)
                     for u in range(SB_KEY_BLOCKS) for p in range(n_pairs)])
                return jj + 1, alive(*watch)

            jj, _ = lax.while_loop(lambda st: jnp.logical_and(st[0] < n_iter, st[1]), kstep,
                                   (jj0, alive(*watch)))
            return jj

        half = SB_ROWS // 2
        jj = sweep((0, SB_ROWS), (half, half), jnp.int32(0))
        sweep((0, half), (0, half), jj)
        for p in range(n_pairs):
            o_ref[0, pl.ds(base, SB_ROWS), pair_lanes(p)] = acc_ref[p].astype(o_ref.dtype)
        return carry

    lax.fori_loop(0, s_len // SB_ROWS, superblock, 0)


def _sb_attention(qkv_sb):
    bn, s, _ = qkv_sb.shape
    n_pairs = H_SB // 2
    width = n_pairs * LANES
    spec = lambda t: pl.BlockSpec((1, s, width), lambda b: (b, 0, t))
    return pl.pallas_call(
        _sb_kernel,
        grid=(bn,),
        in_specs=[spec(0), spec(1), spec(2)],
        out_specs=pl.BlockSpec((1, s, width), lambda b: (b, 0, 0)),
        out_shape=jax.ShapeDtypeStruct((bn, s, width), BF16),
        scratch_shapes=[pltpu.VMEM((n_pairs, SB_ROWS, LANES), F32),
                        pltpu.VMEM((n_pairs, 2, SB_ROWS, LANES), F32)],
        compiler_params=_cparams(1),
        name="sb_attn",
    )(qkv_sb, qkv_sb, qkv_sb)


def _band_bias_tiles(bucket, table_ref, col, max_dist, sink=None):
    row = lax.broadcasted_iota(jnp.int32, (BLK, 2 * BLK), 0)
    kcol = lax.broadcasted_iota(jnp.int32, (BLK, 2 * BLK), 1)
    rel = row + BLK - kcol
    in_band = (rel >= 0) & (rel <= max_dist)
    bias = jnp.zeros((BLK, 2 * BLK), F32)
    for n in range(N_BUCKETS):
        bias = jnp.where(bucket == n, table_ref[n * N_SOFT_HEADS + col], bias)
    tiles = [jnp.where(in_band, bias, NEG_BIG), jnp.where(in_band & (kcol >= BLK), bias, NEG_BIG)]
    if sink is not None:
        tiles = [jnp.where(kcol == 0, sink, t) for t in tiles]
    return tiles


def _band_pair(q, kcat, vcat, biases):
    return _band_finish(*_band_pv(_band_scores(q, kcat, biases), vcat))


def _band_scores(q, kcat, biases):
    head0 = lax.broadcasted_iota(jnp.int32, q.shape, 1) < HEAD_DIM
    zero = jnp.zeros_like(q)
    return [_dot_nt(jnp.where(head0, q, zero) if h == 0 else jnp.where(head0, zero, q), kcat) + biases[h]
            for h in range(2)]


def _band_pv(scores, vcat):
    one = jnp.ones_like(vcat)
    v_head0 = lax.broadcasted_iota(jnp.int32, vcat.shape, 1) < HEAD_DIM
    outs, ms = [], []
    for h, s in enumerate(scores):
        vh = jnp.where(v_head0, vcat, one) if h == 0 else jnp.where(v_head0, one, vcat)
        m = jnp.max(s, axis=-1, keepdims=True)
        p = jnp.exp(s - m)
        outs.append(jnp.dot(p.astype(BF16), vh, preferred_element_type=F32))
        ms.append(m)
    return outs, ms


def _band_finish(outs, ms):
    head0 = lax.broadcasted_iota(jnp.int32, outs[0].shape, 1) < HEAD_DIM
    num = jnp.where(head0, outs[0], outs[1])
    den = pltpu.roll(jnp.where(head0, outs[1], outs[0]), HEAD_DIM, axis=1)
    return num, den, jnp.where(head0, ms[0], ms[1])


def _dil_kernel(table_ref, qkv_ref, bucket_ref, o_ref, bias_ref, og_ref, dg_ref, mg_ref, stage_ref, ostage_ref):
    s_len = o_ref.shape[1]
    n_blocks = s_len // BLK

    @pl.when(pl.program_id(0) == 0)
    def _():
        for g, (window, d) in enumerate(DIL_PATTERNS):
            for h in range(H_PER_DIL):
                head = g * H_PER_DIL + h
                normal, first = _band_bias_tiles(bucket_ref[g], table_ref, head, window // d)
                bias_ref[head, 0] = normal
                bias_ref[head, 1] = first

    for g, (window, d) in enumerate(DIL_PATTERNS):
        nb = s_len // d // BLK
        two_pass = d > MAX_STRIDE
        d_in = d // MAX_STRIDE if two_pass else d
        seg = s_len // MAX_STRIDE

        if two_pass:
            def stage_in(cc, carry):
                c1, j = cc // (seg // BLK), cc % (seg // BLK)
                for a in range(3):
                    stage_ref[a, pl.ds(pl.multiple_of(c1 * seg + j * BLK, BLK), BLK), :] = \
                        qkv_ref[0, 3 * g + a, pl.ds(c1 + MAX_STRIDE * BLK * j, BLK, stride=MAX_STRIDE), :]
                return carry

            lax.fori_loop(0, n_blocks, stage_in, 0)

        def src(a, start):
            if d == 1:
                return qkv_ref[0, 3 * g + a, pl.ds(start, BLK), :]
            if not two_pass:
                return qkv_ref[0, 3 * g + a, pl.ds(start, BLK, stride=d), :]
            return stage_ref[a, pl.ds(staged(start), BLK, stride=d_in), :]

        def staged(start):
            return (start % MAX_STRIDE) * seg + start // MAX_STRIDE

        def put(dst_ref, ostage_slot, start, val):
            if d == 1:
                dst_ref[g, pl.ds(start, BLK), :] = val
            elif not two_pass:
                dst_ref[g, pl.ds(start, BLK, stride=d), :] = val
            else:
                ostage_ref[ostage_slot, pl.ds(staged(start), BLK, stride=d_in), :] = val

        def scores(idx):
            r, i = idx // nb, idx % nb
            start = r + d * BLK * i
            q = src(0, start).astype(BF16)
            heads = (g * H_PER_DIL, g * H_PER_DIL + 1)
            prev = r + d * BLK * jnp.maximum(i - 1, 0)
            kcat = jnp.concatenate([src(1, prev), src(1, start)], axis=0).astype(BF16)
            vcat = jnp.concatenate([src(2, prev), src(2, start)], axis=0).astype(BF16)
            variant = jnp.where(i > 0, 0, 1)
            biases = [bias_ref[hd, variant] for hd in heads]
            return start, _band_scores(q, kcat, biases), vcat

        def body(it, carry):
            staged1 = [scores(it * BAND_UNROLL + u) for u in range(BAND_UNROLL)]
            staged2 = [(start, _band_pv(s, vcat)) for start, s, vcat in staged1]
            for start, (outs, ms) in staged2:
                num, den, m = _band_finish(outs, ms)
                put(og_ref, 0, start, num)
                put(dg_ref, 1, start, den)
                put(mg_ref, 2, start, m)
            return carry

        lax.fori_loop(0, n_blocks // BAND_UNROLL, body, 0)

        if two_pass:
            def stage_out(cc, carry):
                c1, j = cc // (seg // BLK), cc % (seg // BLK)
                rows_in = pl.ds(pl.multiple_of(c1 * seg + j * BLK, BLK), BLK)
                rows_out = pl.ds(c1 + MAX_STRIDE * BLK * j, BLK, stride=MAX_STRIDE)
                og_ref[g, rows_out, :] = ostage_ref[0, rows_in, :]
                dg_ref[g, rows_out, :] = ostage_ref[1, rows_in, :]
                mg_ref[g, rows_out, :] = ostage_ref[2, rows_in, :]
                return carry

            lax.fori_loop(0, n_blocks, stage_out, 0)

    def merge(i, carry):
        sl = pl.ds(pl.multiple_of(i * BLK, BLK), BLK)
        m0, m1, m2 = mg_ref[0, sl, :], mg_ref[1, sl, :], mg_ref[2, sl, :]
        m = jnp.maximum(jnp.maximum(m0, m1), m2)
        e0, e1, e2 = jnp.exp(m0 - m), jnp.exp(m1 - m), jnp.exp(m2 - m)
        num = e0 * og_ref[0, sl, :] + e1 * og_ref[1, sl, :] + e2 * og_ref[2, sl, :]
        den = e0 * dg_ref[0, sl, :] + e1 * dg_ref[1, sl, :] + e2 * dg_ref[2, sl, :]
        o_ref[0, sl, :] = (num / den).astype(o_ref.dtype)
        return carry

    lax.fori_loop(0, n_blocks, merge, 0)


def _dil_attention(qkv_dil, buckets, table):
    bn, n_slabs, s, _ = qkv_dil.shape
    n_groups = len(DIL_PATTERNS)
    return pl.pallas_call(
        _dil_kernel,
        grid_spec=pltpu.PrefetchScalarGridSpec(
            num_scalar_prefetch=1,
            grid=(bn,),
            in_specs=[pl.BlockSpec((1, n_slabs, s, LANES), lambda b, tb: (b, 0, 0, 0)),
                      pl.BlockSpec(buckets.shape, lambda b, tb: (0, 0, 0))],
            out_specs=pl.BlockSpec((1, s, LANES), lambda b, tb: (b, 0, 0)),
            scratch_shapes=[pltpu.VMEM((H_DIL, 2, BLK, 2 * BLK), F32),
                            pltpu.VMEM((n_groups, s, LANES), F32),
                            pltpu.VMEM((n_groups, s, LANES), F32),
                            pltpu.VMEM((n_groups, s, LANES), F32),
                            pltpu.VMEM((3, s, LANES), F32),
                            pltpu.VMEM((3, s, LANES), F32)],
        ),
        out_shape=jax.ShapeDtypeStruct((bn, s, LANES), BF16),
        compiler_params=pltpu.CompilerParams(dimension_semantics=("arbitrary",),
                                             vmem_limit_bytes=VMEM_LIMIT),
        name="dil_attn",
    )(table, qkv_dil, buckets)


def _swa_kernel(table_ref, sink_ref, qkv_ref, bucket_ref, o_ref, bias_ref):
    s_len = o_ref.shape[1]
    k_col = N_SWA_PAIRS * LANES
    v_col = k_col + LANES
    key_row = lax.broadcasted_iota(jnp.int32, (2 * BLK, LANES), 0)

    @pl.when(pl.program_id(0) == 0)
    def _():
        for slot, hd in enumerate(_SWA_HEAD_ORDER):
            normal, first = _band_bias_tiles(bucket_ref[0], table_ref, H_DIL + hd, SWA_WINDOW - 1, sink_ref[hd])
            bias_ref[slot, 0] = normal
            bias_ref[slot, 1] = first

    def scores(i):
        start = pl.multiple_of(i * BLK, BLK)
        prev = pl.multiple_of(jnp.maximum(i - 1, 0) * BLK, BLK)
        kcat = jnp.concatenate([qkv_ref[0, pl.ds(prev, BLK), k_col:k_col + LANES],
                                qkv_ref[0, pl.ds(start, BLK), k_col:k_col + LANES]], axis=0)
        vcat = jnp.concatenate([qkv_ref[0, pl.ds(prev, BLK), v_col:v_col + LANES],
                                qkv_ref[0, pl.ds(start, BLK), v_col:v_col + LANES]], axis=0)
        kcat = jnp.where(key_row == 0, jnp.zeros_like(kcat), kcat)
        vcat = jnp.where(key_row == 0, jnp.zeros_like(vcat), vcat)
        variant = jnp.where(i > 0, 0, 1)
        out = []
        for p in range(N_SWA_PAIRS):
            q = qkv_ref[0, pl.ds(start, BLK), p * LANES:(p + 1) * LANES]
            biases = [bias_ref[2 * p + h, variant] for h in range(2)]
            out.append((start, p, _band_scores(q, kcat, biases), vcat))
        return out

    def body(it, carry):
        staged1 = [unit for u in range(SWA_UNROLL) for unit in scores(it * SWA_UNROLL + u)]
        staged2 = [(start, p, _band_pv(s, vcat)) for start, p, s, vcat in staged1]
        for start, p, (outs, ms) in staged2:
            num, den, _ = _band_finish(outs, ms)
            o_ref[0, pl.ds(start, BLK), p * LANES:(p + 1) * LANES] = (num * (1.0 / den)).astype(o_ref.dtype)
        return carry

    lax.fori_loop(0, s_len // BLK // SWA_UNROLL, body, 0)


def _swa_attention(qkv_swa, bucket, table, sinks_l):
    bn, s, n = qkv_swa.shape
    return pl.pallas_call(
        _swa_kernel,
        grid_spec=pltpu.PrefetchScalarGridSpec(
            num_scalar_prefetch=2,
            grid=(bn,),
            in_specs=[pl.BlockSpec((1, s, n), lambda b, tb, sk: (b, 0, 0)),
                      pl.BlockSpec(bucket.shape, lambda b, tb, sk: (0, 0, 0))],
            out_specs=pl.BlockSpec((1, s, N_SWA_PAIRS * LANES), lambda b, tb, sk: (b, 0, 0)),
            scratch_shapes=[pltpu.VMEM((H_SWA_Q, 2, BLK, 2 * BLK), F32)],
        ),
        out_shape=jax.ShapeDtypeStruct((bn, s, N_SWA_PAIRS * LANES), BF16),
        compiler_params=pltpu.CompilerParams(dimension_semantics=("arbitrary",),
                                             vmem_limit_bytes=VMEM_LIMIT),
        name="swa_attn",
    )(table, sinks_l, qkv_swa, bucket)


def _merge_kernel(x_ref, h_ref, mod_ref, osb_ref, odil_ref, oswa_ref,
                  win_ref, wsb_ref, wdil_ref, wswa_ref, wout_ref, o_ref):
    x = x_ref[0]
    mod = mod_ref[0]
    d = x.shape[1]
    gate0 = win_ref.shape[1] - 3 * d
    h = h_ref[0]
    merged = None
    for t, (o_br, w_br) in enumerate(((osb_ref, wsb_ref), (odil_ref, wdil_ref), (oswa_ref, wswa_ref))):
        w_gate = win_ref[:, gate0 + t * d:gate0 + (t + 1) * d]
        gate = jax.nn.sigmoid(jnp.dot(h, w_gate, preferred_element_type=F32))
        term = gate * jnp.dot(o_br[0], w_br[...], preferred_element_type=F32)
        merged = term if merged is None else merged + term
    y = jnp.dot(merged.astype(BF16), wout_ref[...], preferred_element_type=F32)
    o_ref[0] = x + mod[5:6] * y


def _merge(x, h, mod, o_sb, o_dil, o_swa, w_in, w_sb, w_dil, w_swa, w_out, layer):
    bn, s, d = x.shape
    row = lambda n: pl.BlockSpec((1, ROW_TILE, n), lambda b, i: (b, i, 0))
    return pl.pallas_call(
        _merge_kernel,
        grid=(bn, s // ROW_TILE),
        in_specs=[row(d), row(d),
                  pl.BlockSpec((1, ADA_CHUNKS, d), lambda b, i: (b, 0, 0)),
                  row(o_sb.shape[2]), row(o_dil.shape[2]), row(o_swa.shape[2]),
                  _slab_spec(w_in, layer), _slab_spec(w_sb, layer), _slab_spec(w_dil, layer),
                  _const_spec(w_swa.shape), _slab_spec(w_out, layer)],
        out_specs=row(d),
        out_shape=jax.ShapeDtypeStruct(x.shape, F32),
        compiler_params=_cparams(2),
        name="merge",
    )(x, h, mod, o_sb, o_dil, o_swa, w_in, w_sb, w_dil, w_swa, w_out)


def _t5_bucket(n):
    max_exact = N_BUCKETS // 2
    nf = jnp.maximum(n, 1).astype(jnp.float32)
    large = max_exact + (jnp.log(nf / max_exact) / math.log(MAX_REL_DIST / max_exact)
                         * (N_BUCKETS - max_exact)).astype(jnp.int32)
    large = jnp.minimum(large, N_BUCKETS - 1)
    return jnp.where(n < max_exact, n, large)


def _band_buckets(dilation):
    rel = jnp.arange(BLK)[:, None] + BLK - jnp.arange(2 * BLK)[None, :]
    return _t5_bucket(jnp.maximum(rel, 0) * dilation).astype(jnp.int32)


def _reorder_heads(w, order, axis):
    parts = [lax.slice_in_dim(w, hd * HEAD_DIM, (hd + 1) * HEAD_DIM, axis=axis) for hd in order]
    return jnp.concatenate(parts, axis=axis)


def kernel(x, c, w_ada, b_ada, norm_gain, w_ffn_gate, w_ffn_up, w_ffn_down, w_in,
           w_br_sb, w_br_dil, w_br_swa, w_out, sinks, rel_bias, final_gain):
    depth = w_ada.shape[0]
    bn, s, d = x.shape
    mods = _ada(c, w_ada, b_ada).reshape(depth, bn, ADA_CHUNKS, d)

    buckets = jnp.stack([_band_buckets(dil) for _, dil in DIL_PATTERNS])
    table = rel_bias.reshape(-1)
    wg, wu, wd = w_ffn_gate, w_ffn_up, w_ffn_down
    w_in_bf, w_sb_bf, w_dil_bf, w_out_bf = (w_in.astype(BF16), w_br_sb.astype(BF16),
                                            w_br_dil.astype(BF16), w_out.astype(BF16))

    for l in range(depth):
        mod = mods[l]
        x = _ffn(x, mod, norm_gain[l, 0], wg, wu, wd, l, 0, 0)

        qkv_sb, qkv_dil, qkv_swa, h_mix = _inproj(x, mod, norm_gain[l, 1], w_in_bf, l)
        o_sb = _sb_attention(qkv_sb)
        o_dil = _dil_attention(qkv_dil, buckets, table)
        o_swa = _swa_attention(qkv_swa, buckets[:1], table, sinks[l])
        w_swa = _reorder_heads(w_br_swa[l], _SWA_HEAD_ORDER, 0).astype(BF16)
        x = _merge(x, h_mix, mod, o_sb, o_dil, o_swa, w_in_bf, w_sb_bf, w_dil_bf, w_swa, w_out_bf, l)

        last = l == depth - 1
        x = _ffn(x, mod, norm_gain[l, 2], wg, wu, wd, l, 1, 2, final_gain=final_gain if last else None)
    return x
```

```python
import functools
import math

import jax
import jax.numpy as jnp
from jax import lax
from jax.experimental import pallas as pl
from jax.experimental.pallas import tpu as pltpu

F32 = jnp.float32
BF16 = jnp.bfloat16

HEAD_DIM = 64
LANES = 128
BLK = 128
H_SB = 4
DIL_PATTERNS = ((128, 1), (512, 4), (2048, 16))
H_PER_DIL = 2
H_DIL = H_PER_DIL * len(DIL_PATTERNS)
H_SWA_Q = 6
H_SWA_KV = 2
SWA_WINDOW = 128
N_SOFT_HEADS = H_DIL + H_SWA_Q
N_SWA_PAIRS = H_SWA_Q // 2
N_BUCKETS = 32
MAX_REL_DIST = 2048
RMS_EPS = 1e-6
ADA_CHUNKS = 9
NEG_BIG = -1e30
LOG2E = 1.4426950408889634

VMEM_LIMIT = 56 * 1024 * 1024
ROW_TILE = 512
FFN_ROW_TILE = 512
FF_CHUNK = 256
ADA_TILE = 3072
SB_ROWS = 512
SB_KEY_BLOCKS = 2
SB_DEAD_LOG2 = -150.0
BAND_UNROLL = 16
MAX_STRIDE = 4
SWA_UNROLL = 8

_SWA_HEAD_ORDER = tuple(p + (H_SWA_Q // H_SWA_KV) * h for p in range(N_SWA_PAIRS) for h in range(2))


def _cparams(n_axes):
    return pltpu.CompilerParams(dimension_semantics=("parallel",) * n_axes,
                                vmem_limit_bytes=VMEM_LIMIT)


def _const_spec(shape):
    nd = len(shape)
    return pl.BlockSpec(shape, lambda *_: (0,) * nd, pipeline_mode=pl.Buffered(1))


def _slab_spec(arr, *lead):
    block = (None,) * len(lead) + tuple(arr.shape[len(lead):])
    return pl.BlockSpec(block, lambda *_: tuple(lead) + (0, 0), pipeline_mode=pl.Buffered(1))


def _ada_kernel(c_ref, w_ref, b_ref, o_ref):
    c = c_ref[...]
    sc = (c * jax.nn.sigmoid(c)).astype(BF16)
    o_ref[0] = jnp.dot(sc, w_ref[0].astype(BF16), preferred_element_type=F32) + b_ref[0]


def _ada(c, w_ada, b_ada):
    depth, d, n = w_ada.shape
    bn = c.shape[0]
    return pl.pallas_call(
        _ada_kernel,
        grid=(depth, n // ADA_TILE),
        in_specs=[pl.BlockSpec((bn, d), lambda l, j: (0, 0)),
                  pl.BlockSpec((1, d, ADA_TILE), lambda l, j: (l, 0, j)),
                  pl.BlockSpec((1, 1, ADA_TILE), lambda l, j: (l, 0, j))],
        out_specs=pl.BlockSpec((1, bn, ADA_TILE), lambda l, j: (l, 0, j)),
        out_shape=jax.ShapeDtypeStruct((depth, bn, n), F32),
        compiler_params=_cparams(2),
        name="ada_mod",
    )(c, w_ada, b_ada.reshape(depth, 1, n))


def _rms(x, gain):
    ms = jnp.mean(x * x, axis=-1, keepdims=True)
    return x * lax.rsqrt(ms + RMS_EPS) * gain


def _prenorm(x, gain, mod, j):
    shift = mod[3 * j:3 * j + 1]
    scale = mod[3 * j + 1:3 * j + 2]
    ms = jnp.mean(x * x, axis=-1, keepdims=True)
    return (x * lax.rsqrt(ms + RMS_EPS)) * (gain * (1.0 + scale)) + shift


def _dot_nt(a, b):
    return lax.dot_general(a, b, (((1,), (1,)), ((), ())), preferred_element_type=F32)


def _ffn_kernel(*refs, j, final, layer, which):
    n_in = 7 if final else 6
    x_ref, mod_ref, gain_ref, wg_hbm, wu_hbm, wd_hbm = refs[:6]
    fg_ref = refs[6] if final else None
    o_ref, wg_ref, wu_ref, wd_ref, stage_g_ref, stage_u_ref, stage_d_ref, sem_ref = refs[n_in:]
    n_chunks = wg_ref.shape[1] // FF_CHUNK
    chunk = lambda ci: slice(ci * FF_CHUNK, (ci + 1) * FF_CHUNK)
    first = (pl.program_id(0) == 0) & (pl.program_id(1) == 0)

    def chunk_copies(ci):
        slot = ci % 2
        return [pltpu.make_async_copy(wg_hbm.at[layer, which, :, chunk(ci)], stage_g_ref.at[slot], sem_ref.at[slot, 0]),
                pltpu.make_async_copy(wu_hbm.at[layer, which, :, chunk(ci)], stage_u_ref.at[slot], sem_ref.at[slot, 1]),
                pltpu.make_async_copy(wd_hbm.at[layer, which, chunk(ci), :], stage_d_ref.at[slot], sem_ref.at[slot, 2])]

    def compute(stream_weights):
        if stream_weights:
            for ci in range(min(2, n_chunks)):
                for cp in chunk_copies(ci):
                    cp.start()
        x = x_ref[0]
        mod = mod_ref[0]
        h = _prenorm(x, gain_ref[...], mod, j).astype(BF16)
        acc = jnp.zeros(x.shape, F32)
        for ci in range(n_chunks):
            if stream_weights:
                for cp in chunk_copies(ci):
                    cp.wait()
                wg_ref[:, chunk(ci)] = stage_g_ref[ci % 2].astype(BF16)
                wu_ref[:, chunk(ci)] = stage_u_ref[ci % 2].astype(BF16)
                wd_ref[chunk(ci), :] = stage_d_ref[ci % 2].astype(BF16)
                if ci + 2 < n_chunks:
                    for cp in chunk_copies(ci + 2):
                        cp.start()
            g = jnp.dot(h, wg_ref[:, chunk(ci)], preferred_element_type=F32)
            u = jnp.dot(h, wu_ref[:, chunk(ci)], preferred_element_type=F32)
            a = (g * jax.nn.sigmoid(g) * u).astype(BF16)
            acc = acc + jnp.dot(a, wd_ref[chunk(ci), :], preferred_element_type=F32)
        y = x + (0.5 * mod[3 * j + 2:3 * j + 3]) * acc
        if final:
            y = _rms(y, fg_ref[...])
        o_ref[0] = y

    @pl.when(first)
    def _():
        compute(True)

    @pl.when(jnp.logical_not(first))
    def _():
        compute(False)


def _ffn(x, mod, gain, wg, wu, wd, layer, which, j, final_gain=None):
    bn, s, d = x.shape
    d_ff = wg.shape[-1]
    final = final_gain is not None
    hbm = pl.BlockSpec(memory_space=pl.ANY)
    in_specs = [pl.BlockSpec((1, FFN_ROW_TILE, d), lambda b, i: (b, i, 0)),
                pl.BlockSpec((1, ADA_CHUNKS, d), lambda b, i: (b, 0, 0)),
                _const_spec((1, d)), hbm, hbm, hbm]
    args = [x, mod, gain.reshape(1, d), wg, wu, wd]
    if final:
        in_specs.append(_const_spec((1, d)))
        args.append(final_gain.reshape(1, d))
    return pl.pallas_call(
        functools.partial(_ffn_kernel, j=j, final=final, layer=layer, which=which),
        grid=(bn, s // FFN_ROW_TILE),
        in_specs=in_specs,
        out_specs=pl.BlockSpec((1, FFN_ROW_TILE, d), lambda b, i: (b, i, 0)),
        out_shape=jax.ShapeDtypeStruct(x.shape, F32),
        scratch_shapes=[pltpu.VMEM((d, d_ff), BF16), pltpu.VMEM((d, d_ff), BF16), pltpu.VMEM((d_ff, d), BF16),
                        pltpu.VMEM((2, d, FF_CHUNK), F32), pltpu.VMEM((2, d, FF_CHUNK), F32),
                        pltpu.VMEM((2, FF_CHUNK, d), F32), pltpu.SemaphoreType.DMA((2, 3))],
        compiler_params=pltpu.CompilerParams(dimension_semantics=("arbitrary", "arbitrary"),
                                             vmem_limit_bytes=VMEM_LIMIT),
        name="ffn_final" if final else f"ffn{j}",
    )(*args)


N_SB = 3 * H_SB * HEAD_DIM
N_DIL = 3 * H_DIL * HEAD_DIM
N_SWA = (H_SWA_Q + 2 * H_SWA_KV) * HEAD_DIM
N_QKV = N_SB + N_DIL + N_SWA
N_DIL_SLABS = N_DIL // LANES


W_COLS = 256
N_INPROJ_HEAD = 7 * W_COLS


def _column_streamer(w_hbm, first_col, n_chunks, stage_ref, sem_ref):
    def copy(c):
        return pltpu.make_async_copy(w_hbm.at[:, pl.ds(first_col + c * W_COLS, W_COLS)],
                                     stage_ref.at[c % 2], sem_ref.at[c % 2])

    def prime():
        for c in range(min(2, n_chunks)):
            copy(c).start()

    def take(c, consume):
        copy(c).wait()
        consume(stage_ref.at[c % 2])
        if c + 2 < n_chunks:
            copy(c + 2).start()

    return prime, take


def _inproj_tail_col(col):
    q0 = N_SB + N_DIL
    q1 = q0 + H_SWA_Q * HEAD_DIM
    if col < q0:
        return col - N_INPROJ_HEAD
    if col < q1:
        return (q0 - N_INPROJ_HEAD) + _SWA_HEAD_ORDER.index((col - q0) // HEAD_DIM) * HEAD_DIM
    return col - N_INPROJ_HEAD


def _inproj_kernel(x_ref, mod_ref, gain_ref, w_hbm, sb_ref, dil_ref, swa_ref, hout_ref,
                   whead_ref, wtail_ref, stage_ref, sem_ref, *, layer):
    qs = HEAD_DIM ** -0.5
    n_q_sb = H_SB * HEAD_DIM
    n_groups = len(DIL_PATTERNS)
    n_q_swa = H_SWA_Q * HEAD_DIM
    q0 = N_SB + N_DIL
    n_head_chunks = N_INPROJ_HEAD // W_COLS
    first = (pl.program_id(0) == 0) & (pl.program_id(1) == 0)

    def emit(h, r):
        hout_ref[0] = h
        sb_ref[0, :, :n_q_sb] = (r[:, :n_q_sb] * (qs * LOG2E)).astype(BF16)
        sb_ref[0, :, n_q_sb:] = r[:, n_q_sb:N_SB].astype(BF16)
        for a in range(3):
            for g in range(n_groups):
                c0 = N_SB + (a * n_groups + g) * LANES
                slab = r[:, c0:c0 + LANES]
                dil_ref[0, 3 * g + a] = slab * qs if a == 0 else slab
        swa_ref[0, :, :n_q_swa] = (r[:, q0:q0 + n_q_swa] * qs).astype(BF16)
        swa_ref[0, :, n_q_swa:] = r[:, q0 + n_q_swa:].astype(BF16)

    @pl.when(first)
    def _():
        prime, take = _column_streamer(w_hbm.at[layer], 0, N_QKV // W_COLS, stage_ref, sem_ref)
        prime()
        h = _prenorm(x_ref[0], gain_ref[...], mod_ref[0], 1).astype(BF16)
        parts = []
        for c in range(N_QKV // W_COLS):
            cols = slice(c * W_COLS, (c + 1) * W_COLS)
            if c < n_head_chunks:
                def consume(st, cols=cols):
                    whead_ref[:, cols] = st[...].astype(BF16)
                take(c, consume)
                parts.append(jnp.dot(h, whead_ref[:, cols], preferred_element_type=F32))
            else:
                def consume(st, c=c):
                    for piece in range(W_COLS // HEAD_DIM):
                        dst = _inproj_tail_col(c * W_COLS + piece * HEAD_DIM)
                        wtail_ref[:, dst:dst + HEAD_DIM] = \
                            st[:, piece * HEAD_DIM:(piece + 1) * HEAD_DIM].astype(BF16)
                take(c, consume)
        parts.append(jnp.dot(h, wtail_ref[...], preferred_element_type=F32))
        emit(h, jnp.concatenate(parts, axis=1))

    @pl.when(jnp.logical_not(first))
    def _():
        h = _prenorm(x_ref[0], gain_ref[...], mod_ref[0], 1).astype(BF16)
        emit(h, jnp.concatenate([jnp.dot(h, whead_ref[...], preferred_element_type=F32),
                                 jnp.dot(h, wtail_ref[...], preferred_element_type=F32)], axis=1))


def _inproj(x, mod, gain, w_in, layer):
    bn, s, d = x.shape
    nt = s // ROW_TILE
    n_tail = N_QKV - N_INPROJ_HEAD
    return pl.pallas_call(
        functools.partial(_inproj_kernel, layer=layer),
        grid=(bn, nt),
        in_specs=[pl.BlockSpec((1, ROW_TILE, d), lambda b, i: (b, i, 0)),
                  pl.BlockSpec((1, ADA_CHUNKS, d), lambda b, i: (b, 0, 0)),
                  _const_spec((1, d)), pl.BlockSpec(memory_space=pl.ANY)],
        out_specs=[pl.BlockSpec((1, ROW_TILE, N_SB), lambda b, i: (b, i, 0)),
                   pl.BlockSpec((1, N_DIL_SLABS, ROW_TILE, LANES), lambda b, i: (b, 0, i, 0)),
                   pl.BlockSpec((1, ROW_TILE, N_SWA), lambda b, i: (b, i, 0)),
                   pl.BlockSpec((1, ROW_TILE, d), lambda b, i: (b, i, 0))],
        out_shape=[jax.ShapeDtypeStruct((bn, s, N_SB), BF16),
                   jax.ShapeDtypeStruct((bn, N_DIL_SLABS, s, LANES), F32),
                   jax.ShapeDtypeStruct((bn, s, N_SWA), BF16),
                   jax.ShapeDtypeStruct((bn, s, d), BF16)],
        scratch_shapes=[pltpu.VMEM((d, N_INPROJ_HEAD), BF16), pltpu.VMEM((d, n_tail), BF16),
                        pltpu.VMEM((2, d, W_COLS), F32), pltpu.SemaphoreType.DMA((2,))],
        compiler_params=pltpu.CompilerParams(dimension_semantics=("arbitrary", "arbitrary"),
                                             vmem_limit_bytes=VMEM_LIMIT),
        name="in_proj",
    )(x, mod, gain.reshape(1, d), w_in)


def _sb_kernel(q_ref, k_ref, v_ref, o_ref, acc_ref, c_ref):
    s_len = q_ref.shape[1]
    n_pairs = q_ref.shape[2] // LANES
    per = SB_ROWS // BLK
    lane = lax.broadcasted_iota(jnp.int32, (BLK, LANES), 1)
    head0 = lane < HEAD_DIM
    row2 =lax.broadcasted_iota(jnp.int32, (BLK, 2 * BLK), 0)
    col2 = lax.broadcasted_iota(jnp.int32, (BLK, 2 * BLK), 1)
    strict2 = (col2 & (BLK - 1)) < row2
    rk = lax.broadcasted_iota(jnp.int32, (2 * BLK, 2 * BLK), 0)
    cn = lax.broadcasted_iota(jnp.int32, (2 * BLK, 2 * BLK), 1)
    neg_suffix_ones = jnp.where(((rk & (BLK - 1)) >= cn) | (cn >= BLK), -1.0, 0.0).astype(BF16)

    def by_head(t):
        zero = jnp.zeros_like(t)
        return jnp.concatenate([jnp.where(head0, t, zero), jnp.where(head0, zero, t)], axis=0)

    def mask_first_block(t):
        first = jnp.where(strict2, t[:BLK], 0.0)
        return first if t.shape[0] == BLK else jnp.concatenate([first, t[BLK:]], axis=0)

    def pair_lanes(p):
        return slice(p * LANES, (p + 1) * LANES)

    def scores(p, q, k0):
        return _dot_nt(q, by_head(k_ref[0, pl.ds(k0, BLK), pair_lanes(p)]))

    def carries(z, diag):
        neg_abs = lax.bitcast_convert_type(lax.bitcast_convert_type(z, jnp.int32) | jnp.int32(-2 ** 31), F32)
        softplus2 = jnp.maximum(z, 0.0) + jnp.log2(1.0 + jnp.exp2(neg_abs))
        if diag:
            softplus2 = mask_first_block(softplus2)
        hi = softplus2.astype(BF16)
        lo = (softplus2 - hi.astype(F32)).astype(BF16)
        return [jnp.dot(jnp.concatenate([hi[:, h * BLK:(h + 1) * BLK], lo[:, h * BLK:(h + 1) * BLK]], axis=1),
                        neg_suffix_ones, preferred_element_type=F32) for h in range(2)]

    def weighted_values(p, z, cts, c, k0, diag):
        ws = [jnp.exp2(z[:, h * BLK:(h + 1) * BLK] + cts[h][:, :BLK] + c[h]) for h in range(2)]
        w = jnp.concatenate(ws, axis=1)
        if diag:
            w = mask_first_block(w)
        vv = by_head(v_ref[0, pl.ds(k0, BLK), pair_lanes(p)])
        return jnp.dot(w.astype(BF16), vv, preferred_element_type=F32)

    def superblock(sb, carry):
        base = pl.multiple_of(sb * SB_ROWS, SB_ROWS)
        acc_ref[...] = jnp.zeros_like(acc_ref)
        c_ref[...] = jnp.zeros_like(c_ref)

        def run(units):
            zs = [scores(p, q_ref[0, pl.ds(base + lo, n), pair_lanes(p)], k0) for p, lo, n, k0, _ in units]
            cts = [carries(z, unit[4]) for z, unit in zip(zs, units)]
            for z, ct, (p, lo, n, k0, diag) in zip(zs, cts, units):
                rows = slice(lo, lo + n)
                c = (c_ref[p, 0, rows, :], c_ref[p, 1, rows, :])
                acc_ref[p, rows, :] += weighted_values(p, z, ct, c, k0, diag)
                c_ref[p, 0, rows, :] = c[0] + ct[0][:, BLK:]
                c_ref[p, 1, rows, :] = c[1] + ct[1][:, BLK:]

        def alive(lo, n):
            c_max = None
            for p in range(n_pairs):
                for h in range(2):
                    c = c_ref[p, h, lo:lo + n, :]
                    c_max = c if c_max is None else jnp.maximum(c_max, c)
            return jnp.max(c_max) > SB_DEAD_LOG2

        run([(p, kq * BLK, SB_ROWS - kq * BLK, base + kq * BLK, True)
             for kq in reversed(range(per)) for p in range(n_pairs)])

        n_iter = sb * (per // SB_KEY_BLOCKS)

        def sweep(window, watch, jj0):
            def kstep(state):
                jj, _ = state
                run([(p, window[0], window[1],
                      pl.multiple_of(base - (SB_KEY_BLOCKS * jj + u + 1) * BLK, BLK), False)
                     for u in range(SB_KEY_BLOCKS) for p in range(n_pairs)])
                return jj + 1, alive(*watch)

            jj, _ = lax.while_loop(lambda st: jnp.logical_and(st[0] < n_iter, st[1]), kstep,
                                   (jj0, alive(*watch)))
            return jj

        half = SB_ROWS // 2
        jj = sweep((0, SB_ROWS), (half, half), jnp.int32(0))
        sweep((0, half), (0, half), jj)
        for p in range(n_pairs):
            o_ref[0, pl.ds(base, SB_ROWS), pair_lanes(p)] = acc_ref[p].astype(o_ref.dtype)
        return carry

    lax.fori_loop(0, s_len // SB_ROWS, superblock, 0)


def _sb_attention(qkv_sb):
    bn, s, _ = qkv_sb.shape
    n_pairs = H_SB // 2
    width = n_pairs * LANES
    spec = lambda t: pl.BlockSpec((1, s, width), lambda b: (b, 0, t))
    return pl.pallas_call(
        _sb_kernel,
        grid=(bn,),
        in_specs=[spec(0), spec(1), spec(2)],
        out_specs=pl.BlockSpec((1, s, width), lambda b: (b, 0, 0)),
        out_shape=jax.ShapeDtypeStruct((bn, s, width), BF16),
        scratch_shapes=[pltpu.VMEM((n_pairs, SB_ROWS, LANES), F32),
                        pltpu.VMEM((n_pairs, 2, SB_ROWS, LANES), F32)],
        compiler_params=_cparams(1),
        name="sb_attn",
    )(qkv_sb, qkv_sb, qkv_sb)


def _band_bias_tiles(bucket, table_ref, col, max_dist, sink=None):
    row = lax.broadcasted_iota(jnp.int32, (BLK, 2 * BLK), 0)
    kcol = lax.broadcasted_iota(jnp.int32, (BLK, 2 * BLK), 1)
    rel = row + BLK - kcol
    in_band = (rel >= 0) & (rel <= max_dist)
    bias = jnp.zeros((BLK, 2 * BLK), F32)
    for n in range(N_BUCKETS):
        bias = jnp.where(bucket == n, table_ref[n * N_SOFT_HEADS + col], bias)
    tiles = [jnp.where(in_band, bias, NEG_BIG), jnp.where(in_band & (kcol >= BLK), bias, NEG_BIG)]
    if sink is not None:
        tiles = [jnp.where(kcol == 0, sink, t) for t in tiles]
    return tiles


def _band_pair(q, kcat, vcat, biases):
    return _band_finish(*_band_pv(_band_scores(q, kcat, biases), vcat))


def _band_scores(q, kcat, biases):
    head0 = lax.broadcasted_iota(jnp.int32, q.shape, 1) < HEAD_DIM
    zero = jnp.zeros_like(q)
    return [_dot_nt(jnp.where(head0, q, zero) if h == 0 else jnp.where(head0, zero, q), kcat) + biases[h]
            for h in range(2)]


def _band_pv(scores, vcat):
    one = jnp.ones_like(vcat)
    v_head0 = lax.broadcasted_iota(jnp.int32, vcat.shape, 1) < HEAD_DIM
    outs, ms = [], []
    for h, s in enumerate(scores):
        vh = jnp.where(v_head0, vcat, one) if h == 0 else jnp.where(v_head0, one, vcat)
        m = jnp.max(s, axis=-1, keepdims=True)
        p = jnp.exp(s - m)
        outs.append(jnp.dot(p.astype(BF16), vh, preferred_element_type=F32))
        ms.append(m)
    return outs, ms


def _band_finish(outs, ms):
    head0 = lax.broadcasted_iota(jnp.int32, outs[0].shape, 1) < HEAD_DIM
    num = jnp.where(head0, outs[0], outs[1])
    den = pltpu.roll(jnp.where(head0, outs[1], outs[0]), HEAD_DIM, axis=1)
    return num, den, jnp.where(head0, ms[0], ms[1])


def _dil_kernel(table_ref, qkv_ref, bucket_ref, o_ref, bias_ref, og_ref, dg_ref, mg_ref, stage_ref, ostage_ref):
    s_len = o_ref.shape[1]
    n_blocks = s_len // BLK

    @pl.when(pl.program_id(0) == 0)
    def _():
        for g, (window, d) in enumerate(DIL_PATTERNS):
            for h in range(H_PER_DIL):
                head = g * H_PER_DIL + h
                normal, first = _band_bias_tiles(bucket_ref[g], table_ref, head, window // d)
                bias_ref[head, 0] = normal
                bias_ref[head, 1] = first

    for g, (window, d) in enumerate(DIL_PATTERNS):
        nb = s_len // d // BLK
        two_pass = d > MAX_STRIDE
        d_in = d // MAX_STRIDE if two_pass else d
        seg = s_len // MAX_STRIDE

        if two_pass:
            def stage_in(cc, carry):
                c1, j = cc // (seg // BLK), cc % (seg // BLK)
                for a in range(3):
                    stage_ref[a, pl.ds(pl.multiple_of(c1 * seg + j * BLK, BLK), BLK), :] = \
                        qkv_ref[0, 3 * g + a, pl.ds(c1 + MAX_STRIDE * BLK * j, BLK, stride=MAX_STRIDE), :]
                return carry

            lax.fori_loop(0, n_blocks, stage_in, 0)

        def src(a, start):
            if d == 1:
                return qkv_ref[0, 3 * g + a, pl.ds(start, BLK), :]
            if not two_pass:
                return qkv_ref[0, 3 * g + a, pl.ds(start, BLK, stride=d), :]
            return stage_ref[a, pl.ds(staged(start), BLK, stride=d_in), :]

        def staged(start):
            return (start % MAX_STRIDE) * seg + start // MAX_STRIDE

        def put(dst_ref, ostage_slot, start, val):
            if d == 1:
                dst_ref[g, pl.ds(start, BLK), :] = val
            elif not two_pass:
                dst_ref[g, pl.ds(start, BLK, stride=d), :] = val
            else:
                ostage_ref[ostage_slot, pl.ds(staged(start), BLK, stride=d_in), :] = val

        def scores(idx):
            r, i = idx // nb, idx % nb
            start = r + d * BLK * i
            q = src(0, start).astype(BF16)
            heads = (g * H_PER_DIL, g * H_PER_DIL + 1)
            prev = r + d * BLK * jnp.maximum(i - 1, 0)
            kcat = jnp.concatenate([src(1, prev), src(1, start)], axis=0).astype(BF16)
            vcat = jnp.concatenate([src(2, prev), src(2, start)], axis=0).astype(BF16)
            variant = jnp.where(i > 0, 0, 1)
            biases = [bias_ref[hd, variant] for hd in heads]
            return start, _band_scores(q, kcat, biases), vcat

        def body(it, carry):
            staged1 = [scores(it * BAND_UNROLL + u) for u in range(BAND_UNROLL)]
            staged2 = [(start, _band_pv(s, vcat)) for start, s, vcat in staged1]
            for start, (outs, ms) in staged2:
                num, den, m = _band_finish(outs, ms)
                put(og_ref, 0, start, num)
                put(dg_ref, 1, start, den)
                put(mg_ref, 2, start, m)
            return carry

        lax.fori_loop(0, n_blocks // BAND_UNROLL, body, 0)

        if two_pass:
            def stage_out(cc, carry):
                c1, j = cc // (seg // BLK), cc % (seg // BLK)
                rows_in = pl.ds(pl.multiple_of(c1 * seg + j * BLK, BLK), BLK)
                rows_out = pl.ds(c1 + MAX_STRIDE * BLK * j, BLK, stride=MAX_STRIDE)
                og_ref[g, rows_out, :] = ostage_ref[0, rows_in, :]
                dg_ref[g, rows_out, :] = ostage_ref[1, rows_in, :]
                mg_ref[g, rows_out, :] = ostage_ref[2, rows_in, :]
                return carry

            lax.fori_loop(0, n_blocks, stage_out, 0)

    def merge(i, carry):
        sl = pl.ds(pl.multiple_of(i * BLK, BLK), BLK)
        m0, m1, m2 = mg_ref[0, sl, :], mg_ref[1, sl, :], mg_ref[2, sl, :]
        m = jnp.maximum(jnp.maximum(m0, m1), m2)
        e0, e1, e2 = jnp.exp(m0 - m), jnp.exp(m1 - m), jnp.exp(m2 - m)
        num = e0 * og_ref[0, sl, :] + e1 * og_ref[1, sl, :] + e2 * og_ref[2, sl, :]
        den = e0 * dg_ref[0, sl, :] + e1 * dg_ref[1, sl, :] + e2 * dg_ref[2, sl, :]
        o_ref[0, sl, :] = (num / den).astype(o_ref.dtype)
        return carry

    lax.fori_loop(0, n_blocks, merge, 0)


def _dil_attention(qkv_dil, buckets, table):
    bn, n_slabs, s, _ = qkv_dil.shape
    n_groups = len(DIL_PATTERNS)
    return pl.pallas_call(
        _dil_kernel,
        grid_spec=pltpu.PrefetchScalarGridSpec(
            num_scalar_prefetch=1,
            grid=(bn,),
            in_specs=[pl.BlockSpec((1, n_slabs, s, LANES), lambda b, tb: (b, 0, 0, 0)),
                      pl.BlockSpec(buckets.shape, lambda b, tb: (0, 0, 0))],
            out_specs=pl.BlockSpec((1, s, LANES), lambda b, tb: (b, 0, 0)),
            scratch_shapes=[pltpu.VMEM((H_DIL, 2, BLK, 2 * BLK), F32),
                            pltpu.VMEM((n_groups, s, LANES), F32),
                            pltpu.VMEM((n_groups, s, LANES), F32),
                            pltpu.VMEM((n_groups, s, LANES), F32),
                            pltpu.VMEM((3, s, LANES), F32),
                            pltpu.VMEM((3, s, LANES), F32)],
        ),
        out_shape=jax.ShapeDtypeStruct((bn, s, LANES), BF16),
        compiler_params=pltpu.CompilerParams(dimension_semantics=("arbitrary",),
                                             vmem_limit_bytes=VMEM_LIMIT),
        name="dil_attn",
    )(table, qkv_dil, buckets)


def _swa_kernel(table_ref, sink_ref, qkv_ref, bucket_ref, o_ref, bias_ref):
    s_len = o_ref.shape[1]
    k_col = N_SWA_PAIRS * LANES
    v_col = k_col + LANES
    key_row = lax.broadcasted_iota(jnp.int32, (2 * BLK, LANES), 0)

    @pl.when(pl.program_id(0) == 0)
    def _():
        for slot, hd in enumerate(_SWA_HEAD_ORDER):
            normal, first = _band_bias_tiles(bucket_ref[0], table_ref, H_DIL + hd, SWA_WINDOW - 1, sink_ref[hd])
            bias_ref[slot, 0] = normal
            bias_ref[slot, 1] = first

    def scores(i):
        start = pl.multiple_of(i * BLK, BLK)
        prev = pl.multiple_of(jnp.maximum(i - 1, 0) * BLK, BLK)
        kcat = jnp.concatenate([qkv_ref[0, pl.ds(prev, BLK), k_col:k_col + LANES],
                                qkv_ref[0, pl.ds(start, BLK), k_col:k_col + LANES]], axis=0)
        vcat = jnp.concatenate([qkv_ref[0, pl.ds(prev, BLK), v_col:v_col + LANES],
                                qkv_ref[0, pl.ds(start, BLK), v_col:v_col + LANES]], axis=0)
        kcat = jnp.where(key_row == 0, jnp.zeros_like(kcat), kcat)
        vcat = jnp.where(key_row == 0, jnp.zeros_like(vcat), vcat)
        variant = jnp.where(i > 0, 0, 1)
        out = []
        for p in range(N_SWA_PAIRS):
            q = qkv_ref[0, pl.ds(start, BLK), p * LANES:(p + 1) * LANES]
            biases = [bias_ref[2 * p + h, variant] for h in range(2)]
            out.append((start, p, _band_scores(q, kcat, biases), vcat))
        return out

    def body(it, carry):
        staged1 = [unit for u in range(SWA_UNROLL) for unit in scores(it * SWA_UNROLL + u)]
        staged2 = [(start, p, _band_pv(s, vcat)) for start, p, s, vcat in staged1]
        for start, p, (outs, ms) in staged2:
            num, den, _ = _band_finish(outs, ms)
            o_ref[0, pl.ds(start, BLK), p * LANES:(p + 1) * LANES] = (num * (1.0 / den)).astype(o_ref.dtype)
        return carry

    lax.fori_loop(0, s_len // BLK // SWA_UNROLL, body, 0)


def _swa_attention(qkv_swa, bucket, table, sinks_l):
    bn, s, n = qkv_swa.shape
    return pl.pallas_call(
        _swa_kernel,
        grid_spec=pltpu.PrefetchScalarGridSpec(
            num_scalar_prefetch=2,
            grid=(bn,),
            in_specs=[pl.BlockSpec((1, s, n), lambda b, tb, sk: (b, 0, 0)),
                      pl.BlockSpec(bucket.shape, lambda b, tb, sk: (0, 0, 0))],
            out_specs=pl.BlockSpec((1, s, N_SWA_PAIRS * LANES), lambda b, tb, sk: (b, 0, 0)),
            scratch_shapes=[pltpu.VMEM((H_SWA_Q, 2, BLK, 2 * BLK), F32)],
        ),
        out_shape=jax.ShapeDtypeStruct((bn, s, N_SWA_PAIRS * LANES), BF16),
        compiler_params=pltpu.CompilerParams(dimension_semantics=("arbitrary",),
                                             vmem_limit_bytes=VMEM_LIMIT),
        name="swa_attn",
    )(table, sinks_l, qkv_swa, bucket)


def _merge_kernel(x_ref, h_ref, mod_ref, osb_ref, odil_ref, oswa_ref,
                  win_hbm, wsb_ref, wdil_ref, wswa_ref, wout_ref, o_ref, wgate_ref, stage_ref, sem_ref, *, layer):
    d = x_ref.shape[2]
    gate0 = win_hbm.shape[2] - 3 * d
    chunks_per_gate = d // W_COLS
    branches = ((osb_ref, wsb_ref), (odil_ref, wdil_ref), (oswa_ref, wswa_ref))
    first = (pl.program_id(0) == 0) & (pl.program_id(1) == 0)

    def compute(stream_gates):
        if stream_gates:
            prime, take = _column_streamer(win_hbm.at[layer], gate0, 3 * chunks_per_gate, stage_ref, sem_ref)
            prime()
        x = x_ref[0]
        h = h_ref[0]
        projected = [jnp.dot(o_br[0], w_br[...], preferred_element_type=F32) for o_br, w_br in branches]
        merged = None
        for t in range(3):
            if stream_gates:
                for c in range(t * chunks_per_gate, (t + 1) * chunks_per_gate):
                    def consume(st, c=c):
                        wgate_ref[:, c * W_COLS:(c + 1) * W_COLS] = st[...].astype(BF16)
                    take(c, consume)
            gate = jax.nn.sigmoid(jnp.dot(h, wgate_ref[:, t * d:(t + 1) * d], preferred_element_type=F32))
            term = gate * projected[t]
            merged = term if merged is None else merged + term
        y = jnp.dot(merged.astype(BF16), wout_ref[...], preferred_element_type=F32)
        o_ref[0] = x + mod_ref[0][5:6] * y

    @pl.when(first)
    def _():
        compute(True)

    @pl.when(jnp.logical_not(first))
    def _():
        compute(False)


def _merge(x, h, mod, o_sb, o_dil, o_swa, w_in, w_sb, w_dil, w_swa, w_out, layer):
    bn, s, d = x.shape
    row = lambda n: pl.BlockSpec((1, ROW_TILE, n), lambda b, i: (b, i, 0))
    return pl.pallas_call(
        functools.partial(_merge_kernel, layer=layer),
        grid=(bn, s // ROW_TILE),
        in_specs=[row(d), row(d),
                  pl.BlockSpec((1, ADA_CHUNKS, d), lambda b, i: (b, 0, 0)),
                  row(o_sb.shape[2]), row(o_dil.shape[2]), row(o_swa.shape[2]),
                  pl.BlockSpec(memory_space=pl.ANY), _slab_spec(w_sb, layer), _slab_spec(w_dil, layer),
                  _const_spec(w_swa.shape), _slab_spec(w_out, layer)],
        out_specs=row(d),
        out_shape=jax.ShapeDtypeStruct(x.shape, F32),
        scratch_shapes=[pltpu.VMEM((d, 3 * d), BF16), pltpu.VMEM((2, d, W_COLS), F32),
                        pltpu.SemaphoreType.DMA((2,))],
        compiler_params=pltpu.CompilerParams(dimension_semantics=("arbitrary", "arbitrary"),
                                             vmem_limit_bytes=VMEM_LIMIT),
        name="merge",
    )(x, h, mod, o_sb, o_dil, o_swa, w_in, w_sb, w_dil, w_swa, w_out)


def _t5_bucket(n):
    max_exact = N_BUCKETS // 2
    nf = jnp.maximum(n, 1).astype(jnp.float32)
    large = max_exact + (jnp.log(nf / max_exact) / math.log(MAX_REL_DIST / max_exact)
                         * (N_BUCKETS - max_exact)).astype(jnp.int32)
    large = jnp.minimum(large, N_BUCKETS - 1)
    return jnp.where(n < max_exact, n, large)


def _band_buckets(dilation):
    rel = jnp.arange(BLK)[:, None] + BLK - jnp.arange(2 * BLK)[None, :]
    return _t5_bucket(jnp.maximum(rel, 0) * dilation).astype(jnp.int32)


def _reorder_heads(w, order, axis):
    parts = [lax.slice_in_dim(w, hd * HEAD_DIM, (hd + 1) * HEAD_DIM, axis=axis) for hd in order]
    return jnp.concatenate(parts, axis=axis)


def kernel(x, c, w_ada, b_ada, norm_gain, w_ffn_gate, w_ffn_up, w_ffn_down, w_in,
           w_br_sb, w_br_dil, w_br_swa, w_out, sinks, rel_bias, final_gain):
    depth = w_ada.shape[0]
    bn, s, d = x.shape
    mods = _ada(c, w_ada, b_ada).reshape(depth, bn, ADA_CHUNKS, d)

    buckets = jnp.stack([_band_buckets(dil) for _, dil in DIL_PATTERNS])
    table = rel_bias.reshape(-1)
    wg, wu, wd = w_ffn_gate, w_ffn_up, w_ffn_down
    w_sb_bf, w_dil_bf, w_out_bf = w_br_sb.astype(BF16), w_br_dil.astype(BF16), w_out.astype(BF16)

    for l in range(depth):
        mod = mods[l]
        x = _ffn(x, mod, norm_gain[l, 0], wg, wu, wd, l, 0, 0)

        qkv_sb, qkv_dil, qkv_swa, h_mix = _inproj(x, mod, norm_gain[l, 1], w_in, l)
        o_sb = _sb_attention(qkv_sb)
        o_dil = _dil_attention(qkv_dil, buckets, table)
        o_swa = _swa_attention(qkv_swa, buckets[:1], table, sinks[l])
        w_swa = _reorder_heads(w_br_swa[l], _SWA_HEAD_ORDER, 0).astype(BF16)
        x = _merge(x, h_mix, mod, o_sb, o_dil, o_swa, w_in, w_sb_bf, w_dil_bf, w_swa, w_out_bf, l)

        last = l == depth - 1
        x = _ffn(x, mod, norm_gain[l, 2], wg, wu, wd, l, 1, 2, final_gain=final_gain if last else None)
    return x
```

```python
import functools
import math

import jax
import jax.numpy as jnp
from jax import lax
from jax.experimental import pallas as pl
from jax.experimental.pallas import tpu as pltpu

F32 = jnp.float32
BF16 = jnp.bfloat16

HEAD_DIM = 64
LANES = 128
BLK = 128
H_SB = 4
DIL_PATTERNS = ((128, 1), (512, 4), (2048, 16))
H_PER_DIL = 2
H_DIL = H_PER_DIL * len(DIL_PATTERNS)
H_SWA_Q = 6
H_SWA_KV = 2
SWA_WINDOW = 128
N_SOFT_HEADS = H_DIL + H_SWA_Q
N_SWA_PAIRS = H_SWA_Q // 2
N_BUCKETS = 32
MAX_REL_DIST = 2048
RMS_EPS = 1e-6
ADA_CHUNKS = 9
NEG_BIG = -1e30
LOG2E = 1.4426950408889634

VMEM_LIMIT = 56 * 1024 * 1024
ROW_TILE = 512
FFN_ROW_TILE = 512
FF_CHUNK = 256
ADA_TILE = 3072
SB_ROWS = 512
SB_KEY_BLOCKS = 2
SB_DEAD_LOG2 = -150.0
BAND_UNROLL = 16
MAX_STRIDE = 4
SWA_UNROLL = 8

_SWA_HEAD_ORDER = tuple(p + (H_SWA_Q // H_SWA_KV) * h for p in range(N_SWA_PAIRS) for h in range(2))


def _cparams(n_axes):
    return pltpu.CompilerParams(dimension_semantics=("parallel",) * n_axes,
                                vmem_limit_bytes=VMEM_LIMIT)


def _const_spec(shape):
    nd = len(shape)
    return pl.BlockSpec(shape, lambda *_: (0,) * nd, pipeline_mode=pl.Buffered(1))


def _slab_spec(arr, *lead):
    block = (None,) * len(lead) + tuple(arr.shape[len(lead):])
    return pl.BlockSpec(block, lambda *_: tuple(lead) + (0, 0), pipeline_mode=pl.Buffered(1))


def _ada_kernel(c_ref, w_ref, b_ref, o_ref):
    c = c_ref[...]
    sc = (c * jax.nn.sigmoid(c)).astype(BF16)
    o_ref[0] = jnp.dot(sc, w_ref[0].astype(BF16), preferred_element_type=F32) + b_ref[0]


def _ada(c, w_ada, b_ada):
    depth, d, n = w_ada.shape
    bn = c.shape[0]
    return pl.pallas_call(
        _ada_kernel,
        grid=(depth, n // ADA_TILE),
        in_specs=[pl.BlockSpec((bn, d), lambda l, j: (0, 0)),
                  pl.BlockSpec((1, d, ADA_TILE), lambda l, j: (l, 0, j)),
                  pl.BlockSpec((1, 1, ADA_TILE), lambda l, j: (l, 0, j))],
        out_specs=pl.BlockSpec((1, bn, ADA_TILE), lambda l, j: (l, 0, j)),
        out_shape=jax.ShapeDtypeStruct((depth, bn, n), F32),
        compiler_params=_cparams(2),
        name="ada_mod",
    )(c, w_ada, b_ada.reshape(depth, 1, n))


def _rms(x, gain):
    ms = jnp.mean(x * x, axis=-1, keepdims=True)
    return x * lax.rsqrt(ms + RMS_EPS) * gain


def _prenorm(x, gain, mod, j):
    shift = mod[3 * j:3 * j + 1]
    scale = mod[3 * j + 1:3 * j + 2]
    ms = jnp.mean(x * x, axis=-1, keepdims=True)
    return (x * lax.rsqrt(ms + RMS_EPS)) * (gain * (1.0 + scale)) + shift


def _dot_nt(a, b):
    return lax.dot_general(a, b, (((1,), (1,)), ((), ())), preferred_element_type=F32)


def _ffn_kernel(*refs, j, final, layer, which):
    n_in = 7 if final else 6
    x_ref, mod_ref, gain_ref, wg_hbm, wu_hbm, wd_hbm = refs[:6]
    fg_ref = refs[6] if final else None
    o_ref, wg_ref, wu_ref, wd_ref, stage_g_ref, stage_u_ref, stage_d_ref, sem_ref = refs[n_in:]
    n_chunks = wg_ref.shape[1] // FF_CHUNK
    chunk = lambda ci: slice(ci * FF_CHUNK, (ci + 1) * FF_CHUNK)
    first = (pl.program_id(0) == 0) & (pl.program_id(1) == 0)

    def chunk_copies(ci):
        slot = ci % 2
        return [pltpu.make_async_copy(wg_hbm.at[layer, which, :, chunk(ci)], stage_g_ref.at[slot], sem_ref.at[slot, 0]),
                pltpu.make_async_copy(wu_hbm.at[layer, which, :, chunk(ci)], stage_u_ref.at[slot], sem_ref.at[slot, 1]),
                pltpu.make_async_copy(wd_hbm.at[layer, which, chunk(ci), :], stage_d_ref.at[slot], sem_ref.at[slot, 2])]

    def compute(stream_weights):
        if stream_weights:
            for ci in range(min(2, n_chunks)):
                for cp in chunk_copies(ci):
                    cp.start()
        x = x_ref[0]
        mod = mod_ref[0]
        h = _prenorm(x, gain_ref[...], mod, j).astype(BF16)
        acc = jnp.zeros(x.shape, F32)
        for ci in range(n_chunks):
            if stream_weights:
                for cp in chunk_copies(ci):
                    cp.wait()
                wg_ref[:, chunk(ci)] = stage_g_ref[ci % 2].astype(BF16)
                wu_ref[:, chunk(ci)] = stage_u_ref[ci % 2].astype(BF16)
                wd_ref[chunk(ci), :] = stage_d_ref[ci % 2].astype(BF16)
                if ci + 2 < n_chunks:
                    for cp in chunk_copies(ci + 2):
                        cp.start()
            g = jnp.dot(h, wg_ref[:, chunk(ci)], preferred_element_type=F32)
            u = jnp.dot(h, wu_ref[:, chunk(ci)], preferred_element_type=F32)
            a = (g * jax.nn.sigmoid(g) * u).astype(BF16)
            acc = acc + jnp.dot(a, wd_ref[chunk(ci), :], preferred_element_type=F32)
        y = x + (0.5 * mod[3 * j + 2:3 * j + 3]) * acc
        if final:
            y = _rms(y, fg_ref[...])
        o_ref[0] = y

    @pl.when(first)
    def _():
        compute(True)

    @pl.when(jnp.logical_not(first))
    def _():
        compute(False)


def _ffn(x, mod, gain, wg, wu, wd, layer, which, j, final_gain=None):
    bn, s, d = x.shape
    d_ff = wg.shape[-1]
    final = final_gain is not None
    hbm = pl.BlockSpec(memory_space=pl.ANY)
    in_specs = [pl.BlockSpec((1, FFN_ROW_TILE, d), lambda b, i: (b, i, 0)),
                pl.BlockSpec((1, ADA_CHUNKS, d), lambda b, i: (b, 0, 0)),
                _const_spec((1, d)), hbm, hbm, hbm]
    args = [x, mod, gain.reshape(1, d), wg, wu, wd]
    if final:
        in_specs.append(_const_spec((1, d)))
        args.append(final_gain.reshape(1, d))
    return pl.pallas_call(
        functools.partial(_ffn_kernel, j=j, final=final, layer=layer, which=which),
        grid=(bn, s // FFN_ROW_TILE),
        in_specs=in_specs,
        out_specs=pl.BlockSpec((1, FFN_ROW_TILE, d), lambda b, i: (b, i, 0)),
        out_shape=jax.ShapeDtypeStruct(x.shape, F32),
        scratch_shapes=[pltpu.VMEM((d, d_ff), BF16), pltpu.VMEM((d, d_ff), BF16), pltpu.VMEM((d_ff, d), BF16),
                        pltpu.VMEM((2, d, FF_CHUNK), F32), pltpu.VMEM((2, d, FF_CHUNK), F32),
                        pltpu.VMEM((2, FF_CHUNK, d), F32), pltpu.SemaphoreType.DMA((2, 3))],
        compiler_params=pltpu.CompilerParams(dimension_semantics=("arbitrary", "arbitrary"),
                                             vmem_limit_bytes=VMEM_LIMIT),
        name="ffn_final" if final else f"ffn{j}",
    )(*args)


N_SB = 3 * H_SB * HEAD_DIM
N_DIL = 3 * H_DIL * HEAD_DIM
N_SWA = (H_SWA_Q + 2 * H_SWA_KV) * HEAD_DIM
N_QKV = N_SB + N_DIL + N_SWA
N_DIL_SLABS = N_DIL // LANES


W_COLS = 256
N_INPROJ_HEAD = 7 * W_COLS


def _column_streamer(w_hbm, first_col, n_chunks, stage_ref, sem_ref):
    def copy(c):
        return pltpu.make_async_copy(w_hbm.at[:, pl.ds(first_col + c * W_COLS, W_COLS)],
                                     stage_ref.at[c % 2], sem_ref.at[c % 2])

    def prime():
        for c in range(min(2, n_chunks)):
            copy(c).start()

    def take(c, consume):
        copy(c).wait()
        consume(stage_ref.at[c % 2])
        if c + 2 < n_chunks:
            copy(c + 2).start()

    return prime, take


def _inproj_tail_col(col):
    q0 = N_SB + N_DIL
    q1 = q0 + H_SWA_Q * HEAD_DIM
    if col < q0:
        return col - N_INPROJ_HEAD
    if col < q1:
        return (q0 - N_INPROJ_HEAD) + _SWA_HEAD_ORDER.index((col - q0) // HEAD_DIM) * HEAD_DIM
    return col - N_INPROJ_HEAD


def _inproj_kernel(x_ref, mod_ref, gain_ref, w_hbm, sb_ref, dil_ref, swa_ref, hout_ref,
                   whead_ref, wtail_ref, stage_ref, sem_ref, *, layer):
    qs = HEAD_DIM ** -0.5
    n_q_sb = H_SB * HEAD_DIM
    n_groups = len(DIL_PATTERNS)
    n_q_swa = H_SWA_Q * HEAD_DIM
    q0 = N_SB + N_DIL
    n_head_chunks = N_INPROJ_HEAD // W_COLS
    first = (pl.program_id(0) == 0) & (pl.program_id(1) == 0)

    def emit(h, r):
        hout_ref[0] = h
        sb_ref[0, :, :n_q_sb] = (r[:, :n_q_sb] * (qs * LOG2E)).astype(BF16)
        sb_ref[0, :, n_q_sb:] = r[:, n_q_sb:N_SB].astype(BF16)
        for a in range(3):
            for g in range(n_groups):
                c0 = N_SB + (a * n_groups + g) * LANES
                slab = r[:, c0:c0 + LANES]
                dil_ref[0, 3 * g + a] = slab * qs if a == 0 else slab
        swa_ref[0, :, :n_q_swa] = (r[:, q0:q0 + n_q_swa] * qs).astype(BF16)
        swa_ref[0, :, n_q_swa:] = r[:, q0 + n_q_swa:].astype(BF16)

    @pl.when(first)
    def _():
        prime, take = _column_streamer(w_hbm.at[layer], 0, N_QKV // W_COLS, stage_ref, sem_ref)
        prime()
        h = _prenorm(x_ref[0], gain_ref[...], mod_ref[0], 1).astype(BF16)
        parts = []
        for c in range(N_QKV // W_COLS):
            cols = slice(c * W_COLS, (c + 1) * W_COLS)
            if c < n_head_chunks:
                def consume(st, cols=cols):
                    whead_ref[:, cols] = st[...].astype(BF16)
                take(c, consume)
                parts.append(jnp.dot(h, whead_ref[:, cols], preferred_element_type=F32))
            else:
                def consume(st, c=c):
                    for piece in range(W_COLS // HEAD_DIM):
                        dst = _inproj_tail_col(c * W_COLS + piece * HEAD_DIM)
                        wtail_ref[:, dst:dst + HEAD_DIM] = \
                            st[:, piece * HEAD_DIM:(piece + 1) * HEAD_DIM].astype(BF16)
                take(c, consume)
        parts.append(jnp.dot(h, wtail_ref[...], preferred_element_type=F32))
        emit(h, jnp.concatenate(parts, axis=1))

    @pl.when(jnp.logical_not(first))
    def _():
        h = _prenorm(x_ref[0], gain_ref[...], mod_ref[0], 1).astype(BF16)
        emit(h, jnp.concatenate([jnp.dot(h, whead_ref[...], preferred_element_type=F32),
                                 jnp.dot(h, wtail_ref[...], preferred_element_type=F32)], axis=1))


def _inproj(x, mod, gain, w_in, layer):
    bn, s, d = x.shape
    nt = s // ROW_TILE
    n_tail = N_QKV - N_INPROJ_HEAD
    return pl.pallas_call(
        functools.partial(_inproj_kernel, layer=layer),
        grid=(bn, nt),
        in_specs=[pl.BlockSpec((1, ROW_TILE, d), lambda b, i: (b, i, 0)),
                  pl.BlockSpec((1, ADA_CHUNKS, d), lambda b, i: (b, 0, 0)),
                  _const_spec((1, d)), pl.BlockSpec(memory_space=pl.ANY)],
        out_specs=[pl.BlockSpec((1, ROW_TILE, N_SB), lambda b, i: (b, i, 0)),
                   pl.BlockSpec((1, N_DIL_SLABS, ROW_TILE, LANES), lambda b, i: (b, 0, i, 0)),
                   pl.BlockSpec((1, ROW_TILE, N_SWA), lambda b, i: (b, i, 0)),
                   pl.BlockSpec((1, ROW_TILE, d), lambda b, i: (b, i, 0))],
        out_shape=[jax.ShapeDtypeStruct((bn, s, N_SB), BF16),
                   jax.ShapeDtypeStruct((bn, N_DIL_SLABS, s, LANES), F32),
                   jax.ShapeDtypeStruct((bn, s, N_SWA), BF16),
                   jax.ShapeDtypeStruct((bn, s, d), BF16)],
        scratch_shapes=[pltpu.VMEM((d, N_INPROJ_HEAD), BF16), pltpu.VMEM((d, n_tail), BF16),
                        pltpu.VMEM((2, d, W_COLS), F32), pltpu.SemaphoreType.DMA((2,))],
        compiler_params=pltpu.CompilerParams(dimension_semantics=("arbitrary", "arbitrary"),
                                             vmem_limit_bytes=VMEM_LIMIT),
        name="in_proj",
    )(x, mod, gain.reshape(1, d), w_in)


def _sb_kernel(q_ref, k_ref, v_ref, o_ref, acc_ref, c_ref):
    s_len = q_ref.shape[1]
    n_pairs = q_ref.shape[2] // LANES
    per = SB_ROWS // BLK
    lane = lax.broadcasted_iota(jnp.int32, (BLK, LANES), 1)
    head0 = lane < HEAD_DIM
    row2 =lax.broadcasted_iota(jnp.int32, (BLK, 2 * BLK), 0)
    col2 = lax.broadcasted_iota(jnp.int32, (BLK, 2 * BLK), 1)
    strict2 = (col2 & (BLK - 1)) < row2
    rk = lax.broadcasted_iota(jnp.int32, (2 * BLK, 2 * BLK), 0)
    cn = lax.broadcasted_iota(jnp.int32, (2 * BLK, 2 * BLK), 1)
    neg_suffix_ones = jnp.where(((rk & (BLK - 1)) >= cn) | (cn >= BLK), -1.0, 0.0).astype(BF16)

    def by_head(t):
        zero = jnp.zeros_like(t)
        return jnp.concatenate([jnp.where(head0, t, zero), jnp.where(head0, zero, t)], axis=0)

    def mask_first_block(t):
        first = jnp.where(strict2, t[:BLK], 0.0)
        return first if t.shape[0] == BLK else jnp.concatenate([first, t[BLK:]], axis=0)

    def pair_lanes(p):
        return slice(p * LANES, (p + 1) * LANES)

    def scores(p, q, k0):
        return _dot_nt(q, by_head(k_ref[0, pl.ds(k0, BLK), pair_lanes(p)]))

    def carries(z, diag):
        neg_abs = lax.bitcast_convert_type(lax.bitcast_convert_type(z, jnp.int32) | jnp.int32(-2 ** 31), F32)
        softplus2 = jnp.maximum(z, 0.0) + jnp.log2(1.0 + jnp.exp2(neg_abs))
        if diag:
            softplus2 = mask_first_block(softplus2)
        hi = softplus2.astype(BF16)
        lo = (softplus2 - hi.astype(F32)).astype(BF16)
        return [jnp.dot(jnp.concatenate([hi[:, h * BLK:(h + 1) * BLK], lo[:, h * BLK:(h + 1) * BLK]], axis=1),
                        neg_suffix_ones, preferred_element_type=F32) for h in range(2)]

    def weighted_values(p, z, cts, c, k0, diag):
        ws = [jnp.exp2(z[:, h * BLK:(h + 1) * BLK] + cts[h][:, :BLK] + c[h]) for h in range(2)]
        w = jnp.concatenate(ws, axis=1)
        if diag:
            w = mask_first_block(w)
        vv = by_head(v_ref[0, pl.ds(k0, BLK), pair_lanes(p)])
        return jnp.dot(w.astype(BF16), vv, preferred_element_type=F32)

    def superblock(sb, carry):
        base = pl.multiple_of(sb * SB_ROWS, SB_ROWS)
        acc_ref[...] = jnp.zeros_like(acc_ref)
        c_ref[...] = jnp.zeros_like(c_ref)

        def run(units):
            zs = [scores(p, q_ref[0, pl.ds(base + lo, n), pair_lanes(p)], k0) for p, lo, n, k0, _ in units]
            cts = [carries(z, unit[4]) for z, unit in zip(zs, units)]
            for z, ct, (p, lo, n, k0, diag) in zip(zs, cts, units):
                rows = slice(lo, lo + n)
                c = (c_ref[p, 0, rows, :], c_ref[p, 1, rows, :])
                acc_ref[p, rows, :] += weighted_values(p, z, ct, c, k0, diag)
                c_ref[p, 0, rows, :] = c[0] + ct[0][:, BLK:]
                c_ref[p, 1, rows, :] = c[1] + ct[1][:, BLK:]

        def alive(lo, n):
            c_max = None
            for p in range(n_pairs):
                for h in range(2):
                    c = c_ref[p, h, lo:lo + n, :]
                    c_max = c if c_max is None else jnp.maximum(c_max, c)
            return jnp.max(c_max) > SB_DEAD_LOG2

        run([(p, kq * BLK, SB_ROWS - kq * BLK, base + kq * BLK, True)
             for kq in reversed(range(per)) for p in range(n_pairs)])

        n_iter = sb * (per // SB_KEY_BLOCKS)

        def sweep(window, watch, jj0):
            def kstep(state):
                jj, _ = state
                run([(p, window[0], window[1],
                      pl.multiple_of(base - (SB_KEY_BLOCKS * jj + u + 1) * BLK, BLK), False)
                     for u in range(SB_KEY_BLOCKS) for p in range(n_pairs)])
                return jj + 1, alive(*watch)

            jj, _ = lax.while_loop(lambda st: jnp.logical_and(st[0] < n_iter, st[1]), kstep,
                                   (jj0, alive(*watch)))
            return jj

        half = SB_ROWS // 2
        jj = sweep((0, SB_ROWS), (half, half), jnp.int32(0))
        sweep((0, half), (0, half), jj)
        for p in range(n_pairs):
            o_ref[0, pl.ds(base, SB_ROWS), pair_lanes(p)] = acc_ref[p].astype(o_ref.dtype)
        return carry

    lax.fori_loop(0, s_len // SB_ROWS, superblock, 0)


def _sb_attention(qkv_sb):
    bn, s, _ = qkv_sb.shape
    n_pairs = H_SB // 2
    width = n_pairs * LANES
    spec = lambda t: pl.BlockSpec((1, s, width), lambda b: (b, 0, t))
    return pl.pallas_call(
        _sb_kernel,
        grid=(bn,),
        in_specs=[spec(0), spec(1), spec(2)],
        out_specs=pl.BlockSpec((1, s, width), lambda b: (b, 0, 0)),
        out_shape=jax.ShapeDtypeStruct((bn, s, width), BF16),
        scratch_shapes=[pltpu.VMEM((n_pairs, SB_ROWS, LANES), F32),
                        pltpu.VMEM((n_pairs, 2, SB_ROWS, LANES), F32)],
        compiler_params=_cparams(1),
        name="sb_attn",
    )(qkv_sb, qkv_sb, qkv_sb)


def _band_bias_tiles(bucket, table_ref, col, max_dist, sink=None):
    row = lax.broadcasted_iota(jnp.int32, (BLK, 2 * BLK), 0)
    kcol = lax.broadcasted_iota(jnp.int32, (BLK, 2 * BLK), 1)
    rel = row + BLK - kcol
    in_band = (rel >= 0) & (rel <= max_dist)
    bias = jnp.zeros((BLK, 2 * BLK), F32)
    for n in range(N_BUCKETS):
        bias = jnp.where(bucket == n, table_ref[n * N_SOFT_HEADS + col], bias)
    tiles = [jnp.where(in_band, bias, NEG_BIG), jnp.where(in_band & (kcol >= BLK), bias, NEG_BIG)]
    if sink is not None:
        tiles = [jnp.where(kcol == 0, sink, t) for t in tiles]
    return tiles


def _band_pair(q, kcat, vcat, biases):
    return _band_finish(*_band_pv(_band_scores(q, kcat, biases), vcat))


def _band_scores(q, kcat, biases):
    head0 = lax.broadcasted_iota(jnp.int32, q.shape, 1) < HEAD_DIM
    zero = jnp.zeros_like(q)
    return [_dot_nt(jnp.where(head0, q, zero) if h == 0 else jnp.where(head0, zero, q), kcat) + biases[h]
            for h in range(2)]


def _band_pv(scores, vcat):
    one = jnp.ones_like(vcat)
    v_head0 = lax.broadcasted_iota(jnp.int32, vcat.shape, 1) < HEAD_DIM
    outs, ms = [], []
    for h, s in enumerate(scores):
        vh = jnp.where(v_head0, vcat, one) if h == 0 else jnp.where(v_head0, one, vcat)
        m = jnp.max(s, axis=-1, keepdims=True)
        p = jnp.exp(s - m)
        outs.append(jnp.dot(p.astype(BF16), vh, preferred_element_type=F32))
        ms.append(m)
    return outs, ms


def _band_finish(outs, ms):
    head0 = lax.broadcasted_iota(jnp.int32, outs[0].shape, 1) < HEAD_DIM
    num = jnp.where(head0, outs[0], outs[1])
    den = pltpu.roll(jnp.where(head0, outs[1], outs[0]), HEAD_DIM, axis=1)
    return num, den, jnp.where(head0, ms[0], ms[1])


def _dil_kernel(table_ref, qkv_ref, bucket_ref, o_ref, bias_ref, og_ref, dg_ref, mg_ref, stage_ref, ostage_ref):
    s_len = o_ref.shape[1]
    n_blocks = s_len // BLK

    @pl.when(pl.program_id(0) == 0)
    def _():
        for g, (window, d) in enumerate(DIL_PATTERNS):
            for h in range(H_PER_DIL):
                head = g * H_PER_DIL + h
                normal, first = _band_bias_tiles(bucket_ref[g], table_ref, head, window // d)
                bias_ref[head, 0] = normal
                bias_ref[head, 1] = first

    for g, (window, d) in enumerate(DIL_PATTERNS):
        nb = s_len // d // BLK
        two_pass = d > MAX_STRIDE
        d_in = d // MAX_STRIDE if two_pass else d
        seg = s_len // MAX_STRIDE

        if two_pass:
            def stage_in(cc, carry):
                c1, j = cc // (seg // BLK), cc % (seg // BLK)
                for a in range(3):
                    stage_ref[a, pl.ds(pl.multiple_of(c1 * seg + j * BLK, BLK), BLK), :] = \
                        qkv_ref[0, 3 * g + a, pl.ds(c1 + MAX_STRIDE * BLK * j, BLK, stride=MAX_STRIDE), :]
                return carry

            lax.fori_loop(0, n_blocks, stage_in, 0)

        def src(a, start):
            if d == 1:
                return qkv_ref[0, 3 * g + a, pl.ds(start, BLK), :]
            if not two_pass:
                return qkv_ref[0, 3 * g + a, pl.ds(start, BLK, stride=d), :]
            return stage_ref[a, pl.ds(staged(start), BLK, stride=d_in), :]

        def staged(start):
            return (start % MAX_STRIDE) * seg + start // MAX_STRIDE

        def put(dst_ref, ostage_slot, start, val):
            if d == 1:
                dst_ref[g, pl.ds(start, BLK), :] = val
            elif not two_pass:
                dst_ref[g, pl.ds(start, BLK, stride=d), :] = val
            else:
                ostage_ref[ostage_slot, pl.ds(staged(start), BLK, stride=d_in), :] = val

        def scores(idx):
            r, i = idx // nb, idx % nb
            start = r + d * BLK * i
            q = src(0, start).astype(BF16)
            heads = (g * H_PER_DIL, g * H_PER_DIL + 1)
            prev = r + d * BLK * jnp.maximum(i - 1, 0)
            kcat = jnp.concatenate([src(1, prev), src(1, start)], axis=0).astype(BF16)
            vcat = jnp.concatenate([src(2, prev), src(2, start)], axis=0).astype(BF16)
            variant = jnp.where(i > 0, 0, 1)
            biases = [bias_ref[hd, variant] for hd in heads]
            return start, _band_scores(q, kcat, biases), vcat

        def body(it, carry):
            staged1 = [scores(it * BAND_UNROLL + u) for u in range(BAND_UNROLL)]
            staged2 = [(start, _band_pv(s, vcat)) for start, s, vcat in staged1]
            for start, (outs, ms) in staged2:
                num, den, m = _band_finish(outs, ms)
                put(og_ref, 0, start, num)
                put(dg_ref, 1, start, den)
                put(mg_ref, 2, start, m)
            return carry

        lax.fori_loop(0, n_blocks // BAND_UNROLL, body, 0)

        if two_pass:
            def stage_out(cc, carry):
                c1, j = cc // (seg // BLK), cc % (seg // BLK)
                rows_in = pl.ds(pl.multiple_of(c1 * seg + j * BLK, BLK), BLK)
                rows_out = pl.ds(c1 + MAX_STRIDE * BLK * j, BLK, stride=MAX_STRIDE)
                og_ref[g, rows_out, :] = ostage_ref[0, rows_in, :]
                dg_ref[g, rows_out, :] = ostage_ref[1, rows_in, :]
                mg_ref[g, rows_out, :] = ostage_ref[2, rows_in, :]
                return carry

            lax.fori_loop(0, n_blocks, stage_out, 0)

    def merge(i, carry):
        sl = pl.ds(pl.multiple_of(i * BLK, BLK), BLK)
        m0, m1, m2 = mg_ref[0, sl, :], mg_ref[1, sl, :], mg_ref[2, sl, :]
        m = jnp.maximum(jnp.maximum(m0, m1), m2)
        e0, e1, e2 = jnp.exp(m0 - m), jnp.exp(m1 - m), jnp.exp(m2 - m)
        num = e0 * og_ref[0, sl, :] + e1 * og_ref[1, sl, :] + e2 * og_ref[2, sl, :]
        den = e0 * dg_ref[0, sl, :] + e1 * dg_ref[1, sl, :] + e2 * dg_ref[2, sl, :]
        o_ref[0, sl, :] = (num / den).astype(o_ref.dtype)
        return carry

    lax.fori_loop(0, n_blocks, merge, 0)


def _dil_attention(qkv_dil, buckets, table):
    bn, n_slabs, s, _ = qkv_dil.shape
    n_groups = len(DIL_PATTERNS)
    return pl.pallas_call(
        _dil_kernel,
        grid_spec=pltpu.PrefetchScalarGridSpec(
            num_scalar_prefetch=1,
            grid=(bn,),
            in_specs=[pl.BlockSpec((1, n_slabs, s, LANES), lambda b, tb: (b, 0, 0, 0)),
                      pl.BlockSpec(buckets.shape, lambda b, tb: (0, 0, 0))],
            out_specs=pl.BlockSpec((1, s, LANES), lambda b, tb: (b, 0, 0)),
            scratch_shapes=[pltpu.VMEM((H_DIL, 2, BLK, 2 * BLK), F32),
                            pltpu.VMEM((n_groups, s, LANES), F32),
                            pltpu.VMEM((n_groups, s, LANES), F32),
                            pltpu.VMEM((n_groups, s, LANES), F32),
                            pltpu.VMEM((3, s, LANES), F32),
                            pltpu.VMEM((3, s, LANES), F32)],
        ),
        out_shape=jax.ShapeDtypeStruct((bn, s, LANES), BF16),
        compiler_params=pltpu.CompilerParams(dimension_semantics=("arbitrary",),
                                             vmem_limit_bytes=VMEM_LIMIT),
        name="dil_attn",
    )(table, qkv_dil, buckets)


def _swa_kernel(table_ref, sink_ref, qkv_ref, bucket_ref, o_ref, bias_ref):
    s_len = o_ref.shape[1]
    k_col = N_SWA_PAIRS * LANES
    v_col = k_col + LANES
    key_row = lax.broadcasted_iota(jnp.int32, (2 * BLK, LANES), 0)

    @pl.when(pl.program_id(0) == 0)
    def _():
        for slot, hd in enumerate(_SWA_HEAD_ORDER):
            normal, first = _band_bias_tiles(bucket_ref[0], table_ref, H_DIL + hd, SWA_WINDOW - 1, sink_ref[hd])
            bias_ref[slot, 0] = normal
            bias_ref[slot, 1] = first

    def scores(i):
        start = pl.multiple_of(i * BLK, BLK)
        prev = pl.multiple_of(jnp.maximum(i - 1, 0) * BLK, BLK)
        kcat = jnp.concatenate([qkv_ref[0, pl.ds(prev, BLK), k_col:k_col + LANES],
                                qkv_ref[0, pl.ds(start, BLK), k_col:k_col + LANES]], axis=0)
        vcat = jnp.concatenate([qkv_ref[0, pl.ds(prev, BLK), v_col:v_col + LANES],
                                qkv_ref[0, pl.ds(start, BLK), v_col:v_col + LANES]], axis=0)
        kcat = jnp.where(key_row == 0, jnp.zeros_like(kcat), kcat)
        vcat = jnp.where(key_row == 0, jnp.zeros_like(vcat), vcat)
        variant = jnp.where(i > 0, 0, 1)
        out = []
        for p in range(N_SWA_PAIRS):
            q = qkv_ref[0, pl.ds(start, BLK), p * LANES:(p + 1) * LANES]
            biases = [bias_ref[2 * p + h, variant] for h in range(2)]
            out.append((start, p, _band_scores(q, kcat, biases), vcat))
        return out

    def body(it, carry):
        staged1 = [unit for u in range(SWA_UNROLL) for unit in scores(it * SWA_UNROLL + u)]
        staged2 = [(start, p, _band_pv(s, vcat)) for start, p, s, vcat in staged1]
        for start, p, (outs, ms) in staged2:
            num, den, _ = _band_finish(outs, ms)
            o_ref[0, pl.ds(start, BLK), p * LANES:(p + 1) * LANES] = (num * (1.0 / den)).astype(o_ref.dtype)
        return carry

    lax.fori_loop(0, s_len // BLK // SWA_UNROLL, body, 0)


def _swa_attention(qkv_swa, bucket, table, sinks_l):
    bn, s, n = qkv_swa.shape
    return pl.pallas_call(
        _swa_kernel,
        grid_spec=pltpu.PrefetchScalarGridSpec(
            num_scalar_prefetch=2,
            grid=(bn,),
            in_specs=[pl.BlockSpec((1, s, n), lambda b, tb, sk: (b, 0, 0)),
                      pl.BlockSpec(bucket.shape, lambda b, tb, sk: (0, 0, 0))],
            out_specs=pl.BlockSpec((1, s, N_SWA_PAIRS * LANES), lambda b, tb, sk: (b, 0, 0)),
            scratch_shapes=[pltpu.VMEM((H_SWA_Q, 2, BLK, 2 * BLK), F32)],
        ),
        out_shape=jax.ShapeDtypeStruct((bn, s, N_SWA_PAIRS * LANES), BF16),
        compiler_params=pltpu.CompilerParams(dimension_semantics=("arbitrary",),
                                             vmem_limit_bytes=VMEM_LIMIT),
        name="swa_attn",
    )(table, sinks_l, qkv_swa, bucket)


def _merge_kernel(x_ref, h_ref, mod_ref, osb_ref, odil_ref, oswa_ref,
                  win_hbm, wsb_ref, wdil_ref, wswa_ref, wout_ref, o_ref, wgate_ref, stage_ref, sem_ref, *, layer):
    d = x_ref.shape[2]
    gate0 = win_hbm.shape[2] - 3 * d
    chunks_per_gate = d // W_COLS
    branches = ((osb_ref, wsb_ref), (odil_ref, wdil_ref), (oswa_ref, wswa_ref))
    first = (pl.program_id(0) == 0) & (pl.program_id(1) == 0)

    def compute(stream_gates):
        if stream_gates:
            prime, take = _column_streamer(win_hbm.at[layer], gate0, 3 * chunks_per_gate, stage_ref, sem_ref)
            prime()
        x = x_ref[0]
        h = h_ref[0]
        merged = None
        for t, (o_br, w_br) in enumerate(branches):
            if stream_gates:
                for c in range(t * chunks_per_gate, (t + 1) * chunks_per_gate):
                    def consume(st, c=c):
                        wgate_ref[:, c * W_COLS:(c + 1) * W_COLS] = st[...].astype(BF16)
                    take(c, consume)
            gate = jax.nn.sigmoid(jnp.dot(h, wgate_ref[:, t * d:(t + 1) * d], preferred_element_type=F32))
            term = gate * jnp.dot(o_br[0], w_br[...], preferred_element_type=F32)
            merged = term if merged is None else merged + term
        y = jnp.dot(merged.astype(BF16), wout_ref[...], preferred_element_type=F32)
        o_ref[0] = x + mod_ref[0][5:6] * y

    @pl.when(first)
    def _():
        compute(True)

    @pl.when(jnp.logical_not(first))
    def _():
        compute(False)


def _merge(x, h, mod, o_sb, o_dil, o_swa, w_in, w_sb, w_dil, w_swa, w_out, layer):
    bn, s, d = x.shape
    row = lambda n: pl.BlockSpec((1, ROW_TILE, n), lambda b, i: (b, i, 0))
    return pl.pallas_call(
        functools.partial(_merge_kernel, layer=layer),
        grid=(bn, s // ROW_TILE),
        in_specs=[row(d), row(d),
                  pl.BlockSpec((1, ADA_CHUNKS, d), lambda b, i: (b, 0, 0)),
                  row(o_sb.shape[2]), row(o_dil.shape[2]), row(o_swa.shape[2]),
                  pl.BlockSpec(memory_space=pl.ANY), _slab_spec(w_sb, layer), _slab_spec(w_dil, layer),
                  _const_spec(w_swa.shape), _slab_spec(w_out, layer)],
        out_specs=row(d),
        out_shape=jax.ShapeDtypeStruct(x.shape, F32),
        scratch_shapes=[pltpu.VMEM((d, 3 * d), BF16), pltpu.VMEM((2, d, W_COLS), F32),
                        pltpu.SemaphoreType.DMA((2,))],
        compiler_params=pltpu.CompilerParams(dimension_semantics=("arbitrary", "arbitrary"),
                                             vmem_limit_bytes=VMEM_LIMIT),
        name="merge",
    )(x, h, mod, o_sb, o_dil, o_swa, w_in, w_sb, w_dil, w_swa, w_out)


def _t5_bucket(n):
    max_exact = N_BUCKETS // 2
    nf = jnp.maximum(n, 1).astype(jnp.float32)
    large = max_exact + (jnp.log(nf / max_exact) / math.log(MAX_REL_DIST / max_exact)
                         * (N_BUCKETS - max_exact)).astype(jnp.int32)
    large = jnp.minimum(large, N_BUCKETS - 1)
    return jnp.where(n < max_exact, n, large)


def _band_buckets(dilation):
    rel = jnp.arange(BLK)[:, None] + BLK - jnp.arange(2 * BLK)[None, :]
    return _t5_bucket(jnp.maximum(rel, 0) * dilation).astype(jnp.int32)


def _reorder_heads(w, order, axis):
    parts = [lax.slice_in_dim(w, hd * HEAD_DIM, (hd + 1) * HEAD_DIM, axis=axis) for hd in order]
    return jnp.concatenate(parts, axis=axis)


def kernel(x, c, w_ada, b_ada, norm_gain, w_ffn_gate, w_ffn_up, w_ffn_down, w_in,
           w_br_sb, w_br_dil, w_br_swa, w_out, sinks, rel_bias, final_gain):
    depth = w_ada.shape[0]
    bn, s, d = x.shape
    mods = _ada(c, w_ada, b_ada).reshape(depth, bn, ADA_CHUNKS, d)

    buckets = jnp.stack([_band_buckets(dil) for _, dil in DIL_PATTERNS])
    table = rel_bias.reshape(-1)
    wg, wu, wd = w_ffn_gate, w_ffn_up, w_ffn_down
    w_sb_bf, w_dil_bf, w_out_bf = w_br_sb.astype(BF16), w_br_dil.astype(BF16), w_out.astype(BF16)

    for l in range(depth):
        mod = mods[l]
        x = _ffn(x, mod, norm_gain[l, 0], wg, wu, wd, l, 0, 0)

        qkv_sb, qkv_dil, qkv_swa, h_mix = _inproj(x, mod, norm_gain[l, 1], w_in, l)
        o_sb = _sb_attention(qkv_sb)
        o_dil = _dil_attention(qkv_dil, buckets, table)
        o_swa = _swa_attention(qkv_swa, buckets[:1], table, sinks[l])
        w_swa = _reorder_heads(w_br_swa[l], _SWA_HEAD_ORDER, 0).astype(BF16)
        x = _merge(x, h_mix, mod, o_sb, o_dil, o_swa, w_in, w_sb_bf, w_dil_bf, w_swa, w_out_bf, l)

        last = l == depth - 1
        x = _ffn(x, mod, norm_gain[l, 2], wg, wu, wd, l, 1, 2, final_gain=final_gain if last else None)
    return x
```

```python
import functools
import math

import numpy as np

import jax
import jax.numpy as jnp
from jax import lax
from jax.experimental import pallas as pl
from jax.experimental.pallas import tpu as pltpu

F32 = jnp.float32
BF16 = jnp.bfloat16

HEAD_DIM = 64
LANES = 128
BLK = 128
H_SB = 4
DIL_PATTERNS = ((128, 1), (512, 4), (2048, 16))
H_PER_DIL = 2
H_DIL = H_PER_DIL * len(DIL_PATTERNS)
H_SWA_Q = 6
H_SWA_KV = 2
SWA_WINDOW = 128
N_SOFT_HEADS = H_DIL + H_SWA_Q
N_SWA_PAIRS = H_SWA_Q // 2
N_BUCKETS = 32
MAX_REL_DIST = 2048
RMS_EPS = 1e-6
ADA_CHUNKS = 9
NEG_BIG = -1e30
LOG2E = 1.4426950408889634

VMEM_LIMIT = 56 * 1024 * 1024
ROW_TILE = 512
FFN_ROW_TILE = 512
FF_CHUNK = 256
ADA_TILE = 3072
SB_ROWS = 512
SB_KEY_BLOCKS = 2
SB_DEAD_LOG2 = -150.0
BAND_UNROLL = 16
DIL_MERGE_ROWS = 512
MAX_STRIDE = 4
SWA_UNROLL = 8

_SWA_HEAD_ORDER = tuple(p + (H_SWA_Q // H_SWA_KV) * h for p in range(N_SWA_PAIRS) for h in range(2))


def _cparams(n_axes):
    return pltpu.CompilerParams(dimension_semantics=("parallel",) * n_axes,
                                vmem_limit_bytes=VMEM_LIMIT)


def _const_spec(shape):
    nd = len(shape)
    return pl.BlockSpec(shape, lambda *_: (0,) * nd, pipeline_mode=pl.Buffered(1))


def _slab_spec(arr, *lead):
    block = (None,) * len(lead) + tuple(arr.shape[len(lead):])
    return pl.BlockSpec(block, lambda *_: tuple(lead) + (0, 0), pipeline_mode=pl.Buffered(1))


def _ada_kernel(c_ref, w_ref, b_ref, o_ref):
    c = c_ref[...]
    sc = (c * jax.nn.sigmoid(c)).astype(BF16)
    o_ref[0] = jnp.dot(sc, w_ref[0].astype(BF16), preferred_element_type=F32) + b_ref[0]


def _ada(c, w_ada, b_ada):
    depth, d, n = w_ada.shape
    bn = c.shape[0]
    return pl.pallas_call(
        _ada_kernel,
        grid=(depth, n // ADA_TILE),
        in_specs=[pl.BlockSpec((bn, d), lambda l, j: (0, 0)),
                  pl.BlockSpec((1, d, ADA_TILE), lambda l, j: (l, 0, j)),
                  pl.BlockSpec((1, 1, ADA_TILE), lambda l, j: (l, 0, j))],
        out_specs=pl.BlockSpec((1, bn, ADA_TILE), lambda l, j: (l, 0, j)),
        out_shape=jax.ShapeDtypeStruct((depth, bn, n), F32),
        compiler_params=_cparams(2),
        name="ada_mod",
    )(c, w_ada, b_ada.reshape(depth, 1, n))


def _rms(x, gain):
    ms = jnp.mean(x * x, axis=-1, keepdims=True)
    return x * lax.rsqrt(ms + RMS_EPS) * gain


def _prenorm(x, gain, mod, j):
    shift = mod[3 * j:3 * j + 1]
    scale = mod[3 * j + 1:3 * j + 2]
    ms = jnp.mean(x * x, axis=-1, keepdims=True)
    return (x * lax.rsqrt(ms + RMS_EPS)) * (gain * (1.0 + scale)) + shift


def _dot_nt(a, b):
    return lax.dot_general(a, b, (((1,), (1,)), ((), ())), preferred_element_type=F32)


def _ffn_kernel(*refs, j, final, layer, which):
    n_in = 7 if final else 6
    x_ref, mod_ref, gain_ref, wg_hbm, wu_hbm, wd_hbm = refs[:6]
    fg_ref = refs[6] if final else None
    o_ref, wg_ref, wu_ref, wd_ref, stage_g_ref, stage_u_ref, stage_d_ref, sem_ref = refs[n_in:]
    n_chunks = wg_ref.shape[1] // FF_CHUNK
    chunk = lambda ci: slice(ci * FF_CHUNK, (ci + 1) * FF_CHUNK)
    first = (pl.program_id(0) == 0) & (pl.program_id(1) == 0)

    def chunk_copies(ci):
        slot = ci % 2
        return [pltpu.make_async_copy(wg_hbm.at[layer, which, :, chunk(ci)], stage_g_ref.at[slot], sem_ref.at[slot, 0]),
                pltpu.make_async_copy(wu_hbm.at[layer, which, :, chunk(ci)], stage_u_ref.at[slot], sem_ref.at[slot, 1]),
                pltpu.make_async_copy(wd_hbm.at[layer, which, chunk(ci), :], stage_d_ref.at[slot], sem_ref.at[slot, 2])]

    def compute(stream_weights):
        if stream_weights:
            for ci in range(min(2, n_chunks)):
                for cp in chunk_copies(ci):
                    cp.start()
        x = x_ref[0]
        mod = mod_ref[0]
        h = _prenorm(x, gain_ref[...], mod, j).astype(BF16)
        acc = jnp.zeros(x.shape, F32)
        for ci in range(n_chunks):
            if stream_weights:
                for cp in chunk_copies(ci):
                    cp.wait()
                wg_ref[:, chunk(ci)] = stage_g_ref[ci % 2].astype(BF16)
                wu_ref[:, chunk(ci)] = stage_u_ref[ci % 2].astype(BF16)
                wd_ref[chunk(ci), :] = stage_d_ref[ci % 2].astype(BF16)
                if ci + 2 < n_chunks:
                    for cp in chunk_copies(ci + 2):
                        cp.start()
            g = jnp.dot(h, wg_ref[:, chunk(ci)], preferred_element_type=F32)
            u = jnp.dot(h, wu_ref[:, chunk(ci)], preferred_element_type=F32)
            a = (g * jax.nn.sigmoid(g) * u).astype(BF16)
            acc = acc + jnp.dot(a, wd_ref[chunk(ci), :], preferred_element_type=F32)
        y = x + (0.5 * mod[3 * j + 2:3 * j + 3]) * acc
        if final:
            y = _rms(y, fg_ref[...])
        o_ref[0] = y

    @pl.when(first)
    def _():
        compute(True)

    @pl.when(jnp.logical_not(first))
    def _():
        compute(False)


def _ffn(x, mod, gain, wg, wu, wd, layer, which, j, final_gain=None):
    bn, s, d = x.shape
    d_ff = wg.shape[-1]
    final = final_gain is not None
    hbm = pl.BlockSpec(memory_space=pl.ANY)
    in_specs = [pl.BlockSpec((1, FFN_ROW_TILE, d), lambda b, i: (b, i, 0)),
                pl.BlockSpec((1, ADA_CHUNKS, d), lambda b, i: (b, 0, 0)),
                _const_spec((1, d)), hbm, hbm, hbm]
    args = [x, mod, gain.reshape(1, d), wg, wu, wd]
    if final:
        in_specs.append(_const_spec((1, d)))
        args.append(final_gain.reshape(1, d))
    return pl.pallas_call(
        functools.partial(_ffn_kernel, j=j, final=final, layer=layer, which=which),
        grid=(bn, s // FFN_ROW_TILE),
        in_specs=in_specs,
        out_specs=pl.BlockSpec((1, FFN_ROW_TILE, d), lambda b, i: (b, i, 0)),
        out_shape=jax.ShapeDtypeStruct(x.shape, F32),
        scratch_shapes=[pltpu.VMEM((d, d_ff), BF16), pltpu.VMEM((d, d_ff), BF16), pltpu.VMEM((d_ff, d), BF16),
                        pltpu.VMEM((2, d, FF_CHUNK), F32), pltpu.VMEM((2, d, FF_CHUNK), F32),
                        pltpu.VMEM((2, FF_CHUNK, d), F32), pltpu.SemaphoreType.DMA((2, 3))],
        compiler_params=pltpu.CompilerParams(dimension_semantics=("arbitrary", "arbitrary"),
                                             vmem_limit_bytes=VMEM_LIMIT),
        name="ffn_final" if final else f"ffn{j}",
    )(*args)


N_SB = 3 * H_SB * HEAD_DIM
N_DIL = 3 * H_DIL * HEAD_DIM
N_SWA = (H_SWA_Q + 2 * H_SWA_KV) * HEAD_DIM
N_QKV = N_SB + N_DIL + N_SWA
N_DIL_SLABS = N_DIL // LANES


W_COLS = 256
N_INPROJ_HEAD = 7 * W_COLS


def _column_streamer(w_hbm, first_col, n_chunks, stage_ref, sem_ref):
    def copy(c):
        return pltpu.make_async_copy(w_hbm.at[:, pl.ds(first_col + c * W_COLS, W_COLS)],
                                     stage_ref.at[c % 2], sem_ref.at[c % 2])

    def prime():
        for c in range(min(2, n_chunks)):
            copy(c).start()

    def take(c, consume):
        copy(c).wait()
        consume(stage_ref.at[c % 2])
        if c + 2 < n_chunks:
            copy(c + 2).start()

    return prime, take


def _inproj_tail_col(col):
    q0 = N_SB + N_DIL
    q1 = q0 + H_SWA_Q * HEAD_DIM
    if col < q0:
        return col - N_INPROJ_HEAD
    if col < q1:
        return (q0 - N_INPROJ_HEAD) + _SWA_HEAD_ORDER.index((col - q0) // HEAD_DIM) * HEAD_DIM
    return col - N_INPROJ_HEAD


def _inproj_kernel(x_ref, mod_ref, gain_ref, w_hbm, sb_ref, dil_ref, swa_ref, hout_ref,
                   whead_ref, wtail_ref, stage_ref, sem_ref, *, layer):
    qs = HEAD_DIM ** -0.5
    n_q_sb = H_SB * HEAD_DIM
    n_groups = len(DIL_PATTERNS)
    n_q_swa = H_SWA_Q * HEAD_DIM
    q0 = N_SB + N_DIL
    n_head_chunks = N_INPROJ_HEAD // W_COLS
    first = (pl.program_id(0) == 0) & (pl.program_id(1) == 0)

    def emit(h, r):
        hout_ref[0] = h
        sb_ref[0, :, :n_q_sb] = (r[:, :n_q_sb] * (qs * LOG2E)).astype(BF16)
        sb_ref[0, :, n_q_sb:] = r[:, n_q_sb:N_SB].astype(BF16)
        for a in range(3):
            for g in range(n_groups):
                c0 = N_SB + (a * n_groups + g) * LANES
                slab = r[:, c0:c0 + LANES]
                dil_ref[0, 3 * g + a] = slab * qs if a == 0 else slab
        swa_ref[0, :, :n_q_swa] = (r[:, q0:q0 + n_q_swa] * qs).astype(BF16)
        swa_ref[0, :, n_q_swa:] = r[:, q0 + n_q_swa:].astype(BF16)

    @pl.when(first)
    def _():
        prime, take = _column_streamer(w_hbm.at[layer], 0, N_QKV // W_COLS, stage_ref, sem_ref)
        prime()
        h = _prenorm(x_ref[0], gain_ref[...], mod_ref[0], 1).astype(BF16)
        parts = []
        for c in range(N_QKV // W_COLS):
            cols = slice(c * W_COLS, (c + 1) * W_COLS)
            if c < n_head_chunks:
                def consume(st, cols=cols):
                    whead_ref[:, cols] = st[...].astype(BF16)
                take(c, consume)
                parts.append(jnp.dot(h, whead_ref[:, cols], preferred_element_type=F32))
            else:
                def consume(st, c=c):
                    for piece in range(W_COLS // HEAD_DIM):
                        dst = _inproj_tail_col(c * W_COLS + piece * HEAD_DIM)
                        wtail_ref[:, dst:dst + HEAD_DIM] = \
                            st[:, piece * HEAD_DIM:(piece + 1) * HEAD_DIM].astype(BF16)
                take(c, consume)
        parts.append(jnp.dot(h, wtail_ref[...], preferred_element_type=F32))
        emit(h, jnp.concatenate(parts, axis=1))

    @pl.when(jnp.logical_not(first))
    def _():
        h = _prenorm(x_ref[0], gain_ref[...], mod_ref[0], 1).astype(BF16)
        emit(h, jnp.concatenate([jnp.dot(h, whead_ref[...], preferred_element_type=F32),
                                 jnp.dot(h, wtail_ref[...], preferred_element_type=F32)], axis=1))


def _inproj(x, mod, gain, w_in, layer):
    bn, s, d = x.shape
    nt = s // ROW_TILE
    n_tail = N_QKV - N_INPROJ_HEAD
    return pl.pallas_call(
        functools.partial(_inproj_kernel, layer=layer),
        grid=(bn, nt),
        in_specs=[pl.BlockSpec((1, ROW_TILE, d), lambda b, i: (b, i, 0)),
                  pl.BlockSpec((1, ADA_CHUNKS, d), lambda b, i: (b, 0, 0)),
                  _const_spec((1, d)), pl.BlockSpec(memory_space=pl.ANY)],
        out_specs=[pl.BlockSpec((1, ROW_TILE, N_SB), lambda b, i: (b, i, 0)),
                   pl.BlockSpec((1, N_DIL_SLABS, ROW_TILE, LANES), lambda b, i: (b, 0, i, 0)),
                   pl.BlockSpec((1, ROW_TILE, N_SWA), lambda b, i: (b, i, 0)),
                   pl.BlockSpec((1, ROW_TILE, d), lambda b, i: (b, i, 0))],
        out_shape=[jax.ShapeDtypeStruct((bn, s, N_SB), BF16),
                   jax.ShapeDtypeStruct((bn, N_DIL_SLABS, s, LANES), F32),
                   jax.ShapeDtypeStruct((bn, s, N_SWA), BF16),
                   jax.ShapeDtypeStruct((bn, s, d), BF16)],
        scratch_shapes=[pltpu.VMEM((d, N_INPROJ_HEAD), BF16), pltpu.VMEM((d, n_tail), BF16),
                        pltpu.VMEM((2, d, W_COLS), F32), pltpu.SemaphoreType.DMA((2,))],
        compiler_params=pltpu.CompilerParams(dimension_semantics=("arbitrary", "arbitrary"),
                                             vmem_limit_bytes=VMEM_LIMIT),
        name="in_proj",
    )(x, mod, gain.reshape(1, d), w_in)


def _sb_kernel(q_ref, k_ref, v_ref, o_ref, acc_ref, c_ref):
    s_len = q_ref.shape[1]
    n_pairs = q_ref.shape[2] // LANES
    per = SB_ROWS // BLK
    lane = lax.broadcasted_iota(jnp.int32, (BLK, LANES), 1)
    head0 = lane < HEAD_DIM
    row2 =lax.broadcasted_iota(jnp.int32, (BLK, 2 * BLK), 0)
    col2 = lax.broadcasted_iota(jnp.int32, (BLK, 2 * BLK), 1)
    strict2 = (col2 & (BLK - 1)) < row2
    rk = lax.broadcasted_iota(jnp.int32, (2 * BLK, 2 * BLK), 0)
    cn = lax.broadcasted_iota(jnp.int32, (2 * BLK, 2 * BLK), 1)
    neg_suffix_ones = jnp.where(((rk & (BLK - 1)) >= cn) | (cn >= BLK), -1.0, 0.0).astype(BF16)

    def by_head(t):
        zero = jnp.zeros_like(t)
        return jnp.concatenate([jnp.where(head0, t, zero), jnp.where(head0, zero, t)], axis=0)

    def mask_first_block(t):
        first = jnp.where(strict2, t[:BLK], 0.0)
        return first if t.shape[0] == BLK else jnp.concatenate([first, t[BLK:]], axis=0)

    def pair_lanes(p):
        return slice(p * LANES, (p + 1) * LANES)

    def scores(p, q, k0):
        return _dot_nt(q, by_head(k_ref[0, pl.ds(k0, BLK), pair_lanes(p)]))

    def carries(z, diag):
        softplus2 = jnp.maximum(z, 0.0) + jnp.log2(1.0 + jnp.exp2(-jnp.abs(z)))
        if diag:
            softplus2 = mask_first_block(softplus2)
        hi = softplus2.astype(BF16)
        lo = (softplus2 - hi.astype(F32)).astype(BF16)
        return [jnp.dot(jnp.concatenate([hi[:, h * BLK:(h + 1) * BLK], lo[:, h * BLK:(h + 1) * BLK]], axis=1),
                        neg_suffix_ones, preferred_element_type=F32) for h in range(2)]

    def weighted_values(p, z, cts, c, k0, diag):
        ws = [jnp.exp2(z[:, h * BLK:(h + 1) * BLK] + cts[h][:, :BLK] + c[h]) for h in range(2)]
        w = jnp.concatenate(ws, axis=1)
        if diag:
            w = mask_first_block(w)
        vv = by_head(v_ref[0, pl.ds(k0, BLK), pair_lanes(p)])
        return jnp.dot(w.astype(BF16), vv, preferred_element_type=F32)

    def superblock(sb, carry):
        base = pl.multiple_of(sb * SB_ROWS, SB_ROWS)
        acc_ref[...] = jnp.zeros_like(acc_ref)
        c_ref[...] = jnp.zeros_like(c_ref)

        def run(units):
            zs = [scores(p, q_ref[0, pl.ds(base + lo, n), pair_lanes(p)], k0) for p, lo, n, k0, _ in units]
            cts = [carries(z, unit[4]) for z, unit in zip(zs, units)]
            for z, ct, (p, lo, n, k0, diag) in zip(zs, cts, units):
                rows = slice(lo, lo + n)
                c = (c_ref[p, 0, rows, :], c_ref[p, 1, rows, :])
                acc_ref[p, rows, :] += weighted_values(p, z, ct, c, k0, diag)
                c_ref[p, 0, rows, :] = c[0] + ct[0][:, BLK:]
                c_ref[p, 1, rows, :] = c[1] + ct[1][:, BLK:]

        def alive(lo, n):
            c_max = None
            for p in range(n_pairs):
                for h in range(2):
                    c = c_ref[p, h, lo:lo + n, :]
                    c_max = c if c_max is None else jnp.maximum(c_max, c)
            return jnp.max(c_max) > SB_DEAD_LOG2

        run([(p, kq * BLK, SB_ROWS - kq * BLK, base + kq * BLK, True)
             for kq in reversed(range(per)) for p in range(n_pairs)])

        n_iter = sb * (per // SB_KEY_BLOCKS)

        half = SB_ROWS // 2

        def sweep(window, flag, state):
            def kstep(st):
                run([(p, window[0], window[1],
                      pl.multiple_of(base - (SB_KEY_BLOCKS * st[0] + u + 1) * BLK, BLK), False)
                     for u in range(SB_KEY_BLOCKS) for p in range(n_pairs)])
                return st[0] + 1, alive(half, half), alive(0, half)

            return lax.while_loop(lambda st: jnp.logical_and(st[0] < n_iter, st[flag]), kstep, state)

        state = sweep((0, SB_ROWS), 1, (jnp.int32(0), alive(half, half), alive(0, half)))
        sweep((0, half), 2, state)
        for p in range(n_pairs):
            o_ref[0, pl.ds(base, SB_ROWS), pair_lanes(p)] = acc_ref[p].astype(o_ref.dtype)
        return carry

    lax.fori_loop(0, s_len // SB_ROWS, superblock, 0)


def _sb_attention(qkv_sb):
    bn, s, _ = qkv_sb.shape
    n_pairs = H_SB // 2
    width = n_pairs * LANES
    spec = lambda t: pl.BlockSpec((1, s, width), lambda b: (b, 0, t))
    return pl.pallas_call(
        _sb_kernel,
        grid=(bn,),
        in_specs=[spec(0), spec(1), spec(2)],
        out_specs=pl.BlockSpec((1, s, width), lambda b: (b, 0, 0)),
        out_shape=jax.ShapeDtypeStruct((bn, s, width), BF16),
        scratch_shapes=[pltpu.VMEM((n_pairs, SB_ROWS, LANES), F32),
                        pltpu.VMEM((n_pairs, 2, SB_ROWS, LANES), F32)],
        compiler_params=_cparams(1),
        name="sb_attn",
    )(qkv_sb, qkv_sb, qkv_sb)


def _band_bias_tiles(bucket, table_ref, col, max_dist, sink=None):
    row = lax.broadcasted_iota(jnp.int32, (BLK, 2 * BLK), 0)
    kcol = lax.broadcasted_iota(jnp.int32, (BLK, 2 * BLK), 1)
    rel = row + BLK - kcol
    in_band = (rel >= 0) & (rel <= max_dist)
    bias = jnp.zeros((BLK, 2 * BLK), F32)
    for n in range(N_BUCKETS):
        bias = jnp.where(bucket == n, table_ref[n * N_SOFT_HEADS + col], bias)
    tiles = [jnp.where(in_band, bias, NEG_BIG), jnp.where(in_band & (kcol >= BLK), bias, NEG_BIG)]
    if sink is not None:
        tiles = [jnp.where(kcol == 0, sink, t) for t in tiles]
    return tiles


def _band_pair(q, kcat, vcat, biases):
    return _band_finish(*_band_pv(_band_scores(q, kcat, biases), vcat))


def _band_scores(q, kcat, biases):
    head0 = lax.broadcasted_iota(jnp.int32, q.shape, 1) < HEAD_DIM
    zero = jnp.zeros_like(q)
    return [_dot_nt(jnp.where(head0, q, zero) if h == 0 else jnp.where(head0, zero, q), kcat) + biases[h]
            for h in range(2)]


def _band_pv(scores, vcat):
    one = jnp.ones_like(vcat)
    v_head0 = lax.broadcasted_iota(jnp.int32, vcat.shape, 1) < HEAD_DIM
    outs, ms = [], []
    for h, s in enumerate(scores):
        vh = jnp.where(v_head0, vcat, one) if h == 0 else jnp.where(v_head0, one, vcat)
        m = jnp.max(s, axis=-1, keepdims=True)
        p = jnp.exp(s - m)
        outs.append(jnp.dot(p.astype(BF16), vh, preferred_element_type=F32))
        ms.append(m)
    return outs, ms


def _band_finish(outs, ms):
    head0 = lax.broadcasted_iota(jnp.int32, outs[0].shape, 1) < HEAD_DIM
    num = jnp.where(head0, outs[0], outs[1])
    den = pltpu.roll(jnp.where(head0, outs[1], outs[0]), HEAD_DIM, axis=1)
    return num, den, jnp.where(head0, ms[0], ms[1])


def _dil_kernel(table_ref, qkv_ref, bucket_ref, o_ref, bias_ref, og_ref, dg_ref, mg_ref, stage_ref, ostage_ref):
    s_len = o_ref.shape[1]
    n_blocks = s_len // BLK

    @pl.when(pl.program_id(0) == 0)
    def _():
        for g, (window, d) in enumerate(DIL_PATTERNS):
            for h in range(H_PER_DIL):
                head = g * H_PER_DIL + h
                normal, first = _band_bias_tiles(bucket_ref[g], table_ref, head, window // d)
                bias_ref[head, 0] = normal
                bias_ref[head, 1] = first

    for g, (window, d) in enumerate(DIL_PATTERNS):
        nb = s_len // d // BLK
        two_pass = d > MAX_STRIDE
        d_in = d // MAX_STRIDE if two_pass else d
        seg = s_len // MAX_STRIDE

        if two_pass:
            for c1 in range(MAX_STRIDE):
                for a in range(3):
                    stage_ref[a, c1 * seg:(c1 + 1) * seg, :] = \
                        qkv_ref[0, 3 * g + a, pl.ds(c1, seg, stride=MAX_STRIDE), :]

        def src(a, start):
            if d == 1:
                return qkv_ref[0, 3 * g + a, pl.ds(start, BLK), :]
            if not two_pass:
                return qkv_ref[0, 3 * g + a, pl.ds(start, BLK, stride=d), :]
            return stage_ref[a, pl.ds(staged(start), BLK, stride=d_in), :]

        def staged(start):
            return (start % MAX_STRIDE) * seg + start // MAX_STRIDE

        def put(dst_ref, ostage_slot, start, val):
            if d == 1:
                dst_ref[g, pl.ds(start, BLK), :] = val
            elif not two_pass:
                dst_ref[g, pl.ds(start, BLK, stride=d), :] = val
            else:
                ostage_ref[ostage_slot, pl.ds(staged(start), BLK, stride=d_in), :] = val

        def scores(idx):
            r, i = idx // nb, idx % nb
            start = r + d * BLK * i
            q = src(0, start).astype(BF16)
            heads = (g * H_PER_DIL, g * H_PER_DIL + 1)
            prev = r + d * BLK * jnp.maximum(i - 1, 0)
            kcat = jnp.concatenate([src(1, prev), src(1, start)], axis=0).astype(BF16)
            vcat = jnp.concatenate([src(2, prev), src(2, start)], axis=0).astype(BF16)
            variant = jnp.where(i > 0, 0, 1)
            biases = [bias_ref[hd, variant] for hd in heads]
            return start, _band_scores(q, kcat, biases), vcat

        def body(it, carry):
            staged1 = [scores(it * BAND_UNROLL + u) for u in range(BAND_UNROLL)]
            staged2 = [(start, _band_pv(s, vcat)) for start, s, vcat in staged1]
            for start, (outs, ms) in staged2:
                num, den, m = _band_finish(outs, ms)
                put(og_ref, 0, start, num)
                put(dg_ref, 1, start, den)
                put(mg_ref, 2, start, m)
            return carry

        lax.fori_loop(0, n_blocks // BAND_UNROLL, body, 0)

        if two_pass:
            for c1 in range(MAX_STRIDE):
                rows_in = slice(c1 * seg, (c1 + 1) * seg)
                rows_out = pl.ds(c1, seg, stride=MAX_STRIDE)
                og_ref[g, rows_out, :] = ostage_ref[0, rows_in, :]
                dg_ref[g, rows_out, :] = ostage_ref[1, rows_in, :]
                mg_ref[g, rows_out, :] = ostage_ref[2, rows_in, :]

    for r0 in range(0, s_len, DIL_MERGE_ROWS):
        sl = slice(r0, r0 + DIL_MERGE_ROWS)
        m0, m1, m2 = mg_ref[0, sl, :], mg_ref[1, sl, :], mg_ref[2, sl, :]
        m = jnp.maximum(jnp.maximum(m0, m1), m2)
        e0, e1, e2 = jnp.exp(m0 - m), jnp.exp(m1 - m), jnp.exp(m2 - m)
        num = e0 * og_ref[0, sl, :] + e1 * og_ref[1, sl, :] + e2 * og_ref[2, sl, :]
        den = e0 * dg_ref[0, sl, :] + e1 * dg_ref[1, sl, :] + e2 * dg_ref[2, sl, :]
        o_ref[0, sl, :] = (num / den).astype(o_ref.dtype)


def _dil_attention(qkv_dil, buckets, table):
    bn, n_slabs, s, _ = qkv_dil.shape
    n_groups = len(DIL_PATTERNS)
    return pl.pallas_call(
        _dil_kernel,
        grid_spec=pltpu.PrefetchScalarGridSpec(
            num_scalar_prefetch=1,
            grid=(bn,),
            in_specs=[pl.BlockSpec((1, n_slabs, s, LANES), lambda b, tb: (b, 0, 0, 0)),
                      pl.BlockSpec(buckets.shape, lambda b, tb: (0, 0, 0))],
            out_specs=pl.BlockSpec((1, s, LANES), lambda b, tb: (b, 0, 0)),
            scratch_shapes=[pltpu.VMEM((H_DIL, 2, BLK, 2 * BLK), F32),
                            pltpu.VMEM((n_groups, s, LANES), F32),
                            pltpu.VMEM((n_groups, s, LANES), F32),
                            pltpu.VMEM((n_groups, s, LANES), F32),
                            pltpu.VMEM((3, s, LANES), F32),
                            pltpu.VMEM((3, s, LANES), F32)],
        ),
        out_shape=jax.ShapeDtypeStruct((bn, s, LANES), BF16),
        compiler_params=pltpu.CompilerParams(dimension_semantics=("arbitrary",),
                                             vmem_limit_bytes=VMEM_LIMIT),
        name="dil_attn",
    )(table, qkv_dil, buckets)


def _swa_kernel(table_ref, sink_ref, qkv_ref, bucket_ref, o_ref, bias_ref):
    s_len = o_ref.shape[1]
    k_col = N_SWA_PAIRS * LANES
    v_col = k_col + LANES
    key_row = lax.broadcasted_iota(jnp.int32, (2 * BLK, LANES), 0)

    @pl.when(pl.program_id(0) == 0)
    def _():
        for slot, hd in enumerate(_SWA_HEAD_ORDER):
            normal, first = _band_bias_tiles(bucket_ref[0], table_ref, H_DIL + hd, SWA_WINDOW - 1, sink_ref[hd])
            bias_ref[slot, 0] = normal
            bias_ref[slot, 1] = first

    def scores(i):
        start = pl.multiple_of(i * BLK, BLK)
        prev = pl.multiple_of(jnp.maximum(i - 1, 0) * BLK, BLK)
        kcat = jnp.concatenate([qkv_ref[0, pl.ds(prev, BLK), k_col:k_col + LANES],
                                qkv_ref[0, pl.ds(start, BLK), k_col:k_col + LANES]], axis=0)
        vcat = jnp.concatenate([qkv_ref[0, pl.ds(prev, BLK), v_col:v_col + LANES],
                                qkv_ref[0, pl.ds(start, BLK), v_col:v_col + LANES]], axis=0)
        kcat = jnp.where(key_row == 0, jnp.zeros_like(kcat), kcat)
        vcat = jnp.where(key_row == 0, jnp.zeros_like(vcat), vcat)
        variant = jnp.where(i > 0, 0, 1)
        out = []
        for p in range(N_SWA_PAIRS):
            q = qkv_ref[0, pl.ds(start, BLK), p * LANES:(p + 1) * LANES]
            biases = [bias_ref[2 * p + h, variant] for h in range(2)]
            out.append((start, p, _band_scores(q, kcat, biases), vcat))
        return out

    def body(it, carry):
        staged1 = [unit for u in range(SWA_UNROLL) for unit in scores(it * SWA_UNROLL + u)]
        staged2 = [(start, p, _band_pv(s, vcat)) for start, p, s, vcat in staged1]
        for start, p, (outs, ms) in staged2:
            num, den, _ = _band_finish(outs, ms)
            o_ref[0, pl.ds(start, BLK), p * LANES:(p + 1) * LANES] = (num * (1.0 / den)).astype(o_ref.dtype)
        return carry

    lax.fori_loop(0, s_len // BLK // SWA_UNROLL, body, 0)


def _swa_attention(qkv_swa, bucket, table, sinks_l):
    bn, s, n = qkv_swa.shape
    return pl.pallas_call(
        _swa_kernel,
        grid_spec=pltpu.PrefetchScalarGridSpec(
            num_scalar_prefetch=2,
            grid=(bn,),
            in_specs=[pl.BlockSpec((1, s, n), lambda b, tb, sk: (b, 0, 0)),
                      pl.BlockSpec(bucket.shape, lambda b, tb, sk: (0, 0, 0))],
            out_specs=pl.BlockSpec((1, s, N_SWA_PAIRS * LANES), lambda b, tb, sk: (b, 0, 0)),
            scratch_shapes=[pltpu.VMEM((H_SWA_Q, 2, BLK, 2 * BLK), F32)],
        ),
        out_shape=jax.ShapeDtypeStruct((bn, s, N_SWA_PAIRS * LANES), BF16),
        compiler_params=pltpu.CompilerParams(dimension_semantics=("arbitrary",),
                                             vmem_limit_bytes=VMEM_LIMIT),
        name="swa_attn",
    )(table, sinks_l, qkv_swa, bucket)


def _merge_kernel(x_ref, h_ref, mod_ref, osb_ref, odil_ref, oswa_ref,
                  win_hbm, wsb_ref, wdil_ref, wswa_ref, wout_ref, o_ref, wgate_ref, stage_ref, sem_ref, *, layer):
    d = x_ref.shape[2]
    gate0 = win_hbm.shape[2] - 3 * d
    chunks_per_gate = d // W_COLS
    branches = ((osb_ref, wsb_ref), (odil_ref, wdil_ref), (oswa_ref, wswa_ref))
    first = (pl.program_id(0) == 0) & (pl.program_id(1) == 0)

    def compute(stream_gates):
        if stream_gates:
            prime, take = _column_streamer(win_hbm.at[layer], gate0, 3 * chunks_per_gate, stage_ref, sem_ref)
            prime()
        x = x_ref[0]
        h = h_ref[0]
        merged = None
        for t, (o_br, w_br) in enumerate(branches):
            if stream_gates:
                for c in range(t * chunks_per_gate, (t + 1) * chunks_per_gate):
                    def consume(st, c=c):
                        wgate_ref[:, c * W_COLS:(c + 1) * W_COLS] = st[...].astype(BF16)
                    take(c, consume)
            gate = jax.nn.sigmoid(jnp.dot(h, wgate_ref[:, t * d:(t + 1) * d], preferred_element_type=F32))
            term = gate * jnp.dot(o_br[0], w_br[...], preferred_element_type=F32)
            merged = term if merged is None else merged + term
        y = jnp.dot(merged.astype(BF16), wout_ref[...], preferred_element_type=F32)
        o_ref[0] = x + mod_ref[0][5:6] * y

    @pl.when(first)
    def _():
        compute(True)

    @pl.when(jnp.logical_not(first))
    def _():
        compute(False)


def _merge(x, h, mod, o_sb, o_dil, o_swa, w_in, w_sb, w_dil, w_swa, w_out, layer):
    bn, s, d = x.shape
    row = lambda n: pl.BlockSpec((1, ROW_TILE, n), lambda b, i: (b, i, 0))
    return pl.pallas_call(
        functools.partial(_merge_kernel, layer=layer),
        grid=(bn, s // ROW_TILE),
        in_specs=[row(d), row(d),
                  pl.BlockSpec((1, ADA_CHUNKS, d), lambda b, i: (b, 0, 0)),
                  row(o_sb.shape[2]), row(o_dil.shape[2]), row(o_swa.shape[2]),
                  pl.BlockSpec(memory_space=pl.ANY), _slab_spec(w_sb, layer), _slab_spec(w_dil, layer),
                  _const_spec(w_swa.shape), _slab_spec(w_out, layer)],
        out_specs=row(d),
        out_shape=jax.ShapeDtypeStruct(x.shape, F32),
        scratch_shapes=[pltpu.VMEM((d, 3 * d), BF16), pltpu.VMEM((2, d, W_COLS), F32),
                        pltpu.SemaphoreType.DMA((2,))],
        compiler_params=pltpu.CompilerParams(dimension_semantics=("arbitrary", "arbitrary"),
                                             vmem_limit_bytes=VMEM_LIMIT),
        name="merge",
    )(x, h, mod, o_sb, o_dil, o_swa, w_in, w_sb, w_dil, w_swa, w_out)


def _t5_bucket(n, dtype):
    max_exact = N_BUCKETS // 2
    nf = np.maximum(n, 1).astype(dtype)
    scaled = np.log(nf / dtype(max_exact)) / dtype(math.log(MAX_REL_DIST / max_exact)) * dtype(N_BUCKETS - max_exact)
    large = np.minimum(max_exact + scaled.astype(np.int32), N_BUCKETS - 1)
    return np.where(n < max_exact, n, large).astype(np.int32)


def _band_buckets(dilation):
    rel = np.arange(BLK)[:, None] + BLK - np.arange(2 * BLK)[None, :]
    n = np.maximum(rel, 0) * dilation
    buckets = _t5_bucket(n, np.float32)
    assert np.array_equal(buckets, _t5_bucket(n, np.float64)), "relative-position bucket on a rounding boundary"
    return buckets


def _reorder_heads(w, order, axis):
    parts = [lax.slice_in_dim(w, hd * HEAD_DIM, (hd + 1) * HEAD_DIM, axis=axis) for hd in order]
    return jnp.concatenate(parts, axis=axis)


def kernel(x, c, w_ada, b_ada, norm_gain, w_ffn_gate, w_ffn_up, w_ffn_down, w_in,
           w_br_sb, w_br_dil, w_br_swa, w_out, sinks, rel_bias, final_gain):
    depth = w_ada.shape[0]
    bn, s, d = x.shape
    mods = _ada(c, w_ada, b_ada).reshape(depth, bn, ADA_CHUNKS, d)

    buckets = jnp.asarray(np.stack([_band_buckets(dil) for _, dil in DIL_PATTERNS]))
    table = rel_bias.reshape(-1)
    wg, wu, wd = w_ffn_gate, w_ffn_up, w_ffn_down
    w_sb_bf, w_dil_bf, w_out_bf = w_br_sb.astype(BF16), w_br_dil.astype(BF16), w_out.astype(BF16)

    for l in range(depth):
        mod = mods[l]
        x = _ffn(x, mod, norm_gain[l, 0], wg, wu, wd, l, 0, 0)

        qkv_sb, qkv_dil, qkv_swa, h_mix = _inproj(x, mod, norm_gain[l, 1], w_in, l)
        o_sb = _sb_attention(qkv_sb)
        o_dil = _dil_attention(qkv_dil, buckets, table)
        o_swa = _swa_attention(qkv_swa, buckets[:1], table, sinks[l])
        w_swa = _reorder_heads(w_br_swa[l], _SWA_HEAD_ORDER, 0).astype(BF16)
        x = _merge(x, h_mix, mod, o_sb, o_dil, o_swa, w_in, w_sb_bf, w_dil_bf, w_swa, w_out_bf, l)

        last = l == depth - 1
        x = _ffn(x, mod, norm_gain[l, 2], wg, wu, wd, l, 1, 2, final_gain=final_gain if last else None)
    return x
```

```python
import functools
import math

import numpy as np

import jax
import jax.numpy as jnp
from jax import lax
from jax.experimental import pallas as pl
from jax.experimental.pallas import tpu as pltpu

F32 = jnp.float32
BF16 = jnp.bfloat16

HEAD_DIM = 64
LANES = 128
BLK = 128
H_SB = 4
DIL_PATTERNS = ((128, 1), (512, 4), (2048, 16))
H_PER_DIL = 2
H_DIL = H_PER_DIL * len(DIL_PATTERNS)
H_SWA_Q = 6
H_SWA_KV = 2
SWA_WINDOW = 128
N_SOFT_HEADS = H_DIL + H_SWA_Q
N_SWA_PAIRS = H_SWA_Q // 2
N_BUCKETS = 32
MAX_REL_DIST = 2048
RMS_EPS = 1e-6
ADA_CHUNKS = 9
NEG_BIG = -1e30
LOG2E = 1.4426950408889634

VMEM_LIMIT = 56 * 1024 * 1024
ROW_TILE = 512
FFN_ROW_TILE = 512
FF_CHUNK = 256
ADA_TILE = 3072
SB_ROWS = 512
SB_KEY_BLOCKS = 2
SB_DEAD_LOG2 = -150.0
BAND_UNROLL = 8
DIL_MERGE_ROWS = 512
MAX_STRIDE = 4
SWA_UNROLL = 4

_SWA_HEAD_ORDER = tuple(p + (H_SWA_Q // H_SWA_KV) * h for p in range(N_SWA_PAIRS) for h in range(2))


def _cparams(n_axes):
    return pltpu.CompilerParams(dimension_semantics=("parallel",) * n_axes,
                                vmem_limit_bytes=VMEM_LIMIT)


def _const_spec(shape):
    nd = len(shape)
    return pl.BlockSpec(shape, lambda *_: (0,) * nd, pipeline_mode=pl.Buffered(1))


def _slab_spec(arr, *lead):
    block = (None,) * len(lead) + tuple(arr.shape[len(lead):])
    return pl.BlockSpec(block, lambda *_: tuple(lead) + (0, 0), pipeline_mode=pl.Buffered(1))


def _ada_kernel(c_ref, w_ref, b_ref, o_ref):
    c = c_ref[...]
    sc = (c * jax.nn.sigmoid(c)).astype(BF16)
    o_ref[0] = jnp.dot(sc, w_ref[0].astype(BF16), preferred_element_type=F32) + b_ref[0]


def _ada(c, w_ada, b_ada):
    depth, d, n = w_ada.shape
    bn = c.shape[0]
    return pl.pallas_call(
        _ada_kernel,
        grid=(depth, n // ADA_TILE),
        in_specs=[pl.BlockSpec((bn, d), lambda l, j: (0, 0)),
                  pl.BlockSpec((1, d, ADA_TILE), lambda l, j: (l, 0, j)),
                  pl.BlockSpec((1, 1, ADA_TILE), lambda l, j: (l, 0, j))],
        out_specs=pl.BlockSpec((1, bn, ADA_TILE), lambda l, j: (l, 0, j)),
        out_shape=jax.ShapeDtypeStruct((depth, bn, n), F32),
        compiler_params=_cparams(2),
        name="ada_mod",
    )(c, w_ada, b_ada.reshape(depth, 1, n))


def _rms(x, gain):
    ms = jnp.mean(x * x, axis=-1, keepdims=True)
    return x * lax.rsqrt(ms + RMS_EPS) * gain


def _prenorm(x, gain, mod, j):
    shift = mod[3 * j:3 * j + 1]
    scale = mod[3 * j + 1:3 * j + 2]
    ms = jnp.mean(x * x, axis=-1, keepdims=True)
    return (x * lax.rsqrt(ms + RMS_EPS)) * (gain * (1.0 + scale)) + shift


def _dot_nt(a, b):
    return lax.dot_general(a, b, (((1,), (1,)), ((), ())), preferred_element_type=F32)


def _ffn_kernel(*refs, j, final, layer, which):
    n_in = 7 if final else 6
    x_ref, mod_ref, gain_ref, wg_hbm, wu_hbm, wd_hbm = refs[:6]
    fg_ref = refs[6] if final else None
    o_ref, wg_ref, wu_ref, wd_ref, stage_g_ref, stage_u_ref, stage_d_ref, sem_ref = refs[n_in:]
    n_chunks = wg_ref.shape[1] // FF_CHUNK
    chunk = lambda ci: slice(ci * FF_CHUNK, (ci + 1) * FF_CHUNK)
    first = (pl.program_id(0) == 0) & (pl.program_id(1) == 0)

    def chunk_copies(ci):
        slot = ci % 2
        return [pltpu.make_async_copy(wg_hbm.at[layer, which, :, chunk(ci)], stage_g_ref.at[slot], sem_ref.at[slot, 0]),
                pltpu.make_async_copy(wu_hbm.at[layer, which, :, chunk(ci)], stage_u_ref.at[slot], sem_ref.at[slot, 1]),
                pltpu.make_async_copy(wd_hbm.at[layer, which, chunk(ci), :], stage_d_ref.at[slot], sem_ref.at[slot, 2])]

    def compute(stream_weights):
        if stream_weights:
            for ci in range(min(2, n_chunks)):
                for cp in chunk_copies(ci):
                    cp.start()
        x = x_ref[0]
        mod = mod_ref[0]
        h = _prenorm(x, gain_ref[...], mod, j).astype(BF16)
        acc = jnp.zeros(x.shape, F32)
        for ci in range(n_chunks):
            if stream_weights:
                for cp in chunk_copies(ci):
                    cp.wait()
                wg_ref[:, chunk(ci)] = stage_g_ref[ci % 2].astype(BF16)
                wu_ref[:, chunk(ci)] = stage_u_ref[ci % 2].astype(BF16)
                wd_ref[chunk(ci), :] = stage_d_ref[ci % 2].astype(BF16)
                if ci + 2 < n_chunks:
                    for cp in chunk_copies(ci + 2):
                        cp.start()
            g = jnp.dot(h, wg_ref[:, chunk(ci)], preferred_element_type=F32)
            u = jnp.dot(h, wu_ref[:, chunk(ci)], preferred_element_type=F32)
            a = (g * jax.nn.sigmoid(g) * u).astype(BF16)
            acc = acc + jnp.dot(a, wd_ref[chunk(ci), :], preferred_element_type=F32)
        y = x + (0.5 * mod[3 * j + 2:3 * j + 3]) * acc
        if final:
            y = _rms(y, fg_ref[...])
        o_ref[0] = y

    @pl.when(first)
    def _():
        compute(True)

    @pl.when(jnp.logical_not(first))
    def _():
        compute(False)


def _ffn(x, mod, gain, wg, wu, wd, layer, which, j, final_gain=None):
    bn, s, d = x.shape
    d_ff = wg.shape[-1]
    final = final_gain is not None
    hbm = pl.BlockSpec(memory_space=pl.ANY)
    in_specs = [pl.BlockSpec((1, FFN_ROW_TILE, d), lambda b, i: (b, i, 0)),
                pl.BlockSpec((1, ADA_CHUNKS, d), lambda b, i: (b, 0, 0)),
                _const_spec((1, d)), hbm, hbm, hbm]
    args = [x, mod, gain.reshape(1, d), wg, wu, wd]
    if final:
        in_specs.append(_const_spec((1, d)))
        args.append(final_gain.reshape(1, d))
    return pl.pallas_call(
        functools.partial(_ffn_kernel, j=j, final=final, layer=layer, which=which),
        grid=(bn, s // FFN_ROW_TILE),
        in_specs=in_specs,
        out_specs=pl.BlockSpec((1, FFN_ROW_TILE, d), lambda b, i: (b, i, 0)),
        out_shape=jax.ShapeDtypeStruct(x.shape, F32),
        scratch_shapes=[pltpu.VMEM((d, d_ff), BF16), pltpu.VMEM((d, d_ff), BF16), pltpu.VMEM((d_ff, d), BF16),
                        pltpu.VMEM((2, d, FF_CHUNK), F32), pltpu.VMEM((2, d, FF_CHUNK), F32),
                        pltpu.VMEM((2, FF_CHUNK, d), F32), pltpu.SemaphoreType.DMA((2, 3))],
        compiler_params=pltpu.CompilerParams(dimension_semantics=("arbitrary", "arbitrary"),
                                             vmem_limit_bytes=VMEM_LIMIT),
        name="ffn_final" if final else f"ffn{j}",
    )(*args)


N_SB = 3 * H_SB * HEAD_DIM
N_DIL = 3 * H_DIL * HEAD_DIM
N_SWA = (H_SWA_Q + 2 * H_SWA_KV) * HEAD_DIM
N_QKV = N_SB + N_DIL + N_SWA
N_DIL_SLABS = N_DIL // LANES


W_COLS = 256
N_INPROJ_HEAD = 7 * W_COLS


def _column_streamer(w_hbm, first_col, n_chunks, stage_ref, sem_ref):
    def copy(c):
        return pltpu.make_async_copy(w_hbm.at[:, pl.ds(first_col + c * W_COLS, W_COLS)],
                                     stage_ref.at[c % 2], sem_ref.at[c % 2])

    def prime():
        for c in range(min(2, n_chunks)):
            copy(c).start()

    def take(c, consume):
        copy(c).wait()
        consume(stage_ref.at[c % 2])
        if c + 2 < n_chunks:
            copy(c + 2).start()

    return prime, take


def _inproj_tail_col(col):
    q0 = N_SB + N_DIL
    q1 = q0 + H_SWA_Q * HEAD_DIM
    if col < q0:
        return col - N_INPROJ_HEAD
    if col < q1:
        return (q0 - N_INPROJ_HEAD) + _SWA_HEAD_ORDER.index((col - q0) // HEAD_DIM) * HEAD_DIM
    return col - N_INPROJ_HEAD


def _inproj_kernel(x_ref, mod_ref, gain_ref, w_hbm, sb_ref, dil_ref, swa_ref, hout_ref,
                   whead_ref, wtail_ref, stage_ref, sem_ref, *, layer):
    qs = HEAD_DIM ** -0.5
    n_q_sb = H_SB * HEAD_DIM
    n_groups = len(DIL_PATTERNS)
    n_q_swa = H_SWA_Q * HEAD_DIM
    q0 = N_SB + N_DIL
    n_head_chunks = N_INPROJ_HEAD // W_COLS
    first = (pl.program_id(0) == 0) & (pl.program_id(1) == 0)

    def emit(h, r):
        hout_ref[0] = h
        sb_ref[0, :, :n_q_sb] = (r[:, :n_q_sb] * (qs * LOG2E)).astype(BF16)
        sb_ref[0, :, n_q_sb:] = r[:, n_q_sb:N_SB].astype(BF16)
        for a in range(3):
            for g in range(n_groups):
                c0 = N_SB + (a * n_groups + g) * LANES
                slab = r[:, c0:c0 + LANES]
                dil_ref[0, 3 * g + a] = slab * qs if a == 0 else slab
        swa_ref[0, :, :n_q_swa] = (r[:, q0:q0 + n_q_swa] * qs).astype(BF16)
        swa_ref[0, :, n_q_swa:] = r[:, q0 + n_q_swa:].astype(BF16)

    @pl.when(first)
    def _():
        prime, take = _column_streamer(w_hbm.at[layer], 0, N_QKV // W_COLS, stage_ref, sem_ref)
        prime()
        h = _prenorm(x_ref[0], gain_ref[...], mod_ref[0], 1).astype(BF16)
        parts = []
        for c in range(N_QKV // W_COLS):
            cols = slice(c * W_COLS, (c + 1) * W_COLS)
            if c < n_head_chunks:
                def consume(st, cols=cols):
                    whead_ref[:, cols] = st[...].astype(BF16)
                take(c, consume)
                parts.append(jnp.dot(h, whead_ref[:, cols], preferred_element_type=F32))
            else:
                def consume(st, c=c):
                    for piece in range(W_COLS // HEAD_DIM):
                        dst = _inproj_tail_col(c * W_COLS + piece * HEAD_DIM)
                        wtail_ref[:, dst:dst + HEAD_DIM] = \
                            st[:, piece * HEAD_DIM:(piece + 1) * HEAD_DIM].astype(BF16)
                take(c, consume)
        parts.append(jnp.dot(h, wtail_ref[...], preferred_element_type=F32))
        emit(h, jnp.concatenate(parts, axis=1))

    @pl.when(jnp.logical_not(first))
    def _():
        h = _prenorm(x_ref[0], gain_ref[...], mod_ref[0], 1).astype(BF16)
        emit(h, jnp.concatenate([jnp.dot(h, whead_ref[...], preferred_element_type=F32),
                                 jnp.dot(h, wtail_ref[...], preferred_element_type=F32)], axis=1))


def _inproj(x, mod, gain, w_in, layer):
    bn, s, d = x.shape
    nt = s // ROW_TILE
    n_tail = N_QKV - N_INPROJ_HEAD
    return pl.pallas_call(
        functools.partial(_inproj_kernel, layer=layer),
        grid=(bn, nt),
        in_specs=[pl.BlockSpec((1, ROW_TILE, d), lambda b, i: (b, i, 0)),
                  pl.BlockSpec((1, ADA_CHUNKS, d), lambda b, i: (b, 0, 0)),
                  _const_spec((1, d)), pl.BlockSpec(memory_space=pl.ANY)],
        out_specs=[pl.BlockSpec((1, ROW_TILE, N_SB), lambda b, i: (b, i, 0)),
                   pl.BlockSpec((1, N_DIL_SLABS, ROW_TILE, LANES), lambda b, i: (b, 0, i, 0)),
                   pl.BlockSpec((1, ROW_TILE, N_SWA), lambda b, i: (b, i, 0)),
                   pl.BlockSpec((1, ROW_TILE, d), lambda b, i: (b, i, 0))],
        out_shape=[jax.ShapeDtypeStruct((bn, s, N_SB), BF16),
                   jax.ShapeDtypeStruct((bn, N_DIL_SLABS, s, LANES), F32),
                   jax.ShapeDtypeStruct((bn, s, N_SWA), BF16),
                   jax.ShapeDtypeStruct((bn, s, d), BF16)],
        scratch_shapes=[pltpu.VMEM((d, N_INPROJ_HEAD), BF16), pltpu.VMEM((d, n_tail), BF16),
                        pltpu.VMEM((2, d, W_COLS), F32), pltpu.SemaphoreType.DMA((2,))],
        compiler_params=pltpu.CompilerParams(dimension_semantics=("arbitrary", "arbitrary"),
                                             vmem_limit_bytes=VMEM_LIMIT),
        name="in_proj",
    )(x, mod, gain.reshape(1, d), w_in)


def _sb_kernel(q_ref, k_ref, v_ref, o_ref, acc_ref, c_ref):
    s_len = q_ref.shape[1]
    n_pairs = q_ref.shape[2] // LANES
    per = SB_ROWS // BLK
    lane = lax.broadcasted_iota(jnp.int32, (BLK, LANES), 1)
    head0 = lane < HEAD_DIM
    row2 =lax.broadcasted_iota(jnp.int32, (BLK, 2 * BLK), 0)
    col2 = lax.broadcasted_iota(jnp.int32, (BLK, 2 * BLK), 1)
    strict2 = (col2 & (BLK - 1)) < row2
    rk = lax.broadcasted_iota(jnp.int32, (2 * BLK, 2 * BLK), 0)
    cn = lax.broadcasted_iota(jnp.int32, (2 * BLK, 2 * BLK), 1)
    neg_suffix_ones = jnp.where(((rk & (BLK - 1)) >= cn) | (cn >= BLK), -1.0, 0.0).astype(BF16)

    def by_head(t):
        zero = jnp.zeros_like(t)
        return jnp.concatenate([jnp.where(head0, t, zero), jnp.where(head0, zero, t)], axis=0)

    def mask_first_block(t):
        first = jnp.where(strict2, t[:BLK], 0.0)
        return first if t.shape[0] == BLK else jnp.concatenate([first, t[BLK:]], axis=0)

    def pair_lanes(p):
        return slice(p * LANES, (p + 1) * LANES)

    def scores(p, q, k0):
        return _dot_nt(q, by_head(k_ref[0, pl.ds(k0, BLK), pair_lanes(p)]))

    def carries(z, diag):
        softplus2 = jnp.maximum(z, 0.0) + jnp.log2(1.0 + jnp.exp2(-jnp.abs(z)))
        if diag:
            softplus2 = mask_first_block(softplus2)
        hi = softplus2.astype(BF16)
        lo = (softplus2 - hi.astype(F32)).astype(BF16)
        return [jnp.dot(jnp.concatenate([hi[:, h * BLK:(h + 1) * BLK], lo[:, h * BLK:(h + 1) * BLK]], axis=1),
                        neg_suffix_ones, preferred_element_type=F32) for h in range(2)]

    def weighted_values(p, z, cts, c, k0, diag):
        ws = [jnp.exp2(z[:, h * BLK:(h + 1) * BLK] + cts[h][:, :BLK] + c[h]) for h in range(2)]
        w = jnp.concatenate(ws, axis=1)
        if diag:
            w = mask_first_block(w)
        vv = by_head(v_ref[0, pl.ds(k0, BLK), pair_lanes(p)])
        return jnp.dot(w.astype(BF16), vv, preferred_element_type=F32)

    def superblock(sb, carry):
        base = pl.multiple_of(sb * SB_ROWS, SB_ROWS)
        acc_ref[...] = jnp.zeros_like(acc_ref)
        c_ref[...] = jnp.zeros_like(c_ref)

        def run(units):
            zs = [scores(p, q_ref[0, pl.ds(base + lo, n), pair_lanes(p)], k0) for p, lo, n, k0, _ in units]
            cts = [carries(z, unit[4]) for z, unit in zip(zs, units)]
            for z, ct, (p, lo, n, k0, diag) in zip(zs, cts, units):
                rows = slice(lo, lo + n)
                c = (c_ref[p, 0, rows, :], c_ref[p, 1, rows, :])
                acc_ref[p, rows, :] += weighted_values(p, z, ct, c, k0, diag)
                c_ref[p, 0, rows, :] = c[0] + ct[0][:, BLK:]
                c_ref[p, 1, rows, :] = c[1] + ct[1][:, BLK:]

        def alive(lo, n):
            c_max = None
            for p in range(n_pairs):
                for h in range(2):
                    c = c_ref[p, h, lo:lo + n, :]
                    c_max = c if c_max is None else jnp.maximum(c_max, c)
            return jnp.max(c_max) > SB_DEAD_LOG2

        run([(p, kq * BLK, SB_ROWS - kq * BLK, base + kq * BLK, True)
             for kq in reversed(range(per)) for p in range(n_pairs)])

        n_iter = sb * (per // SB_KEY_BLOCKS)

        half = SB_ROWS // 2

        def sweep(window, flag, state):
            def kstep(st):
                run([(p, window[0], window[1],
                      pl.multiple_of(base - (SB_KEY_BLOCKS * st[0] + u + 1) * BLK, BLK), False)
                     for u in range(SB_KEY_BLOCKS) for p in range(n_pairs)])
                return st[0] + 1, alive(half, half), alive(0, half)

            return lax.while_loop(lambda st: jnp.logical_and(st[0] < n_iter, st[flag]), kstep, state)

        state = sweep((0, SB_ROWS), 1, (jnp.int32(0), alive(half, half), alive(0, half)))
        sweep((0, half), 2, state)
        for p in range(n_pairs):
            o_ref[0, pl.ds(base, SB_ROWS), pair_lanes(p)] = acc_ref[p].astype(o_ref.dtype)
        return carry

    lax.fori_loop(0, s_len // SB_ROWS, superblock, 0)


def _sb_attention(qkv_sb):
    bn, s, _ = qkv_sb.shape
    n_pairs = H_SB // 2
    width = n_pairs * LANES
    spec = lambda t: pl.BlockSpec((1, s, width), lambda b: (b, 0, t))
    return pl.pallas_call(
        _sb_kernel,
        grid=(bn,),
        in_specs=[spec(0), spec(1), spec(2)],
        out_specs=pl.BlockSpec((1, s, width), lambda b: (b, 0, 0)),
        out_shape=jax.ShapeDtypeStruct((bn, s, width), BF16),
        scratch_shapes=[pltpu.VMEM((n_pairs, SB_ROWS, LANES), F32),
                        pltpu.VMEM((n_pairs, 2, SB_ROWS, LANES), F32)],
        compiler_params=_cparams(1),
        name="sb_attn",
    )(qkv_sb, qkv_sb, qkv_sb)


def _band_bias_tiles(bucket, table_ref, col, max_dist, sink=None):
    row = lax.broadcasted_iota(jnp.int32, (BLK, 2 * BLK), 0)
    kcol = lax.broadcasted_iota(jnp.int32, (BLK, 2 * BLK), 1)
    rel = row + BLK - kcol
    in_band = (rel >= 0) & (rel <= max_dist)
    bias = jnp.zeros((BLK, 2 * BLK), F32)
    for n in range(N_BUCKETS):
        bias = jnp.where(bucket == n, table_ref[n * N_SOFT_HEADS + col], bias)
    tiles = [jnp.where(in_band, bias, NEG_BIG), jnp.where(in_band & (kcol >= BLK), bias, NEG_BIG)]
    if sink is not None:
        tiles = [jnp.where(kcol == 0, sink, t) for t in tiles]
    return tiles


def _band_pair(q, kcat, vcat, biases):
    return _band_finish(*_band_pv(_band_scores(q, kcat, biases), vcat))


def _band_scores(q, kcat, biases):
    head0 = lax.broadcasted_iota(jnp.int32, q.shape, 1) < HEAD_DIM
    zero = jnp.zeros_like(q)
    return [_dot_nt(jnp.where(head0, q, zero) if h == 0 else jnp.where(head0, zero, q), kcat) + biases[h]
            for h in range(2)]


def _band_pv(scores, vcat):
    one = jnp.ones_like(vcat)
    v_head0 = lax.broadcasted_iota(jnp.int32, vcat.shape, 1) < HEAD_DIM
    outs, ms = [], []
    for h, s in enumerate(scores):
        vh = jnp.where(v_head0, vcat, one) if h == 0 else jnp.where(v_head0, one, vcat)
        m = jnp.max(s, axis=-1, keepdims=True)
        p = jnp.exp(s - m)
        outs.append(jnp.dot(p.astype(BF16), vh, preferred_element_type=F32))
        ms.append(m)
    return outs, ms


def _band_finish(outs, ms):
    head0 = lax.broadcasted_iota(jnp.int32, outs[0].shape, 1) < HEAD_DIM
    num = jnp.where(head0, outs[0], outs[1])
    den = pltpu.roll(jnp.where(head0, outs[1], outs[0]), HEAD_DIM, axis=1)
    return num, den, jnp.where(head0, ms[0], ms[1])


def _dil_kernel(table_ref, qkv_ref, bucket_ref, o_ref, bias_ref, og_ref, dg_ref, mg_ref, stage_ref, ostage_ref):
    s_len = o_ref.shape[1]
    n_blocks = s_len // BLK

    @pl.when(pl.program_id(0) == 0)
    def _():
        for g, (window, d) in enumerate(DIL_PATTERNS):
            for h in range(H_PER_DIL):
                head = g * H_PER_DIL + h
                normal, first = _band_bias_tiles(bucket_ref[g], table_ref, head, window // d)
                bias_ref[head, 0] = normal
                bias_ref[head, 1] = first

    for g, (window, d) in enumerate(DIL_PATTERNS):
        nb = s_len // d // BLK
        two_pass = d > MAX_STRIDE
        d_in = d // MAX_STRIDE if two_pass else d
        seg = s_len // MAX_STRIDE

        if two_pass:
            for c1 in range(MAX_STRIDE):
                for a in range(3):
                    stage_ref[a, c1 * seg:(c1 + 1) * seg, :] = \
                        qkv_ref[0, 3 * g + a, pl.ds(c1, seg, stride=MAX_STRIDE), :]

        def src(a, start):
            if d == 1:
                return qkv_ref[0, 3 * g + a, pl.ds(start, BLK), :]
            if not two_pass:
                return qkv_ref[0, 3 * g + a, pl.ds(start, BLK, stride=d), :]
            return stage_ref[a, pl.ds(staged(start), BLK, stride=d_in), :]

        def staged(start):
            return (start % MAX_STRIDE) * seg + start // MAX_STRIDE

        def put(dst_ref, ostage_slot, start, val):
            if d == 1:
                dst_ref[g, pl.ds(start, BLK), :] = val
            elif not two_pass:
                dst_ref[g, pl.ds(start, BLK, stride=d), :] = val
            else:
                ostage_ref[ostage_slot, pl.ds(staged(start), BLK, stride=d_in), :] = val

        def scores(idx):
            r, i = idx // nb, idx % nb
            start = r + d * BLK * i
            q = src(0, start).astype(BF16)
            heads = (g * H_PER_DIL, g * H_PER_DIL + 1)
            prev = r + d * BLK * jnp.maximum(i - 1, 0)
            kcat = jnp.concatenate([src(1, prev), src(1, start)], axis=0).astype(BF16)
            vcat = jnp.concatenate([src(2, prev), src(2, start)], axis=0).astype(BF16)
            variant = jnp.where(i > 0, 0, 1)
            biases = [bias_ref[hd, variant] for hd in heads]
            return start, _band_scores(q, kcat, biases), vcat

        def body(it, carry):
            staged1 = [scores(it * BAND_UNROLL + u) for u in range(BAND_UNROLL)]
            staged2 = [(start, _band_pv(s, vcat)) for start, s, vcat in staged1]
            for start, (outs, ms) in staged2:
                num, den, m = _band_finish(outs, ms)
                put(og_ref, 0, start, num)
                put(dg_ref, 1, start, den)
                put(mg_ref, 2, start, m)
            return carry

        lax.fori_loop(0, n_blocks // BAND_UNROLL, body, 0)

        if two_pass:
            for c1 in range(MAX_STRIDE):
                rows_in = slice(c1 * seg, (c1 + 1) * seg)
                rows_out = pl.ds(c1, seg, stride=MAX_STRIDE)
                og_ref[g, rows_out, :] = ostage_ref[0, rows_in, :]
                dg_ref[g, rows_out, :] = ostage_ref[1, rows_in, :]
                mg_ref[g, rows_out, :] = ostage_ref[2, rows_in, :]

    for r0 in range(0, s_len, DIL_MERGE_ROWS):
        sl = slice(r0, r0 + DIL_MERGE_ROWS)
        m0, m1, m2 = mg_ref[0, sl, :], mg_ref[1, sl, :], mg_ref[2, sl, :]
        m = jnp.maximum(jnp.maximum(m0, m1), m2)
        e0, e1, e2 = jnp.exp(m0 - m), jnp.exp(m1 - m), jnp.exp(m2 - m)
        num = e0 * og_ref[0, sl, :] + e1 * og_ref[1, sl, :] + e2 * og_ref[2, sl, :]
        den = e0 * dg_ref[0, sl, :] + e1 * dg_ref[1, sl, :] + e2 * dg_ref[2, sl, :]
        o_ref[0, sl, :] = (num / den).astype(o_ref.dtype)


def _dil_attention(qkv_dil, buckets, table):
    bn, n_slabs, s, _ = qkv_dil.shape
    n_groups = len(DIL_PATTERNS)
    return pl.pallas_call(
        _dil_kernel,
        grid_spec=pltpu.PrefetchScalarGridSpec(
            num_scalar_prefetch=1,
            grid=(bn,),
            in_specs=[pl.BlockSpec((1, n_slabs, s, LANES), lambda b, tb: (b, 0, 0, 0)),
                      pl.BlockSpec(buckets.shape, lambda b, tb: (0, 0, 0))],
            out_specs=pl.BlockSpec((1, s, LANES), lambda b, tb: (b, 0, 0)),
            scratch_shapes=[pltpu.VMEM((H_DIL, 2, BLK, 2 * BLK), F32),
                            pltpu.VMEM((n_groups, s, LANES), F32),
                            pltpu.VMEM((n_groups, s, LANES), F32),
                            pltpu.VMEM((n_groups, s, LANES), F32),
                            pltpu.VMEM((3, s, LANES), F32),
                            pltpu.VMEM((3, s, LANES), F32)],
        ),
        out_shape=jax.ShapeDtypeStruct((bn, s, LANES), BF16),
        compiler_params=pltpu.CompilerParams(dimension_semantics=("arbitrary",),
                                             vmem_limit_bytes=VMEM_LIMIT),
        name="dil_attn",
    )(table, qkv_dil, buckets)


def _swa_kernel(table_ref, sink_ref, qkv_ref, bucket_ref, o_ref, bias_ref):
    s_len = o_ref.shape[1]
    k_col = N_SWA_PAIRS * LANES
    v_col = k_col + LANES
    key_row = lax.broadcasted_iota(jnp.int32, (2 * BLK, LANES), 0)

    @pl.when(pl.program_id(0) == 0)
    def _():
        for slot, hd in enumerate(_SWA_HEAD_ORDER):
            normal, first = _band_bias_tiles(bucket_ref[0], table_ref, H_DIL + hd, SWA_WINDOW - 1, sink_ref[hd])
            bias_ref[slot, 0] = normal
            bias_ref[slot, 1] = first

    def scores(i):
        start = pl.multiple_of(i * BLK, BLK)
        prev = pl.multiple_of(jnp.maximum(i - 1, 0) * BLK, BLK)
        kcat = jnp.concatenate([qkv_ref[0, pl.ds(prev, BLK), k_col:k_col + LANES],
                                qkv_ref[0, pl.ds(start, BLK), k_col:k_col + LANES]], axis=0)
        vcat = jnp.concatenate([qkv_ref[0, pl.ds(prev, BLK), v_col:v_col + LANES],
                                qkv_ref[0, pl.ds(start, BLK), v_col:v_col + LANES]], axis=0)
        kcat = jnp.where(key_row == 0, jnp.zeros_like(kcat), kcat)
        vcat = jnp.where(key_row == 0, jnp.zeros_like(vcat), vcat)
        variant = jnp.where(i > 0, 0, 1)
        out = []
        for p in range(N_SWA_PAIRS):
            q = qkv_ref[0, pl.ds(start, BLK), p * LANES:(p + 1) * LANES]
            biases = [bias_ref[2 * p + h, variant] for h in range(2)]
            out.append((start, p, _band_scores(q, kcat, biases), vcat))
        return out

    def body(it, carry):
        staged1 = [unit for u in range(SWA_UNROLL) for unit in scores(it * SWA_UNROLL + u)]
        staged2 = [(start, p, _band_pv(s, vcat)) for start, p, s, vcat in staged1]
        for start, p, (outs, ms) in staged2:
            num, den, _ = _band_finish(outs, ms)
            o_ref[0, pl.ds(start, BLK), p * LANES:(p + 1) * LANES] = (num * (1.0 / den)).astype(o_ref.dtype)
        return carry

    lax.fori_loop(0, s_len // BLK // SWA_UNROLL, body, 0)


def _swa_attention(qkv_swa, bucket, table, sinks_l):
    bn, s, n = qkv_swa.shape
    return pl.pallas_call(
        _swa_kernel,
        grid_spec=pltpu.PrefetchScalarGridSpec(
            num_scalar_prefetch=2,
            grid=(bn,),
            in_specs=[pl.BlockSpec((1, s, n), lambda b, tb, sk: (b, 0, 0)),
                      pl.BlockSpec(bucket.shape, lambda b, tb, sk: (0, 0, 0))],
            out_specs=pl.BlockSpec((1, s, N_SWA_PAIRS * LANES), lambda b, tb, sk: (b, 0, 0)),
            scratch_shapes=[pltpu.VMEM((H_SWA_Q, 2, BLK, 2 * BLK), F32)],
        ),
        out_shape=jax.ShapeDtypeStruct((bn, s, N_SWA_PAIRS * LANES), BF16),
        compiler_params=pltpu.CompilerParams(dimension_semantics=("arbitrary",),
                                             vmem_limit_bytes=VMEM_LIMIT),
        name="swa_attn",
    )(table, sinks_l, qkv_swa, bucket)


def _merge_kernel(x_ref, h_ref, mod_ref, osb_ref, odil_ref, oswa_ref,
                  win_hbm, wsb_ref, wdil_ref, wswa_ref, wout_ref, o_ref, wgate_ref, stage_ref, sem_ref, *, layer):
    d = x_ref.shape[2]
    gate0 = win_hbm.shape[2] - 3 * d
    chunks_per_gate = d // W_COLS
    branches = ((osb_ref, wsb_ref), (odil_ref, wdil_ref), (oswa_ref, wswa_ref))
    first = (pl.program_id(0) == 0) & (pl.program_id(1) == 0)

    def compute(stream_gates):
        if stream_gates:
            prime, take = _column_streamer(win_hbm.at[layer], gate0, 3 * chunks_per_gate, stage_ref, sem_ref)
            prime()
        x = x_ref[0]
        h = h_ref[0]
        merged = None
        for t, (o_br, w_br) in enumerate(branches):
            if stream_gates:
                for c in range(t * chunks_per_gate, (t + 1) * chunks_per_gate):
                    def consume(st, c=c):
                        wgate_ref[:, c * W_COLS:(c + 1) * W_COLS] = st[...].astype(BF16)
                    take(c, consume)
            gate = jax.nn.sigmoid(jnp.dot(h, wgate_ref[:, t * d:(t + 1) * d], preferred_element_type=F32))
            term = gate * jnp.dot(o_br[0], w_br[...], preferred_element_type=F32)
            merged = term if merged is None else merged + term
        y = jnp.dot(merged.astype(BF16), wout_ref[...], preferred_element_type=F32)
        o_ref[0] = x + mod_ref[0][5:6] * y

    @pl.when(first)
    def _():
        compute(True)

    @pl.when(jnp.logical_not(first))
    def _():
        compute(False)


def _merge(x, h, mod, o_sb, o_dil, o_swa, w_in, w_sb, w_dil, w_swa, w_out, layer):
    bn, s, d = x.shape
    row = lambda n: pl.BlockSpec((1, ROW_TILE, n), lambda b, i: (b, i, 0))
    return pl.pallas_call(
        functools.partial(_merge_kernel, layer=layer),
        grid=(bn, s // ROW_TILE),
        in_specs=[row(d), row(d),
                  pl.BlockSpec((1, ADA_CHUNKS, d), lambda b, i: (b, 0, 0)),
                  row(o_sb.shape[2]), row(o_dil.shape[2]), row(o_swa.shape[2]),
                  pl.BlockSpec(memory_space=pl.ANY), _slab_spec(w_sb, layer), _slab_spec(w_dil, layer),
                  _const_spec(w_swa.shape), _slab_spec(w_out, layer)],
        out_specs=row(d),
        out_shape=jax.ShapeDtypeStruct(x.shape, F32),
        scratch_shapes=[pltpu.VMEM((d, 3 * d), BF16), pltpu.VMEM((2, d, W_COLS), F32),
                        pltpu.SemaphoreType.DMA((2,))],
        compiler_params=pltpu.CompilerParams(dimension_semantics=("arbitrary", "arbitrary"),
                                             vmem_limit_bytes=VMEM_LIMIT),
        name="merge",
    )(x, h, mod, o_sb, o_dil, o_swa, w_in, w_sb, w_dil, w_swa, w_out)


def _t5_bucket(n, dtype):
    max_exact = N_BUCKETS // 2
    nf = np.maximum(n, 1).astype(dtype)
    scaled = np.log(nf / dtype(max_exact)) / dtype(math.log(MAX_REL_DIST / max_exact)) * dtype(N_BUCKETS - max_exact)
    large = np.minimum(max_exact + scaled.astype(np.int32), N_BUCKETS - 1)
    return np.where(n < max_exact, n, large).astype(np.int32)


def _band_buckets(dilation):
    rel = np.arange(BLK)[:, None] + BLK - np.arange(2 * BLK)[None, :]
    n = np.maximum(rel, 0) * dilation
    buckets = _t5_bucket(n, np.float32)
    assert np.array_equal(buckets, _t5_bucket(n, np.float64)), "relative-position bucket on a rounding boundary"
    return buckets


def _reorder_heads(w, order, axis):
    parts = [lax.slice_in_dim(w, hd * HEAD_DIM, (hd + 1) * HEAD_DIM, axis=axis) for hd in order]
    return jnp.concatenate(parts, axis=axis)


def kernel(x, c, w_ada, b_ada, norm_gain, w_ffn_gate, w_ffn_up, w_ffn_down, w_in,
           w_br_sb, w_br_dil, w_br_swa, w_out, sinks, rel_bias, final_gain):
    depth = w_ada.shape[0]
    bn, s, d = x.shape
    mods = _ada(c, w_ada, b_ada).reshape(depth, bn, ADA_CHUNKS, d)

    buckets = jnp.asarray(np.stack([_band_buckets(dil) for _, dil in DIL_PATTERNS]))
    table = rel_bias.reshape(-1)
    wg, wu, wd = w_ffn_gate, w_ffn_up, w_ffn_down
    w_sb_bf, w_dil_bf, w_out_bf = w_br_sb.astype(BF16), w_br_dil.astype(BF16), w_out.astype(BF16)

    for l in range(depth):
        mod = mods[l]
        x = _ffn(x, mod, norm_gain[l, 0], wg, wu, wd, l, 0, 0)

        qkv_sb, qkv_dil, qkv_swa, h_mix = _inproj(x, mod, norm_gain[l, 1], w_in, l)
        o_sb = _sb_attention(qkv_sb)
        o_dil = _dil_attention(qkv_dil, buckets, table)
        o_swa = _swa_attention(qkv_swa, buckets[:1], table, sinks[l])
        w_swa = _reorder_heads(w_br_swa[l], _SWA_HEAD_ORDER, 0).astype(BF16)
        x = _merge(x, h_mix, mod, o_sb, o_dil, o_swa, w_in, w_sb_bf, w_dil_bf, w_swa, w_out_bf, l)

        last = l == depth - 1
        x = _ffn(x, mod, norm_gain[l, 2], wg, wu, wd, l, 1, 2, final_gain=final_gain if last else None)
    return x
```

```python
import functools
import math

import numpy as np

import jax
import jax.numpy as jnp
from jax import lax
from jax.experimental import pallas as pl
from jax.experimental.pallas import tpu as pltpu

F32 = jnp.float32
BF16 = jnp.bfloat16

HEAD_DIM = 64
LANES = 128
BLK = 128
H_SB = 4
DIL_PATTERNS = ((128, 1), (512, 4), (2048, 16))
H_PER_DIL = 2
H_DIL = H_PER_DIL * len(DIL_PATTERNS)
H_SWA_Q = 6
H_SWA_KV = 2
SWA_WINDOW = 128
N_SOFT_HEADS = H_DIL + H_SWA_Q
N_SWA_PAIRS = H_SWA_Q // 2
N_BUCKETS = 32
MAX_REL_DIST = 2048
RMS_EPS = 1e-6
ADA_CHUNKS = 9
NEG_BIG = -1e30
LOG2E = 1.4426950408889634

VMEM_LIMIT = 56 * 1024 * 1024
ROW_TILE = 512
FF_CHUNK = 256
ADA_TILE = 3072
SB_ROWS = 512
SB_KEY_BLOCKS = 2
SB_DEAD_LOG2 = -150.0
BAND_UNROLL = 16
DIL_MERGE_ROWS = 512
MAX_STRIDE = 4
SWA_UNROLL = 16

_SWA_HEAD_ORDER = tuple(p + (H_SWA_Q // H_SWA_KV) * h for p in range(N_SWA_PAIRS) for h in range(2))


def _cparams(n_axes):
    return pltpu.CompilerParams(dimension_semantics=("parallel",) * n_axes,
                                vmem_limit_bytes=VMEM_LIMIT)


def _const_spec(shape):
    nd = len(shape)
    return pl.BlockSpec(shape, lambda *_: (0,) * nd, pipeline_mode=pl.Buffered(1))


def _slab_spec(arr, *lead):
    block = (None,) * len(lead) + tuple(arr.shape[len(lead):])
    return pl.BlockSpec(block, lambda *_: tuple(lead) + (0, 0), pipeline_mode=pl.Buffered(1))


def _ada_kernel(c_ref, w_ref, b_ref, o_ref):
    c = c_ref[...]
    sc = (c * jax.nn.sigmoid(c)).astype(BF16)
    o_ref[0] = jnp.dot(sc, w_ref[0].astype(BF16), preferred_element_type=F32) + b_ref[0]


def _ada(c, w_ada, b_ada):
    depth, d, n = w_ada.shape
    bn = c.shape[0]
    return pl.pallas_call(
        _ada_kernel,
        grid=(depth, n // ADA_TILE),
        in_specs=[pl.BlockSpec((bn, d), lambda l, j: (0, 0)),
                  pl.BlockSpec((1, d, ADA_TILE), lambda l, j: (l, 0, j)),
                  pl.BlockSpec((1, 1, ADA_TILE), lambda l, j: (l, 0, j))],
        out_specs=pl.BlockSpec((1, bn, ADA_TILE), lambda l, j: (l, 0, j)),
        out_shape=jax.ShapeDtypeStruct((depth, bn, n), F32),
        compiler_params=_cparams(2),
        name="ada_mod",
    )(c, w_ada, b_ada.reshape(depth, 1, n))


def _rms(x, gain):
    ms = jnp.mean(x * x, axis=-1, keepdims=True)
    return x * lax.rsqrt(ms + RMS_EPS) * gain


def _prenorm(x, gain, mod, j):
    shift = mod[3 * j:3 * j + 1]
    scale = mod[3 * j + 1:3 * j + 2]
    ms = jnp.mean(x * x, axis=-1, keepdims=True)
    return (x * lax.rsqrt(ms + RMS_EPS)) * (gain * (1.0 + scale)) + shift


def _dot_nt(a, b):
    return lax.dot_general(a, b, (((1,), (1,)), ((), ())), preferred_element_type=F32)


def _ffn_kernel(*refs, j, final, layer, which):
    n_in = 7 if final else 6
    x_ref, mod_ref, gain_ref, wg_hbm, wu_hbm, wd_hbm = refs[:6]
    fg_ref = refs[6] if final else None
    o_ref, wg_ref, wu_ref, wd_ref, stage_g_ref, stage_u_ref, stage_d_ref, sem_ref = refs[n_in:]
    n_chunks = wg_ref.shape[1] // FF_CHUNK
    chunk = lambda ci: slice(ci * FF_CHUNK, (ci + 1) * FF_CHUNK)
    first = (pl.program_id(0) == 0) & (pl.program_id(1) == 0)

    def chunk_copies(ci):
        slot = ci % 2
        return [pltpu.make_async_copy(wg_hbm.at[layer, which, :, chunk(ci)], stage_g_ref.at[slot], sem_ref.at[slot, 0]),
                pltpu.make_async_copy(wu_hbm.at[layer, which, :, chunk(ci)], stage_u_ref.at[slot], sem_ref.at[slot, 1]),
                pltpu.make_async_copy(wd_hbm.at[layer, which, chunk(ci), :], stage_d_ref.at[slot], sem_ref.at[slot, 2])]

    def compute(stream_weights):
        if stream_weights:
            for ci in range(min(2, n_chunks)):
                for cp in chunk_copies(ci):
                    cp.start()
        x = x_ref[0]
        mod = mod_ref[0]
        h = _prenorm(x, gain_ref[...], mod, j).astype(BF16)
        acc = jnp.zeros(x.shape, F32)
        for ci in range(n_chunks):
            if stream_weights:
                for cp in chunk_copies(ci):
                    cp.wait()
                wg_ref[:, chunk(ci)] = stage_g_ref[ci % 2].astype(BF16)
                wu_ref[:, chunk(ci)] = stage_u_ref[ci % 2].astype(BF16)
                wd_ref[chunk(ci), :] = stage_d_ref[ci % 2].astype(BF16)
                if ci + 2 < n_chunks:
                    for cp in chunk_copies(ci + 2):
                        cp.start()
            g = jnp.dot(h, wg_ref[:, chunk(ci)], preferred_element_type=F32)
            u = jnp.dot(h, wu_ref[:, chunk(ci)], preferred_element_type=F32)
            a = (g * jax.nn.sigmoid(g) * u).astype(BF16)
            acc = acc + jnp.dot(a, wd_ref[chunk(ci), :], preferred_element_type=F32)
        y = x + (0.5 * mod[3 * j + 2:3 * j + 3]) * acc
        if final:
            y = _rms(y, fg_ref[...])
        o_ref[0] = y

    @pl.when(first)
    def _():
        compute(True)

    @pl.when(jnp.logical_not(first))
    def _():
        compute(False)


def _ffn(x, mod, gain, wg, wu, wd, layer, which, j, final_gain=None):
    bn, s, d = x.shape
    d_ff = wg.shape[-1]
    final = final_gain is not None
    hbm = pl.BlockSpec(memory_space=pl.ANY)
    in_specs = [pl.BlockSpec((1, ROW_TILE, d), lambda b, i: (b, i, 0)),
                pl.BlockSpec((1, ADA_CHUNKS, d), lambda b, i: (b, 0, 0)),
                _const_spec((1, d)), hbm, hbm, hbm]
    args = [x, mod, gain.reshape(1, d), wg, wu, wd]
    if final:
        in_specs.append(_const_spec((1, d)))
        args.append(final_gain.reshape(1, d))
    return pl.pallas_call(
        functools.partial(_ffn_kernel, j=j, final=final, layer=layer, which=which),
        grid=(bn, s // ROW_TILE),
        in_specs=in_specs,
        out_specs=pl.BlockSpec((1, ROW_TILE, d), lambda b, i: (b, i, 0)),
        out_shape=jax.ShapeDtypeStruct(x.shape, F32),
        scratch_shapes=[pltpu.VMEM((d, d_ff), BF16), pltpu.VMEM((d, d_ff), BF16), pltpu.VMEM((d_ff, d), BF16),
                        pltpu.VMEM((2, d, FF_CHUNK), F32), pltpu.VMEM((2, d, FF_CHUNK), F32),
                        pltpu.VMEM((2, FF_CHUNK, d), F32), pltpu.SemaphoreType.DMA((2, 3))],
        compiler_params=pltpu.CompilerParams(dimension_semantics=("arbitrary", "arbitrary"),
                                             vmem_limit_bytes=VMEM_LIMIT),
        name="ffn_final" if final else f"ffn{j}",
    )(*args)


N_SB = 3 * H_SB * HEAD_DIM
N_DIL = 3 * H_DIL * HEAD_DIM
N_SWA = (H_SWA_Q + 2 * H_SWA_KV) * HEAD_DIM
N_QKV = N_SB + N_DIL + N_SWA
N_DIL_SLABS = N_DIL // LANES


W_COLS = 256
N_INPROJ_HEAD = 7 * W_COLS


def _column_streamer(w_hbm, first_col, n_chunks, stage_ref, sem_ref):
    def copy(c):
        return pltpu.make_async_copy(w_hbm.at[:, pl.ds(first_col + c * W_COLS, W_COLS)],
                                     stage_ref.at[c % 2], sem_ref.at[c % 2])

    def prime():
        for c in range(min(2, n_chunks)):
            copy(c).start()

    def take(c, consume):
        copy(c).wait()
        consume(stage_ref.at[c % 2])
        if c + 2 < n_chunks:
            copy(c + 2).start()

    return prime, take


def _inproj_tail_col(col):
    q0 = N_SB + N_DIL
    q1 = q0 + H_SWA_Q * HEAD_DIM
    if col < q0:
        return col - N_INPROJ_HEAD
    if col < q1:
        return (q0 - N_INPROJ_HEAD) + _SWA_HEAD_ORDER.index((col - q0) // HEAD_DIM) * HEAD_DIM
    return col - N_INPROJ_HEAD


def _inproj_kernel(x_ref, mod_ref, gain_ref, w_hbm, sb_ref, dil_ref, swa_ref, hout_ref,
                   whead_ref, wtail_ref, stage_ref, sem_ref, *, layer):
    qs = HEAD_DIM ** -0.5
    n_q_sb = H_SB * HEAD_DIM
    n_groups = len(DIL_PATTERNS)
    n_q_swa = H_SWA_Q * HEAD_DIM
    q0 = N_SB + N_DIL
    n_head_chunks = N_INPROJ_HEAD // W_COLS
    first = (pl.program_id(0) == 0) & (pl.program_id(1) == 0)

    def emit(h, r):
        hout_ref[0] = h
        sb_ref[0, :, :n_q_sb] = (r[:, :n_q_sb] * (qs * LOG2E)).astype(BF16)
        sb_ref[0, :, n_q_sb:] = r[:, n_q_sb:N_SB].astype(BF16)
        for a in range(3):
            for g in range(n_groups):
                c0 = N_SB + (a * n_groups + g) * LANES
                slab = r[:, c0:c0 + LANES]
                dil_ref[0, 3 * g + a] = slab * qs if a == 0 else slab
        swa_ref[0, :, :n_q_swa] = (r[:, q0:q0 + n_q_swa] * qs).astype(BF16)
        swa_ref[0, :, n_q_swa:] = r[:, q0 + n_q_swa:].astype(BF16)

    @pl.when(first)
    def _():
        prime, take = _column_streamer(w_hbm.at[layer], 0, N_QKV // W_COLS, stage_ref, sem_ref)
        prime()
        h = _prenorm(x_ref[0], gain_ref[...], mod_ref[0], 1).astype(BF16)
        parts = []
        for c in range(N_QKV // W_COLS):
            cols = slice(c * W_COLS, (c + 1) * W_COLS)
            if c < n_head_chunks:
                def consume(st, cols=cols):
                    whead_ref[:, cols] = st[...].astype(BF16)
                take(c, consume)
                parts.append(jnp.dot(h, whead_ref[:, cols], preferred_element_type=F32))
            else:
                def consume(st, c=c):
                    for piece in range(W_COLS // HEAD_DIM):
                        dst = _inproj_tail_col(c * W_COLS + piece * HEAD_DIM)
                        wtail_ref[:, dst:dst + HEAD_DIM] = \
                            st[:, piece * HEAD_DIM:(piece + 1) * HEAD_DIM].astype(BF16)
                take(c, consume)
        parts.append(jnp.dot(h, wtail_ref[...], preferred_element_type=F32))
        emit(h, jnp.concatenate(parts, axis=1))

    @pl.when(jnp.logical_not(first))
    def _():
        h = _prenorm(x_ref[0], gain_ref[...], mod_ref[0], 1).astype(BF16)
        emit(h, jnp.concatenate([jnp.dot(h, whead_ref[...], preferred_element_type=F32),
                                 jnp.dot(h, wtail_ref[...], preferred_element_type=F32)], axis=1))


def _inproj(x, mod, gain, w_in, layer):
    bn, s, d = x.shape
    nt = s // ROW_TILE
    n_tail = N_QKV - N_INPROJ_HEAD
    return pl.pallas_call(
        functools.partial(_inproj_kernel, layer=layer),
        grid=(bn, nt),
        in_specs=[pl.BlockSpec((1, ROW_TILE, d), lambda b, i: (b, i, 0)),
                  pl.BlockSpec((1, ADA_CHUNKS, d), lambda b, i: (b, 0, 0)),
                  _const_spec((1, d)), pl.BlockSpec(memory_space=pl.ANY)],
        out_specs=[pl.BlockSpec((1, ROW_TILE, N_SB), lambda b, i: (b, i, 0)),
                   pl.BlockSpec((1, N_DIL_SLABS, ROW_TILE, LANES), lambda b, i: (b, 0, i, 0)),
                   pl.BlockSpec((1, ROW_TILE, N_SWA), lambda b, i: (b, i, 0)),
                   pl.BlockSpec((1, ROW_TILE, d), lambda b, i: (b, i, 0))],
        out_shape=[jax.ShapeDtypeStruct((bn, s, N_SB), BF16),
                   jax.ShapeDtypeStruct((bn, N_DIL_SLABS, s, LANES), F32),
                   jax.ShapeDtypeStruct((bn, s, N_SWA), BF16),
                   jax.ShapeDtypeStruct((bn, s, d), BF16)],
        scratch_shapes=[pltpu.VMEM((d, N_INPROJ_HEAD), BF16), pltpu.VMEM((d, n_tail), BF16),
                        pltpu.VMEM((2, d, W_COLS), F32), pltpu.SemaphoreType.DMA((2,))],
        compiler_params=pltpu.CompilerParams(dimension_semantics=("arbitrary", "arbitrary"),
                                             vmem_limit_bytes=VMEM_LIMIT),
        name="in_proj",
    )(x, mod, gain.reshape(1, d), w_in)


def _sb_kernel(q_ref, k_ref, v_ref, o_ref, acc_ref, c_ref):
    s_len = q_ref.shape[1]
    n_pairs = q_ref.shape[2] // LANES
    per = SB_ROWS // BLK
    lane = lax.broadcasted_iota(jnp.int32, (BLK, LANES), 1)
    head0 = lane < HEAD_DIM
    row2 =lax.broadcasted_iota(jnp.int32, (BLK, 2 * BLK), 0)
    col2 = lax.broadcasted_iota(jnp.int32, (BLK, 2 * BLK), 1)
    strict2 = (col2 & (BLK - 1)) < row2
    rk = lax.broadcasted_iota(jnp.int32, (2 * BLK, 2 * BLK), 0)
    cn = lax.broadcasted_iota(jnp.int32, (2 * BLK, 2 * BLK), 1)
    neg_suffix_ones = jnp.where(((rk & (BLK - 1)) >= cn) | (cn >= BLK), -1.0, 0.0).astype(BF16)

    def by_head(t):
        zero = jnp.zeros_like(t)
        return jnp.concatenate([jnp.where(head0, t, zero), jnp.where(head0, zero, t)], axis=0)

    def mask_first_block(t):
        first = jnp.where(strict2, t[:BLK], 0.0)
        return first if t.shape[0] == BLK else jnp.concatenate([first, t[BLK:]], axis=0)

    def pair_lanes(p):
        return slice(p * LANES, (p + 1) * LANES)

    def scores(p, q, k0):
        return _dot_nt(q, by_head(k_ref[0, pl.ds(k0, BLK), pair_lanes(p)]))

    def carries(z, diag):
        softplus2 = jnp.maximum(z, 0.0) + jnp.log2(1.0 + jnp.exp2(-jnp.abs(z)))
        if diag:
            softplus2 = mask_first_block(softplus2)
        hi = softplus2.astype(BF16)
        lo = (softplus2 - hi.astype(F32)).astype(BF16)
        return [jnp.dot(jnp.concatenate([hi[:, h * BLK:(h + 1) * BLK], lo[:, h * BLK:(h + 1) * BLK]], axis=1),
                        neg_suffix_ones, preferred_element_type=F32) for h in range(2)]

    def weighted_values(p, z, cts, c, k0, diag):
        ws = [jnp.exp2(z[:, h * BLK:(h + 1) * BLK] + cts[h][:, :BLK] + c[h]) for h in range(2)]
        w = jnp.concatenate(ws, axis=1)
        if diag:
            w = mask_first_block(w)
        vv = by_head(v_ref[0, pl.ds(k0, BLK), pair_lanes(p)])
        return jnp.dot(w.astype(BF16), vv, preferred_element_type=F32)

    def superblock(sb, carry):
        base = pl.multiple_of(sb * SB_ROWS, SB_ROWS)
        acc_ref[...] = jnp.zeros_like(acc_ref)
        c_ref[...] = jnp.zeros_like(c_ref)

        def run(units):
            zs = [scores(p, q_ref[0, pl.ds(base + lo, n), pair_lanes(p)], k0) for p, lo, n, k0, _ in units]
            cts = [carries(z, unit[4]) for z, unit in zip(zs, units)]
            for z, ct, (p, lo, n, k0, diag) in zip(zs, cts, units):
                rows = slice(lo, lo + n)
                c = (c_ref[p, 0, rows, :], c_ref[p, 1, rows, :])
                acc_ref[p, rows, :] += weighted_values(p, z, ct, c, k0, diag)
                c_ref[p, 0, rows, :] = c[0] + ct[0][:, BLK:]
                c_ref[p, 1, rows, :] = c[1] + ct[1][:, BLK:]

        def alive(lo, n):
            c_max = None
            for p in range(n_pairs):
                for h in range(2):
                    c = c_ref[p, h, lo:lo + n, :]
                    c_max = c if c_max is None else jnp.maximum(c_max, c)
            return jnp.max(c_max) > SB_DEAD_LOG2

        run([(p, kq * BLK, SB_ROWS - kq * BLK, base + kq * BLK, True)
             for kq in reversed(range(per)) for p in range(n_pairs)])

        n_iter = sb * (per // SB_KEY_BLOCKS)

        half = SB_ROWS // 2

        def sweep(window, flag, state):
            def kstep(st):
                run([(p, window[0], window[1],
                      pl.multiple_of(base - (SB_KEY_BLOCKS * st[0] + u + 1) * BLK, BLK), False)
                     for u in range(SB_KEY_BLOCKS) for p in range(n_pairs)])
                return st[0] + 1, alive(half, half), alive(0, half)

            return lax.while_loop(lambda st: jnp.logical_and(st[0] < n_iter, st[flag]), kstep, state)

        state = sweep((0, SB_ROWS), 1, (jnp.int32(0), alive(half, half), alive(0, half)))
        sweep((0, half), 2, state)
        for p in range(n_pairs):
            o_ref[0, pl.ds(base, SB_ROWS), pair_lanes(p)] = acc_ref[p].astype(o_ref.dtype)
        return carry

    lax.fori_loop(0, s_len // SB_ROWS, superblock, 0)


def _sb_attention(qkv_sb):
    bn, s, _ = qkv_sb.shape
    n_pairs = H_SB // 2
    width = n_pairs * LANES
    spec = lambda t: pl.BlockSpec((1, s, width), lambda b: (b, 0, t))
    return pl.pallas_call(
        _sb_kernel,
        grid=(bn,),
        in_specs=[spec(0), spec(1), spec(2)],
        out_specs=pl.BlockSpec((1, s, width), lambda b: (b, 0, 0)),
        out_shape=jax.ShapeDtypeStruct((bn, s, width), BF16),
        scratch_shapes=[pltpu.VMEM((n_pairs, SB_ROWS, LANES), F32),
                        pltpu.VMEM((n_pairs, 2, SB_ROWS, LANES), F32)],
        compiler_params=_cparams(1),
        name="sb_attn",
    )(qkv_sb, qkv_sb, qkv_sb)


def _band_bias_tiles(bucket, table_ref, col, max_dist, sink=None):
    row = lax.broadcasted_iota(jnp.int32, (BLK, 2 * BLK), 0)
    kcol = lax.broadcasted_iota(jnp.int32, (BLK, 2 * BLK), 1)
    rel = row + BLK - kcol
    in_band = (rel >= 0) & (rel <= max_dist)
    bias = jnp.zeros((BLK, 2 * BLK), F32)
    for n in range(N_BUCKETS):
        bias = jnp.where(bucket == n, table_ref[n * N_SOFT_HEADS + col], bias)
    tiles = [jnp.where(in_band, bias, NEG_BIG), jnp.where(in_band & (kcol >= BLK), bias, NEG_BIG)]
    if sink is not None:
        tiles = [jnp.where(kcol == 0, sink, t) for t in tiles]
    return tiles


def _band_scores(q, kcat, biases):
    head0 = lax.broadcasted_iota(jnp.int32, q.shape, 1) < HEAD_DIM
    zero = jnp.zeros_like(q)
    return [_dot_nt(jnp.where(head0, q, zero) if h == 0 else jnp.where(head0, zero, q), kcat) + biases[h]
            for h in range(2)]


def _band_pv(scores, vcat):
    one = jnp.ones_like(vcat)
    v_head0 = lax.broadcasted_iota(jnp.int32, vcat.shape, 1) < HEAD_DIM
    outs, ms = [], []
    for h, s in enumerate(scores):
        vh = jnp.where(v_head0, vcat, one) if h == 0 else jnp.where(v_head0, one, vcat)
        m = jnp.max(s, axis=-1, keepdims=True)
        p = jnp.exp(s - m)
        outs.append(jnp.dot(p.astype(BF16), vh, preferred_element_type=F32))
        ms.append(m)
    return outs, ms


def _band_finish(outs, ms):
    head0 = lax.broadcasted_iota(jnp.int32, outs[0].shape, 1) < HEAD_DIM
    num = jnp.where(head0, outs[0], outs[1])
    den = pltpu.roll(jnp.where(head0, outs[1], outs[0]), HEAD_DIM, axis=1)
    return num, den, jnp.where(head0, ms[0], ms[1])


def _dil_kernel(table_ref, qkv_ref, bucket_ref, o_ref, bias_ref, og_ref, dg_ref, mg_ref, stage_ref, ostage_ref):
    s_len = o_ref.shape[1]
    n_blocks = s_len // BLK

    @pl.when(pl.program_id(0) == 0)
    def _():
        for g, (window, d) in enumerate(DIL_PATTERNS):
            for h in range(H_PER_DIL):
                head = g * H_PER_DIL + h
                normal, first = _band_bias_tiles(bucket_ref[g], table_ref, head, window // d)
                bias_ref[head, 0] = normal
                bias_ref[head, 1] = first

    for g, (window, d) in enumerate(DIL_PATTERNS):
        nb = s_len // d // BLK
        two_pass = d > MAX_STRIDE
        d_in = d // MAX_STRIDE if two_pass else d
        seg = s_len // MAX_STRIDE

        if two_pass:
            for c1 in range(MAX_STRIDE):
                for a in range(3):
                    stage_ref[a, c1 * seg:(c1 + 1) * seg, :] = \
                        qkv_ref[0, 3 * g + a, pl.ds(c1, seg, stride=MAX_STRIDE), :]

        def src(a, start):
            if d == 1:
                return qkv_ref[0, 3 * g + a, pl.ds(start, BLK), :]
            if not two_pass:
                return qkv_ref[0, 3 * g + a, pl.ds(start, BLK, stride=d), :]
            return stage_ref[a, pl.ds(staged(start), BLK, stride=d_in), :]

        def staged(start):
            return (start % MAX_STRIDE) * seg + start // MAX_STRIDE

        def put(dst_ref, ostage_slot, start, val):
            if d == 1:
                dst_ref[g, pl.ds(start, BLK), :] = val
            elif not two_pass:
                dst_ref[g, pl.ds(start, BLK, stride=d), :] = val
            else:
                ostage_ref[ostage_slot, pl.ds(staged(start), BLK, stride=d_in), :] = val

        def scores(idx):
            r, i = idx // nb, idx % nb
            start = r + d * BLK * i
            q = src(0, start).astype(BF16)
            heads = (g * H_PER_DIL, g * H_PER_DIL + 1)
            prev = r + d * BLK * jnp.maximum(i - 1, 0)
            kcat = jnp.concatenate([src(1, prev), src(1, start)], axis=0).astype(BF16)
            vcat = jnp.concatenate([src(2, prev), src(2, start)], axis=0).astype(BF16)
            variant = jnp.where(i > 0, 0, 1)
            biases = [bias_ref[hd, variant] for hd in heads]
            return start, _band_scores(q, kcat, biases), vcat

        def body(it, carry):
            staged1 = [scores(it * BAND_UNROLL + u) for u in range(BAND_UNROLL)]
            staged2 = [(start, _band_pv(s, vcat)) for start, s, vcat in staged1]
            for start, (outs, ms) in staged2:
                num, den, m = _band_finish(outs, ms)
                put(og_ref, 0, start, num)
                put(dg_ref, 1, start, den)
                put(mg_ref, 2, start, m)
            return carry

        lax.fori_loop(0, n_blocks // BAND_UNROLL, body, 0)

        if two_pass:
            for c1 in range(MAX_STRIDE):
                rows_in = slice(c1 * seg, (c1 + 1) * seg)
                rows_out = pl.ds(c1, seg, stride=MAX_STRIDE)
                og_ref[g, rows_out, :] = ostage_ref[0, rows_in, :]
                dg_ref[g, rows_out, :] = ostage_ref[1, rows_in, :]
                mg_ref[g, rows_out, :] = ostage_ref[2, rows_in, :]

    for r0 in range(0, s_len, DIL_MERGE_ROWS):
        sl = slice(r0, r0 + DIL_MERGE_ROWS)
        m0, m1, m2 = mg_ref[0, sl, :], mg_ref[1, sl, :], mg_ref[2, sl, :]
        m = jnp.maximum(jnp.maximum(m0, m1), m2)
        e0, e1, e2 = jnp.exp(m0 - m), jnp.exp(m1 - m), jnp.exp(m2 - m)
        num = e0 * og_ref[0, sl, :] + e1 * og_ref[1, sl, :] + e2 * og_ref[2, sl, :]
        den = e0 * dg_ref[0, sl, :] + e1 * dg_ref[1, sl, :] + e2 * dg_ref[2, sl, :]
        o_ref[0, sl, :] = (num / den).astype(o_ref.dtype)


def _dil_attention(qkv_dil, buckets, table):
    bn, n_slabs, s, _ = qkv_dil.shape
    n_groups = len(DIL_PATTERNS)
    return pl.pallas_call(
        _dil_kernel,
        grid_spec=pltpu.PrefetchScalarGridSpec(
            num_scalar_prefetch=1,
            grid=(bn,),
            in_specs=[pl.BlockSpec((1, n_slabs, s, LANES), lambda b, tb: (b, 0, 0, 0)),
                      pl.BlockSpec(buckets.shape, lambda b, tb: (0, 0, 0))],
            out_specs=pl.BlockSpec((1, s, LANES), lambda b, tb: (b, 0, 0)),
            scratch_shapes=[pltpu.VMEM((H_DIL, 2, BLK, 2 * BLK), F32),
                            pltpu.VMEM((n_groups, s, LANES), F32),
                            pltpu.VMEM((n_groups, s, LANES), F32),
                            pltpu.VMEM((n_groups, s, LANES), F32),
                            pltpu.VMEM((3, s, LANES), F32),
                            pltpu.VMEM((3, s, LANES), F32)],
        ),
        out_shape=jax.ShapeDtypeStruct((bn, s, LANES), BF16),
        compiler_params=pltpu.CompilerParams(dimension_semantics=("arbitrary",),
                                             vmem_limit_bytes=VMEM_LIMIT),
        name="dil_attn",
    )(table, qkv_dil, buckets)


def _swa_kernel(table_ref, sink_ref, qkv_ref, bucket_ref, o_ref, bias_ref):
    s_len = o_ref.shape[1]
    k_col = N_SWA_PAIRS * LANES
    v_col = k_col + LANES
    key_row = lax.broadcasted_iota(jnp.int32, (2 * BLK, LANES), 0)

    @pl.when(pl.program_id(0) == 0)
    def _():
        for slot, hd in enumerate(_SWA_HEAD_ORDER):
            normal, first = _band_bias_tiles(bucket_ref[0], table_ref, H_DIL + hd, SWA_WINDOW - 1, sink_ref[hd])
            bias_ref[slot, 0] = normal
            bias_ref[slot, 1] = first

    def scores(i):
        start = pl.multiple_of(i * BLK, BLK)
        prev = pl.multiple_of(jnp.maximum(i - 1, 0) * BLK, BLK)
        kcat = jnp.concatenate([qkv_ref[0, pl.ds(prev, BLK), k_col:k_col + LANES],
                                qkv_ref[0, pl.ds(start, BLK), k_col:k_col + LANES]], axis=0)
        vcat = jnp.concatenate([qkv_ref[0, pl.ds(prev, BLK), v_col:v_col + LANES],
                                qkv_ref[0, pl.ds(start, BLK), v_col:v_col + LANES]], axis=0)
        kcat = jnp.where(key_row == 0, jnp.zeros_like(kcat), kcat)
        vcat = jnp.where(key_row == 0, jnp.zeros_like(vcat), vcat)
        variant = jnp.where(i > 0, 0, 1)
        out = []
        for p in range(N_SWA_PAIRS):
            q = qkv_ref[0, pl.ds(start, BLK), p * LANES:(p + 1) * LANES]
            biases = [bias_ref[2 * p + h, variant] for h in range(2)]
            out.append((start, p, _band_scores(q, kcat, biases), vcat))
        return out

    def body(it, carry):
        staged1 = [unit for u in range(SWA_UNROLL) for unit in scores(it * SWA_UNROLL + u)]
        staged2 = [(start, p, _band_pv(s, vcat)) for start, p, s, vcat in staged1]
        for start, p, (outs, ms) in staged2:
            num, den, _ = _band_finish(outs, ms)
            o_ref[0, pl.ds(start, BLK), p * LANES:(p + 1) * LANES] = (num * (1.0 / den)).astype(o_ref.dtype)
        return carry

    lax.fori_loop(0, s_len // BLK // SWA_UNROLL, body, 0)


def _swa_attention(qkv_swa, bucket, table, sinks_l):
    bn, s, n = qkv_swa.shape
    return pl.pallas_call(
        _swa_kernel,
        grid_spec=pltpu.PrefetchScalarGridSpec(
            num_scalar_prefetch=2,
            grid=(bn,),
            in_specs=[pl.BlockSpec((1, s, n), lambda b, tb, sk: (b, 0, 0)),
                      pl.BlockSpec(bucket.shape, lambda b, tb, sk: (0, 0, 0))],
            out_specs=pl.BlockSpec((1, s, N_SWA_PAIRS * LANES), lambda b, tb, sk: (b, 0, 0)),
            scratch_shapes=[pltpu.VMEM((H_SWA_Q, 2, BLK, 2 * BLK), F32)],
        ),
        out_shape=jax.ShapeDtypeStruct((bn, s, N_SWA_PAIRS * LANES), BF16),
        compiler_params=pltpu.CompilerParams(dimension_semantics=("arbitrary",),
                                             vmem_limit_bytes=VMEM_LIMIT),
        name="swa_attn",
    )(table, sinks_l, qkv_swa, bucket)


def _merge_kernel(x_ref, h_ref, mod_ref, osb_ref, odil_ref, oswa_ref,
                  win_hbm, wsb_ref, wdil_ref, wswa_ref, wout_ref, o_ref, wgate_ref, stage_ref, sem_ref, *, layer):
    d = x_ref.shape[2]
    gate0 = win_hbm.shape[2] - 3 * d
    chunks_per_gate = d // W_COLS
    branches = ((osb_ref, wsb_ref), (odil_ref, wdil_ref), (oswa_ref, wswa_ref))
    first = (pl.program_id(0) == 0) & (pl.program_id(1) == 0)

    def compute(stream_gates):
        if stream_gates:
            prime, take = _column_streamer(win_hbm.at[layer], gate0, 3 * chunks_per_gate, stage_ref, sem_ref)
            prime()
        x = x_ref[0]
        h = h_ref[0]
        merged = None
        for t, (o_br, w_br) in enumerate(branches):
            if stream_gates:
                for c in range(t * chunks_per_gate, (t + 1) * chunks_per_gate):
                    def consume(st, c=c):
                        wgate_ref[:, c * W_COLS:(c + 1) * W_COLS] = st[...].astype(BF16)
                    take(c, consume)
            gate = jax.nn.sigmoid(jnp.dot(h, wgate_ref[:, t * d:(t + 1) * d], preferred_element_type=F32))
            term = gate * jnp.dot(o_br[0], w_br[...], preferred_element_type=F32)
            merged = term if merged is None else merged + term
        y = jnp.dot(merged.astype(BF16), wout_ref[...], preferred_element_type=F32)
        o_ref[0] = x + mod_ref[0][5:6] * y

    @pl.when(first)
    def _():
        compute(True)

    @pl.when(jnp.logical_not(first))
    def _():
        compute(False)


def _merge(x, h, mod, o_sb, o_dil, o_swa, w_in, w_sb, w_dil, w_swa, w_out, layer):
    bn, s, d = x.shape
    row = lambda n: pl.BlockSpec((1, ROW_TILE, n), lambda b, i: (b, i, 0))
    return pl.pallas_call(
        functools.partial(_merge_kernel, layer=layer),
        grid=(bn, s // ROW_TILE),
        in_specs=[row(d), row(d),
                  pl.BlockSpec((1, ADA_CHUNKS, d), lambda b, i: (b, 0, 0)),
                  row(o_sb.shape[2]), row(o_dil.shape[2]), row(o_swa.shape[2]),
                  pl.BlockSpec(memory_space=pl.ANY), _slab_spec(w_sb, layer), _slab_spec(w_dil, layer),
                  _const_spec(w_swa.shape), _slab_spec(w_out, layer)],
        out_specs=row(d),
        out_shape=jax.ShapeDtypeStruct(x.shape, F32),
        scratch_shapes=[pltpu.VMEM((d, 3 * d), BF16), pltpu.VMEM((2, d, W_COLS), F32),
                        pltpu.SemaphoreType.DMA((2,))],
        compiler_params=pltpu.CompilerParams(dimension_semantics=("arbitrary", "arbitrary"),
                                             vmem_limit_bytes=VMEM_LIMIT),
        name="merge",
    )(x, h, mod, o_sb, o_dil, o_swa, w_in, w_sb, w_dil, w_swa, w_out)


def _t5_bucket(n, dtype):
    max_exact = N_BUCKETS // 2
    nf = np.maximum(n, 1).astype(dtype)
    scaled = np.log(nf / dtype(max_exact)) / dtype(math.log(MAX_REL_DIST / max_exact)) * dtype(N_BUCKETS - max_exact)
    large = np.minimum(max_exact + scaled.astype(np.int32), N_BUCKETS - 1)
    return np.where(n < max_exact, n, large).astype(np.int32)


def _band_buckets(dilation):
    rel = np.arange(BLK)[:, None] + BLK - np.arange(2 * BLK)[None, :]
    n = np.maximum(rel, 0) * dilation
    buckets = _t5_bucket(n, np.float32)
    assert np.array_equal(buckets, _t5_bucket(n, np.float64)), "relative-position bucket on a rounding boundary"
    return buckets


def _reorder_heads(w, order, axis):
    parts = [lax.slice_in_dim(w, hd * HEAD_DIM, (hd + 1) * HEAD_DIM, axis=axis) for hd in order]
    return jnp.concatenate(parts, axis=axis)


def kernel(x, c, w_ada, b_ada, norm_gain, w_ffn_gate, w_ffn_up, w_ffn_down, w_in,
           w_br_sb, w_br_dil, w_br_swa, w_out, sinks, rel_bias, final_gain):
    depth = w_ada.shape[0]
    bn, s, d = x.shape
    mods = _ada(c, w_ada, b_ada).reshape(depth, bn, ADA_CHUNKS, d)

    buckets = jnp.asarray(np.stack([_band_buckets(dil) for _, dil in DIL_PATTERNS]))
    table = rel_bias.reshape(-1)
    wg, wu, wd = w_ffn_gate, w_ffn_up, w_ffn_down
    w_sb_bf, w_dil_bf, w_out_bf = w_br_sb.astype(BF16), w_br_dil.astype(BF16), w_out.astype(BF16)

    for l in range(depth):
        mod = mods[l]
        x = _ffn(x, mod, norm_gain[l, 0], wg, wu, wd, l, 0, 0)

        qkv_sb, qkv_dil, qkv_swa, h_mix = _inproj(x, mod, norm_gain[l, 1], w_in, l)
        o_sb = _sb_attention(qkv_sb)
        o_dil = _dil_attention(qkv_dil, buckets, table)
        o_swa = _swa_attention(qkv_swa, buckets[:1], table, sinks[l])
        w_swa = _reorder_heads(w_br_swa[l], _SWA_HEAD_ORDER, 0).astype(BF16)
        x = _merge(x, h_mix, mod, o_sb, o_dil, o_swa, w_in, w_sb_bf, w_dil_bf, w_swa, w_out_bf, l)

        last = l == depth - 1
        x = _ffn(x, mod, norm_gain[l, 2], wg, wu, wd, l, 1, 2, final_gain=final_gain if last else None)
    return x
```

```python
import functools
import math

import numpy as np

import jax
import jax.numpy as jnp
from jax import lax
from jax.experimental import pallas as pl
from jax.experimental.pallas import tpu as pltpu

F32 = jnp.float32
BF16 = jnp.bfloat16

HEAD_DIM = 64
LANES = 128
BLK = 128
H_SB = 4
DIL_PATTERNS = ((128, 1), (512, 4), (2048, 16))
H_PER_DIL = 2
H_DIL = H_PER_DIL * len(DIL_PATTERNS)
H_SWA_Q = 6
H_SWA_KV = 2
SWA_WINDOW = 128
N_SOFT_HEADS = H_DIL + H_SWA_Q
N_SWA_PAIRS = H_SWA_Q // 2
N_BUCKETS = 32
MAX_REL_DIST = 2048
RMS_EPS = 1e-6
ADA_CHUNKS = 9
NEG_BIG = -1e30
LOG2E = 1.4426950408889634

VMEM_LIMIT = 56 * 1024 * 1024
ROW_TILE = 512
FF_CHUNK = 256
ADA_TILE = 3072
SB_ROWS = 512
SB_KEY_BLOCKS = 2
SB_DEAD_LOG2 = -150.0
BAND_UNROLL = 16
DIL_MERGE_ROWS = 512
MAX_STRIDE = 4
SWA_UNROLL = 16

_SWA_HEAD_ORDER = tuple(p + (H_SWA_Q // H_SWA_KV) * h for p in range(N_SWA_PAIRS) for h in range(2))


def _cparams(n_axes):
    return pltpu.CompilerParams(dimension_semantics=("parallel",) * n_axes,
                                vmem_limit_bytes=VMEM_LIMIT)


def _const_spec(shape):
    nd = len(shape)
    return pl.BlockSpec(shape, lambda *_: (0,) * nd, pipeline_mode=pl.Buffered(1))


def _slab_spec(arr, *lead):
    block = (None,) * len(lead) + tuple(arr.shape[len(lead):])
    return pl.BlockSpec(block, lambda *_: tuple(lead) + (0, 0), pipeline_mode=pl.Buffered(1))


def _ada_kernel(c_ref, w_ref, b_ref, o_ref):
    c = c_ref[...]
    sc = (c * jax.nn.sigmoid(c)).astype(BF16)
    o_ref[0] = jnp.dot(sc, w_ref[0].astype(BF16), preferred_element_type=F32) + b_ref[0]


def _ada(c, w_ada, b_ada):
    depth, d, n = w_ada.shape
    bn = c.shape[0]
    return pl.pallas_call(
        _ada_kernel,
        grid=(depth, n // ADA_TILE),
        in_specs=[pl.BlockSpec((bn, d), lambda l, j: (0, 0)),
                  pl.BlockSpec((1, d, ADA_TILE), lambda l, j: (l, 0, j)),
                  pl.BlockSpec((1, 1, ADA_TILE), lambda l, j: (l, 0, j))],
        out_specs=pl.BlockSpec((1, bn, ADA_TILE), lambda l, j: (l, 0, j)),
        out_shape=jax.ShapeDtypeStruct((depth, bn, n), F32),
        compiler_params=_cparams(2),
        name="ada_mod",
    )(c, w_ada, b_ada.reshape(depth, 1, n))


def _rms(x, gain):
    ms = jnp.mean(x * x, axis=-1, keepdims=True)
    return x * lax.rsqrt(ms + RMS_EPS) * gain


def _prenorm(x, gain, mod, j):
    shift = mod[3 * j:3 * j + 1]
    scale = mod[3 * j + 1:3 * j + 2]
    ms = jnp.mean(x * x, axis=-1, keepdims=True)
    return (x * lax.rsqrt(ms + RMS_EPS)) * (gain * (1.0 + scale)) + shift


def _dot_nt(a, b):
    return lax.dot_general(a, b, (((1,), (1,)), ((), ())), preferred_element_type=F32)


def _ffn_kernel(*refs, j, final, layer, which):
    n_in = 7 if final else 6
    x_ref, mod_ref, gain_ref, wg_hbm, wu_hbm, wd_hbm = refs[:6]
    fg_ref = refs[6] if final else None
    o_ref, wg_ref, wu_ref, wd_ref, stage_g_ref, stage_u_ref, stage_d_ref, sem_ref = refs[n_in:]
    n_chunks = wg_ref.shape[1] // FF_CHUNK
    chunk = lambda ci: slice(ci * FF_CHUNK, (ci + 1) * FF_CHUNK)
    first = (pl.program_id(0) == 0) & (pl.program_id(1) == 0)

    def chunk_copies(ci):
        slot = ci % 2
        return [pltpu.make_async_copy(wg_hbm.at[layer, which, :, chunk(ci)], stage_g_ref.at[slot], sem_ref.at[slot, 0]),
                pltpu.make_async_copy(wu_hbm.at[layer, which, :, chunk(ci)], stage_u_ref.at[slot], sem_ref.at[slot, 1]),
                pltpu.make_async_copy(wd_hbm.at[layer, which, chunk(ci), :], stage_d_ref.at[slot], sem_ref.at[slot, 2])]

    def compute(stream_weights):
        if stream_weights:
            for ci in range(min(2, n_chunks)):
                for cp in chunk_copies(ci):
                    cp.start()
        x = x_ref[0]
        mod = mod_ref[0]
        h = _prenorm(x, gain_ref[...], mod, j).astype(BF16)
        acc = jnp.zeros(x.shape, F32)
        for ci in range(n_chunks):
            if stream_weights:
                for cp in chunk_copies(ci):
                    cp.wait()
                wg_ref[:, chunk(ci)] = stage_g_ref[ci % 2].astype(BF16)
                wu_ref[:, chunk(ci)] = stage_u_ref[ci % 2].astype(BF16)
                wd_ref[chunk(ci), :] = stage_d_ref[ci % 2].astype(BF16)
                if ci + 2 < n_chunks:
                    for cp in chunk_copies(ci + 2):
                        cp.start()
            g = jnp.dot(h, wg_ref[:, chunk(ci)], preferred_element_type=F32)
            u = jnp.dot(h, wu_ref[:, chunk(ci)], preferred_element_type=F32)
            a = (g * jax.nn.sigmoid(g) * u).astype(BF16)
            acc = acc + jnp.dot(a, wd_ref[chunk(ci), :], preferred_element_type=F32)
        y = x + (0.5 * mod[3 * j + 2:3 * j + 3]) * acc
        if final:
            y = _rms(y, fg_ref[...])
        o_ref[0] = y

    @pl.when(first)
    def _():
        compute(True)

    @pl.when(jnp.logical_not(first))
    def _():
        compute(False)


def _ffn(x, mod, gain, wg, wu, wd, layer, which, j, final_gain=None):
    bn, s, d = x.shape
    d_ff = wg.shape[-1]
    final = final_gain is not None
    hbm = pl.BlockSpec(memory_space=pl.ANY)
    in_specs = [pl.BlockSpec((1, ROW_TILE, d), lambda b, i: (b, i, 0)),
                pl.BlockSpec((1, ADA_CHUNKS, d), lambda b, i: (b, 0, 0)),
                _const_spec((1, d)), hbm, hbm, hbm]
    args = [x, mod, gain.reshape(1, d), wg, wu, wd]
    if final:
        in_specs.append(_const_spec((1, d)))
        args.append(final_gain.reshape(1, d))
    return pl.pallas_call(
        functools.partial(_ffn_kernel, j=j, final=final, layer=layer, which=which),
        grid=(bn, s // ROW_TILE),
        in_specs=in_specs,
        out_specs=pl.BlockSpec((1, ROW_TILE, d), lambda b, i: (b, i, 0)),
        out_shape=jax.ShapeDtypeStruct(x.shape, F32),
        scratch_shapes=[pltpu.VMEM((d, d_ff), BF16), pltpu.VMEM((d, d_ff), BF16), pltpu.VMEM((d_ff, d), BF16),
                        pltpu.VMEM((2, d, FF_CHUNK), F32), pltpu.VMEM((2, d, FF_CHUNK), F32),
                        pltpu.VMEM((2, FF_CHUNK, d), F32), pltpu.SemaphoreType.DMA((2, 3))],
        compiler_params=pltpu.CompilerParams(dimension_semantics=("arbitrary", "arbitrary"),
                                             vmem_limit_bytes=VMEM_LIMIT),
        name="ffn_final" if final else f"ffn{j}",
    )(*args)


N_SB = 3 * H_SB * HEAD_DIM
N_DIL = 3 * H_DIL * HEAD_DIM
N_SWA = (H_SWA_Q + 2 * H_SWA_KV) * HEAD_DIM
N_QKV = N_SB + N_DIL + N_SWA
N_DIL_SLABS = N_DIL // LANES


W_COLS = 256
N_INPROJ_HEAD = 7 * W_COLS


def _column_streamer(w_hbm, first_col, n_chunks, stage_ref, sem_ref):
    def copy(c):
        return pltpu.make_async_copy(w_hbm.at[:, pl.ds(first_col + c * W_COLS, W_COLS)],
                                     stage_ref.at[c % 2], sem_ref.at[c % 2])

    def prime():
        for c in range(min(2, n_chunks)):
            copy(c).start()

    def take(c, consume):
        copy(c).wait()
        consume(stage_ref.at[c % 2])
        if c + 2 < n_chunks:
            copy(c + 2).start()

    return prime, take


def _inproj_tail_col(col):
    q0 = N_SB + N_DIL
    q1 = q0 + H_SWA_Q * HEAD_DIM
    if col < q0:
        return col - N_INPROJ_HEAD
    if col < q1:
        return (q0 - N_INPROJ_HEAD) + _SWA_HEAD_ORDER.index((col - q0) // HEAD_DIM) * HEAD_DIM
    return col - N_INPROJ_HEAD


def _inproj_kernel(x_ref, mod_ref, gain_ref, w_hbm, sb_ref, dil_ref, swa_ref, hout_ref,
                   whead_ref, wtail_ref, stage_ref, sem_ref, *, layer):
    qs = HEAD_DIM ** -0.5
    n_q_sb = H_SB * HEAD_DIM
    n_groups = len(DIL_PATTERNS)
    n_q_swa = H_SWA_Q * HEAD_DIM
    q0 = N_SB + N_DIL
    n_head_chunks = N_INPROJ_HEAD // W_COLS
    first = (pl.program_id(0) == 0) & (pl.program_id(1) == 0)

    def emit(h, r):
        hout_ref[0] = h
        sb_ref[0, :, :n_q_sb] = (r[:, :n_q_sb] * (qs * LOG2E)).astype(BF16)
        sb_ref[0, :, n_q_sb:] = r[:, n_q_sb:N_SB].astype(BF16)
        for a in range(3):
            for g in range(n_groups):
                c0 = N_SB + (a * n_groups + g) * LANES
                slab = r[:, c0:c0 + LANES]
                dil_ref[0, 3 * g + a] = slab * qs if a == 0 else slab
        swa_ref[0, :, :n_q_swa] = (r[:, q0:q0 + n_q_swa] * qs).astype(BF16)
        swa_ref[0, :, n_q_swa:] = r[:, q0 + n_q_swa:].astype(BF16)

    @pl.when(first)
    def _():
        prime, take = _column_streamer(w_hbm.at[layer], 0, N_QKV // W_COLS, stage_ref, sem_ref)
        prime()
        h = _prenorm(x_ref[0], gain_ref[...], mod_ref[0], 1).astype(BF16)
        parts = []
        for c in range(N_QKV // W_COLS):
            cols = slice(c * W_COLS, (c + 1) * W_COLS)
            if c < n_head_chunks:
                def consume(st, cols=cols):
                    whead_ref[:, cols] = st[...].astype(BF16)
                take(c, consume)
                parts.append(jnp.dot(h, whead_ref[:, cols], preferred_element_type=F32))
            else:
                def consume(st, c=c):
                    for piece in range(W_COLS // HEAD_DIM):
                        dst = _inproj_tail_col(c * W_COLS + piece * HEAD_DIM)
                        wtail_ref[:, dst:dst + HEAD_DIM] = \
                            st[:, piece * HEAD_DIM:(piece + 1) * HEAD_DIM].astype(BF16)
                take(c, consume)
        parts.append(jnp.dot(h, wtail_ref[...], preferred_element_type=F32))
        emit(h, jnp.concatenate(parts, axis=1))

    @pl.when(jnp.logical_not(first))
    def _():
        h = _prenorm(x_ref[0], gain_ref[...], mod_ref[0], 1).astype(BF16)
        emit(h, jnp.concatenate([jnp.dot(h, whead_ref[...], preferred_element_type=F32),
                                 jnp.dot(h, wtail_ref[...], preferred_element_type=F32)], axis=1))


def _inproj(x, mod, gain, w_in, layer):
    bn, s, d = x.shape
    nt = s // ROW_TILE
    n_tail = N_QKV - N_INPROJ_HEAD
    return pl.pallas_call(
        functools.partial(_inproj_kernel, layer=layer),
        grid=(bn, nt),
        in_specs=[pl.BlockSpec((1, ROW_TILE, d), lambda b, i: (b, i, 0)),
                  pl.BlockSpec((1, ADA_CHUNKS, d), lambda b, i: (b, 0, 0)),
                  _const_spec((1, d)), pl.BlockSpec(memory_space=pl.ANY)],
        out_specs=[pl.BlockSpec((1, ROW_TILE, N_SB), lambda b, i: (b, i, 0)),
                   pl.BlockSpec((1, N_DIL_SLABS, ROW_TILE, LANES), lambda b, i: (b, 0, i, 0)),
                   pl.BlockSpec((1, ROW_TILE, N_SWA), lambda b, i: (b, i, 0)),
                   pl.BlockSpec((1, ROW_TILE, d), lambda b, i: (b, i, 0))],
        out_shape=[jax.ShapeDtypeStruct((bn, s, N_SB), BF16),
                   jax.ShapeDtypeStruct((bn, N_DIL_SLABS, s, LANES), F32),
                   jax.ShapeDtypeStruct((bn, s, N_SWA), BF16),
                   jax.ShapeDtypeStruct((bn, s, d), BF16)],
        scratch_shapes=[pltpu.VMEM((d, N_INPROJ_HEAD), BF16), pltpu.VMEM((d, n_tail), BF16),
                        pltpu.VMEM((2, d, W_COLS), F32), pltpu.SemaphoreType.DMA((2,))],
        compiler_params=pltpu.CompilerParams(dimension_semantics=("arbitrary", "arbitrary"),
                                             vmem_limit_bytes=VMEM_LIMIT),
        name="in_proj",
    )(x, mod, gain.reshape(1, d), w_in)


def _sb_kernel(q_ref, k_ref, v_ref, o_ref, acc_ref, c_ref):
    s_len = q_ref.shape[1]
    n_pairs = q_ref.shape[2] // LANES
    per = SB_ROWS // BLK
    lane = lax.broadcasted_iota(jnp.int32, (BLK, LANES), 1)
    head0 = lane < HEAD_DIM
    row2 =lax.broadcasted_iota(jnp.int32, (BLK, 2 * BLK), 0)
    col2 = lax.broadcasted_iota(jnp.int32, (BLK, 2 * BLK), 1)
    strict2 = (col2 & (BLK - 1)) < row2
    rk = lax.broadcasted_iota(jnp.int32, (BLK, 2 * BLK), 0)
    cn = lax.broadcasted_iota(jnp.int32, (BLK, 2 * BLK), 1)
    neg_suffix_ones = jnp.where((rk >= cn) | (cn >= BLK), -1.0, 0.0).astype(BF16)

    def by_head(t):
        zero = jnp.zeros_like(t)
        return jnp.concatenate([jnp.where(head0, t, zero), jnp.where(head0, zero, t)], axis=0)

    def mask_first_block(t):
        first = jnp.where(strict2, t[:BLK], 0.0)
        return first if t.shape[0] == BLK else jnp.concatenate([first, t[BLK:]], axis=0)

    def pair_lanes(p):
        return slice(p * LANES, (p + 1) * LANES)

    def scores(p, q, k0):
        return _dot_nt(q, by_head(k_ref[0, pl.ds(k0, BLK), pair_lanes(p)]))

    def carries(z, diag):
        softplus2 = jnp.maximum(z, 0.0) + jnp.log2(1.0 + jnp.exp2(-jnp.abs(z)))
        if diag:
            softplus2 = mask_first_block(softplus2)
        sp = softplus2.astype(BF16)
        return [jnp.dot(sp[:, h * BLK:(h + 1) * BLK], neg_suffix_ones, preferred_element_type=F32)
                for h in range(2)]

    def weighted_values(p, z, cts, c, k0, diag):
        ws = [jnp.exp2(z[:, h * BLK:(h + 1) * BLK] + cts[h][:, :BLK] + c[h]) for h in range(2)]
        w = jnp.concatenate(ws, axis=1)
        if diag:
            w = mask_first_block(w)
        vv = by_head(v_ref[0, pl.ds(k0, BLK), pair_lanes(p)])
        return jnp.dot(w.astype(BF16), vv, preferred_element_type=F32)

    def superblock(sb, carry):
        base = pl.multiple_of(sb * SB_ROWS, SB_ROWS)
        acc_ref[...] = jnp.zeros_like(acc_ref)
        c_ref[...] = jnp.zeros_like(c_ref)

        def run(units):
            zs = [scores(p, q_ref[0, pl.ds(base + lo, n), pair_lanes(p)], k0) for p, lo, n, k0, _ in units]
            cts = [carries(z, unit[4]) for z, unit in zip(zs, units)]
            for z, ct, (p, lo, n, k0, diag) in zip(zs, cts, units):
                rows = slice(lo, lo + n)
                c = (c_ref[p, 0, rows, :], c_ref[p, 1, rows, :])
                acc_ref[p, rows, :] += weighted_values(p, z, ct, c, k0, diag)
                c_ref[p, 0, rows, :] = c[0] + ct[0][:, BLK:]
                c_ref[p, 1, rows, :] = c[1] + ct[1][:, BLK:]

        def alive(lo, n):
            c_max = None
            for p in range(n_pairs):
                for h in range(2):
                    c = c_ref[p, h, lo:lo + n, :]
                    c_max = c if c_max is None else jnp.maximum(c_max, c)
            return jnp.max(c_max) > SB_DEAD_LOG2

        run([(p, kq * BLK, SB_ROWS - kq * BLK, base + kq * BLK, True)
             for kq in reversed(range(per)) for p in range(n_pairs)])

        n_iter = sb * (per // SB_KEY_BLOCKS)

        half = SB_ROWS // 2

        def sweep(window, flag, state):
            def kstep(st):
                run([(p, window[0], window[1],
                      pl.multiple_of(base - (SB_KEY_BLOCKS * st[0] + u + 1) * BLK, BLK), False)
                     for u in range(SB_KEY_BLOCKS) for p in range(n_pairs)])
                return st[0] + 1, alive(half, half), alive(0, half)

            return lax.while_loop(lambda st: jnp.logical_and(st[0] < n_iter, st[flag]), kstep, state)

        state = sweep((0, SB_ROWS), 1, (jnp.int32(0), alive(half, half), alive(0, half)))
        sweep((0, half), 2, state)
        for p in range(n_pairs):
            o_ref[0, pl.ds(base, SB_ROWS), pair_lanes(p)] = acc_ref[p].astype(o_ref.dtype)
        return carry

    lax.fori_loop(0, s_len // SB_ROWS, superblock, 0)


def _sb_attention(qkv_sb):
    bn, s, _ = qkv_sb.shape
    n_pairs = H_SB // 2
    width = n_pairs * LANES
    spec = lambda t: pl.BlockSpec((1, s, width), lambda b: (b, 0, t))
    return pl.pallas_call(
        _sb_kernel,
        grid=(bn,),
        in_specs=[spec(0), spec(1), spec(2)],
        out_specs=pl.BlockSpec((1, s, width), lambda b: (b, 0, 0)),
        out_shape=jax.ShapeDtypeStruct((bn, s, width), BF16),
        scratch_shapes=[pltpu.VMEM((n_pairs, SB_ROWS, LANES), F32),
                        pltpu.VMEM((n_pairs, 2, SB_ROWS, LANES), F32)],
        compiler_params=_cparams(1),
        name="sb_attn",
    )(qkv_sb, qkv_sb, qkv_sb)


def _band_bias_tiles(bucket, table_ref, col, max_dist, sink=None):
    row = lax.broadcasted_iota(jnp.int32, (BLK, 2 * BLK), 0)
    kcol = lax.broadcasted_iota(jnp.int32, (BLK, 2 * BLK), 1)
    rel = row + BLK - kcol
    in_band = (rel >= 0) & (rel <= max_dist)
    bias = jnp.zeros((BLK, 2 * BLK), F32)
    for n in range(N_BUCKETS):
        bias = jnp.where(bucket == n, table_ref[n * N_SOFT_HEADS + col], bias)
    tiles = [jnp.where(in_band, bias, NEG_BIG), jnp.where(in_band & (kcol >= BLK), bias, NEG_BIG)]
    if sink is not None:
        tiles = [jnp.where(kcol == 0, sink, t) for t in tiles]
    return tiles


def _band_scores(q, kcat, biases):
    head0 = lax.broadcasted_iota(jnp.int32, q.shape, 1) < HEAD_DIM
    zero = jnp.zeros_like(q)
    return [_dot_nt(jnp.where(head0, q, zero) if h == 0 else jnp.where(head0, zero, q), kcat) + biases[h]
            for h in range(2)]


def _band_pv(scores, vcat):
    one = jnp.ones_like(vcat)
    v_head0 = lax.broadcasted_iota(jnp.int32, vcat.shape, 1) < HEAD_DIM
    outs, ms = [], []
    for h, s in enumerate(scores):
        vh = jnp.where(v_head0, vcat, one) if h == 0 else jnp.where(v_head0, one, vcat)
        m = jnp.max(s, axis=-1, keepdims=True)
        p = jnp.exp(s - m)
        outs.append(jnp.dot(p.astype(BF16), vh, preferred_element_type=F32))
        ms.append(m)
    return outs, ms


def _band_finish(outs, ms):
    head0 = lax.broadcasted_iota(jnp.int32, outs[0].shape, 1) < HEAD_DIM
    num = jnp.where(head0, outs[0], outs[1])
    den = pltpu.roll(jnp.where(head0, outs[1], outs[0]), HEAD_DIM, axis=1)
    return num, den, jnp.where(head0, ms[0], ms[1])


def _dil_kernel(table_ref, qkv_ref, bucket_ref, o_ref, bias_ref, og_ref, dg_ref, mg_ref, stage_ref, ostage_ref):
    s_len = o_ref.shape[1]
    n_blocks = s_len // BLK

    @pl.when(pl.program_id(0) == 0)
    def _():
        for g, (window, d) in enumerate(DIL_PATTERNS):
            for h in range(H_PER_DIL):
                head = g * H_PER_DIL + h
                normal, first = _band_bias_tiles(bucket_ref[g], table_ref, head, window // d)
                bias_ref[head, 0] = normal
                bias_ref[head, 1] = first

    for g, (window, d) in enumerate(DIL_PATTERNS):
        nb = s_len // d // BLK
        two_pass = d > MAX_STRIDE
        d_in = d // MAX_STRIDE if two_pass else d
        seg = s_len // MAX_STRIDE

        if two_pass:
            for c1 in range(MAX_STRIDE):
                for a in range(3):
                    stage_ref[a, c1 * seg:(c1 + 1) * seg, :] = \
                        qkv_ref[0, 3 * g + a, pl.ds(c1, seg, stride=MAX_STRIDE), :]

        def src(a, start):
            if d == 1:
                return qkv_ref[0, 3 * g + a, pl.ds(start, BLK), :]
            if not two_pass:
                return qkv_ref[0, 3 * g + a, pl.ds(start, BLK, stride=d), :]
            return stage_ref[a, pl.ds(staged(start), BLK, stride=d_in), :]

        def staged(start):
            return (start % MAX_STRIDE) * seg + start // MAX_STRIDE

        def put(dst_ref, ostage_slot, start, val):
            if d == 1:
                dst_ref[g, pl.ds(start, BLK), :] = val
            elif not two_pass:
                dst_ref[g, pl.ds(start, BLK, stride=d), :] = val
            else:
                ostage_ref[ostage_slot, pl.ds(staged(start), BLK, stride=d_in), :] = val

        def scores(idx):
            r, i = idx // nb, idx % nb
            start = r + d * BLK * i
            q = src(0, start).astype(BF16)
            heads = (g * H_PER_DIL, g * H_PER_DIL + 1)
            prev = r + d * BLK * jnp.maximum(i - 1, 0)
            kcat = jnp.concatenate([src(1, prev), src(1, start)], axis=0).astype(BF16)
            vcat = jnp.concatenate([src(2, prev), src(2, start)], axis=0).astype(BF16)
            variant = jnp.where(i > 0, 0, 1)
            biases = [bias_ref[hd, variant] for hd in heads]
            return start, _band_scores(q, kcat, biases), vcat

        def body(it, carry):
            staged1 = [scores(it * BAND_UNROLL + u) for u in range(BAND_UNROLL)]
            staged2 = [(start, _band_pv(s, vcat)) for start, s, vcat in staged1]
            for start, (outs, ms) in staged2:
                num, den, m = _band_finish(outs, ms)
                put(og_ref, 0, start, num)
                put(dg_ref, 1, start, den)
                put(mg_ref, 2, start, m)
            return carry

        lax.fori_loop(0, n_blocks // BAND_UNROLL, body, 0)

        if two_pass:
            for c1 in range(MAX_STRIDE):
                rows_in = slice(c1 * seg, (c1 + 1) * seg)
                rows_out = pl.ds(c1, seg, stride=MAX_STRIDE)
                og_ref[g, rows_out, :] = ostage_ref[0, rows_in, :]
                dg_ref[g, rows_out, :] = ostage_ref[1, rows_in, :]
                mg_ref[g, rows_out, :] = ostage_ref[2, rows_in, :]

    for r0 in range(0, s_len, DIL_MERGE_ROWS):
        sl = slice(r0, r0 + DIL_MERGE_ROWS)
        m0, m1, m2 = mg_ref[0, sl, :], mg_ref[1, sl, :], mg_ref[2, sl, :]
        m = jnp.maximum(jnp.maximum(m0, m1), m2)
        e0, e1, e2 = jnp.exp(m0 - m), jnp.exp(m1 - m), jnp.exp(m2 - m)
        num = e0 * og_ref[0, sl, :] + e1 * og_ref[1, sl, :] + e2 * og_ref[2, sl, :]
        den = e0 * dg_ref[0, sl, :] + e1 * dg_ref[1, sl, :] + e2 * dg_ref[2, sl, :]
        o_ref[0, sl, :] = (num / den).astype(o_ref.dtype)


def _dil_attention(qkv_dil, buckets, table):
    bn, n_slabs, s, _ = qkv_dil.shape
    n_groups = len(DIL_PATTERNS)
    return pl.pallas_call(
        _dil_kernel,
        grid_spec=pltpu.PrefetchScalarGridSpec(
            num_scalar_prefetch=1,
            grid=(bn,),
            in_specs=[pl.BlockSpec((1, n_slabs, s, LANES), lambda b, tb: (b, 0, 0, 0)),
                      pl.BlockSpec(buckets.shape, lambda b, tb: (0, 0, 0))],
            out_specs=pl.BlockSpec((1, s, LANES), lambda b, tb: (b, 0, 0)),
            scratch_shapes=[pltpu.VMEM((H_DIL, 2, BLK, 2 * BLK), F32),
                            pltpu.VMEM((n_groups, s, LANES), F32),
                            pltpu.VMEM((n_groups, s, LANES), F32),
                            pltpu.VMEM((n_groups, s, LANES), F32),
                            pltpu.VMEM((3, s, LANES), F32),
                            pltpu.VMEM((3, s, LANES), F32)],
        ),
        out_shape=jax.ShapeDtypeStruct((bn, s, LANES), BF16),
        compiler_params=pltpu.CompilerParams(dimension_semantics=("arbitrary",),
                                             vmem_limit_bytes=VMEM_LIMIT),
        name="dil_attn",
    )(table, qkv_dil, buckets)


def _swa_kernel(table_ref, sink_ref, qkv_ref, bucket_ref, o_ref, bias_ref):
    s_len = o_ref.shape[1]
    k_col = N_SWA_PAIRS * LANES
    v_col = k_col + LANES
    key_row = lax.broadcasted_iota(jnp.int32, (2 * BLK, LANES), 0)

    @pl.when(pl.program_id(0) == 0)
    def _():
        for slot, hd in enumerate(_SWA_HEAD_ORDER):
            normal, first = _band_bias_tiles(bucket_ref[0], table_ref, H_DIL + hd, SWA_WINDOW - 1, sink_ref[hd])
            bias_ref[slot, 0] = normal
            bias_ref[slot, 1] = first

    def scores(i):
        start = pl.multiple_of(i * BLK, BLK)
        prev = pl.multiple_of(jnp.maximum(i - 1, 0) * BLK, BLK)
        kcat = jnp.concatenate([qkv_ref[0, pl.ds(prev, BLK), k_col:k_col + LANES],
                                qkv_ref[0, pl.ds(start, BLK), k_col:k_col + LANES]], axis=0)
        vcat = jnp.concatenate([qkv_ref[0, pl.ds(prev, BLK), v_col:v_col + LANES],
                                qkv_ref[0, pl.ds(start, BLK), v_col:v_col + LANES]], axis=0)
        kcat = jnp.where(key_row == 0, jnp.zeros_like(kcat), kcat)
        vcat = jnp.where(key_row == 0, jnp.zeros_like(vcat), vcat)
        variant = jnp.where(i > 0, 0, 1)
        out = []
        for p in range(N_SWA_PAIRS):
            q = qkv_ref[0, pl.ds(start, BLK), p * LANES:(p + 1) * LANES]
            biases = [bias_ref[2 * p + h, variant] for h in range(2)]
            out.append((start, p, _band_scores(q, kcat, biases), vcat))
        return out

    def body(it, carry):
        staged1 = [unit for u in range(SWA_UNROLL) for unit in scores(it * SWA_UNROLL + u)]
        staged2 = [(start, p, _band_pv(s, vcat)) for start, p, s, vcat in staged1]
        for start, p, (outs, ms) in staged2:
            num, den, _ = _band_finish(outs, ms)
            o_ref[0, pl.ds(start, BLK), p * LANES:(p + 1) * LANES] = (num * (1.0 / den)).astype(o_ref.dtype)
        return carry

    lax.fori_loop(0, s_len // BLK // SWA_UNROLL, body, 0)


def _swa_attention(qkv_swa, bucket, table, sinks_l):
    bn, s, n = qkv_swa.shape
    return pl.pallas_call(
        _swa_kernel,
        grid_spec=pltpu.PrefetchScalarGridSpec(
            num_scalar_prefetch=2,
            grid=(bn,),
            in_specs=[pl.BlockSpec((1, s, n), lambda b, tb, sk: (b, 0, 0)),
                      pl.BlockSpec(bucket.shape, lambda b, tb, sk: (0, 0, 0))],
            out_specs=pl.BlockSpec((1, s, N_SWA_PAIRS * LANES), lambda b, tb, sk: (b, 0, 0)),
            scratch_shapes=[pltpu.VMEM((H_SWA_Q, 2, BLK, 2 * BLK), F32)],
        ),
        out_shape=jax.ShapeDtypeStruct((bn, s, N_SWA_PAIRS * LANES), BF16),
        compiler_params=pltpu.CompilerParams(dimension_semantics=("arbitrary",),
                                             vmem_limit_bytes=VMEM_LIMIT),
        name="swa_attn",
    )(table, sinks_l, qkv_swa, bucket)


def _merge_kernel(x_ref, h_ref, mod_ref, osb_ref, odil_ref, oswa_ref,
                  win_hbm, wsb_ref, wdil_ref, wswa_ref, wout_ref, o_ref, wgate_ref, stage_ref, sem_ref, *, layer):
    d = x_ref.shape[2]
    gate0 = win_hbm.shape[2] - 3 * d
    chunks_per_gate = d // W_COLS
    branches = ((osb_ref, wsb_ref), (odil_ref, wdil_ref), (oswa_ref, wswa_ref))
    first = (pl.program_id(0) == 0) & (pl.program_id(1) == 0)

    def compute(stream_gates):
        if stream_gates:
            prime, take = _column_streamer(win_hbm.at[layer], gate0, 3 * chunks_per_gate, stage_ref, sem_ref)
            prime()
        x = x_ref[0]
        h = h_ref[0]
        merged = None
        for t, (o_br, w_br) in enumerate(branches):
            if stream_gates:
                for c in range(t * chunks_per_gate, (t + 1) * chunks_per_gate):
                    def consume(st, c=c):
                        wgate_ref[:, c * W_COLS:(c + 1) * W_COLS] = st[...].astype(BF16)
                    take(c, consume)
            gate = jax.nn.sigmoid(jnp.dot(h, wgate_ref[:, t * d:(t + 1) * d], preferred_element_type=F32))
            term = gate * jnp.dot(o_br[0], w_br[...], preferred_element_type=F32)
            merged = term if merged is None else merged + term
        y = jnp.dot(merged.astype(BF16), wout_ref[...], preferred_element_type=F32)
        o_ref[0] = x + mod_ref[0][5:6] * y

    @pl.when(first)
    def _():
        compute(True)

    @pl.when(jnp.logical_not(first))
    def _():
        compute(False)


def _merge(x, h, mod, o_sb, o_dil, o_swa, w_in, w_sb, w_dil, w_swa, w_out, layer):
    bn, s, d = x.shape
    row = lambda n: pl.BlockSpec((1, ROW_TILE, n), lambda b, i: (b, i, 0))
    return pl.pallas_call(
        functools.partial(_merge_kernel, layer=layer),
        grid=(bn, s // ROW_TILE),
        in_specs=[row(d), row(d),
                  pl.BlockSpec((1, ADA_CHUNKS, d), lambda b, i: (b, 0, 0)),
                  row(o_sb.shape[2]), row(o_dil.shape[2]), row(o_swa.shape[2]),
                  pl.BlockSpec(memory_space=pl.ANY), _slab_spec(w_sb, layer), _slab_spec(w_dil, layer),
                  _const_spec(w_swa.shape), _slab_spec(w_out, layer)],
        out_specs=row(d),
        out_shape=jax.ShapeDtypeStruct(x.shape, F32),
        scratch_shapes=[pltpu.VMEM((d, 3 * d), BF16), pltpu.VMEM((2, d, W_COLS), F32),
                        pltpu.SemaphoreType.DMA((2,))],
        compiler_params=pltpu.CompilerParams(dimension_semantics=("arbitrary", "arbitrary"),
                                             vmem_limit_bytes=VMEM_LIMIT),
        name="merge",
    )(x, h, mod, o_sb, o_dil, o_swa, w_in, w_sb, w_dil, w_swa, w_out)


def _t5_bucket(n, dtype):
    max_exact = N_BUCKETS // 2
    nf = np.maximum(n, 1).astype(dtype)
    scaled = np.log(nf / dtype(max_exact)) / dtype(math.log(MAX_REL_DIST / max_exact)) * dtype(N_BUCKETS - max_exact)
    large = np.minimum(max_exact + scaled.astype(np.int32), N_BUCKETS - 1)
    return np.where(n < max_exact, n, large).astype(np.int32)


def _band_buckets(dilation):
    rel = np.arange(BLK)[:, None] + BLK - np.arange(2 * BLK)[None, :]
    n = np.maximum(rel, 0) * dilation
    buckets = _t5_bucket(n, np.float32)
    assert np.array_equal(buckets, _t5_bucket(n, np.float64)), "relative-position bucket on a rounding boundary"
    return buckets


def _reorder_heads(w, order, axis):
    parts = [lax.slice_in_dim(w, hd * HEAD_DIM, (hd + 1) * HEAD_DIM, axis=axis) for hd in order]
    return jnp.concatenate(parts, axis=axis)


def kernel(x, c, w_ada, b_ada, norm_gain, w_ffn_gate, w_ffn_up, w_ffn_down, w_in,
           w_br_sb, w_br_dil, w_br_swa, w_out, sinks, rel_bias, final_gain):
    depth = w_ada.shape[0]
    bn, s, d = x.shape
    mods = _ada(c, w_ada, b_ada).reshape(depth, bn, ADA_CHUNKS, d)

    buckets = jnp.asarray(np.stack([_band_buckets(dil) for _, dil in DIL_PATTERNS]))
    table = rel_bias.reshape(-1)
    wg, wu, wd = w_ffn_gate, w_ffn_up, w_ffn_down
    w_sb_bf, w_dil_bf, w_out_bf = w_br_sb.astype(BF16), w_br_dil.astype(BF16), w_out.astype(BF16)

    for l in range(depth):
        mod = mods[l]
        x = _ffn(x, mod, norm_gain[l, 0], wg, wu, wd, l, 0, 0)

        qkv_sb, qkv_dil, qkv_swa, h_mix = _inproj(x, mod, norm_gain[l, 1], w_in, l)
        o_sb = _sb_attention(qkv_sb)
        o_dil = _dil_attention(qkv_dil, buckets, table)
        o_swa = _swa_attention(qkv_swa, buckets[:1], table, sinks[l])
        w_swa = _reorder_heads(w_br_swa[l], _SWA_HEAD_ORDER, 0).astype(BF16)
        x = _merge(x, h_mix, mod, o_sb, o_dil, o_swa, w_in, w_sb_bf, w_dil_bf, w_swa, w_out_bf, l)

        last = l == depth - 1
        x = _ffn(x, mod, norm_gain[l, 2], wg, wu, wd, l, 1, 2, final_gain=final_gain if last else None)
    return x
```
